```python
import jax, jax.numpy as jnp
from jax import lax
import numpy as np

D_MODEL = 1024
BATCH = 8
SEQ = 2048
DEPTH = 2

GRID_W = 64
CTX_LEN = 256
D_FF = ((8 * D_MODEL // 3 + 255) // 256) * 256
HALF_STEP = 0.5
N_MOD = 9
RMS_EPS = 1e-6
NEG_INF = -1e30

CONV_CH = D_MODEL // 4
CONV_K = 3
POOL_CH = D_MODEL // 4
POOL_WINDOWS = (2, 4, 8, 16)
POOL_GROUP = POOL_CH // 4
NA_HEAD_DIM = 64
NA_CH = D_MODEL // 2
NA_HEADS = NA_CH // NA_HEAD_DIM
NA_KH_MAX = 8
NA_KW = 16
NA_QC = 16
NA_KB = NA_QC + NA_KW - 1
D_MIX = CONV_CH + POOL_CH + NA_CH
OFF_B = CONV_CH
OFF_C = 2 * CONV_CH
OFF_P = 3 * CONV_CH
OFF_Q = 3 * CONV_CH + POOL_CH
OFF_K = OFF_Q + NA_CH
OFF_V = OFF_K + NA_CH
D_IN = OFF_V + NA_CH

kernel_name = 'hybrid_conv_pool_neighbourhood_macaron_dit'


def rms_norm(x, g):
    xf = x.astype(jnp.float32)
    y = xf * lax.rsqrt(jnp.mean(xf * xf, axis=-1, keepdims=True) + RMS_EPS)
    return (y * g.astype(jnp.float32)).astype(x.dtype)


def mod_norm(x, m, i, g):
    return rms_norm(x, g) * (1 + m[:, 3 * i + 1, None]) + m[:, 3 * i, None]


def swiglu(h, w1, w2):
    a, b = jnp.split(h @ w1, 2, axis=-1)
    return (jax.nn.silu(a) * b) @ w2


def ffn_sublayer(x, m, i, g, w1, w2):
    return x + HALF_STEP * m[:, 3 * i + 2, None] * swiglu(mod_norm(x, m, i, g), w1, w2)


def short_conv(u, w):
    return lax.conv_general_dilated(
        u, w[:, None, :].astype(u.dtype), window_strides=(1,),
        padding=[(CONV_K // 2, CONV_K // 2)],
        dimension_numbers=('NWC', 'WIO', 'NWC'), feature_group_count=u.shape[-1])


def gated_conv_mixer(h, b_gate, c_gate, conv_w):
    return b_gate * short_conv(c_gate * h, conv_w)


def multiscale_pool_mixer(v, pool_w, pool_scale):
    L = v.shape[1]
    vf = v.astype(jnp.float32)
    cs = jnp.concatenate([jnp.zeros_like(vf[:, :1]), jnp.cumsum(vf, axis=1)], axis=1)
    t = jnp.arange(L)
    outs = []
    for g, w in enumerate(POOL_WINDOWS):
        left = w // 2
        right = w - 1 - left
        lo = jnp.clip(t - left, 0, L)
        hi = jnp.clip(t + right + 1, 0, L)
        sl = slice(g * POOL_GROUP, (g + 1) * POOL_GROUP)
        mean = (cs[:, hi, sl] - cs[:, lo, sl]) / (hi - lo).astype(jnp.float32)[None, :, None]
        outs.append((mean - vf[..., sl]).astype(v.dtype) @ pool_w[g])
    return jnp.concatenate(outs, axis=-1) * pool_scale


def context_attention(q, k, v):
    B, L = q.shape[:2]
    s = jnp.einsum('bqhd,bkhd->bhqk', q, k, preferred_element_type=jnp.float32) * (NA_HEAD_DIM ** -0.5)
    p = jax.nn.softmax(s, axis=-1).astype(v.dtype)
    return jnp.einsum('bhqk,bkhd->bqhd', p, v).reshape(B, L, NA_CH)


def neighbourhood_attention(q, k, v, k_ctx, v_ctx, rpb):
    B, S = q.shape[:2]
    rows = S // GRID_W
    kh = min(NA_KH_MAX, rows)
    nj = GRID_W // NA_QC
    scale = NA_HEAD_DIM ** -0.5
    qg = q.reshape(B, rows, GRID_W, NA_HEADS, NA_HEAD_DIM)
    kg = k.reshape(B, rows, GRID_W, NA_HEADS, NA_HEAD_DIM)
    vg = v.reshape(B, rows, GRID_W, NA_HEADS, NA_HEAD_DIM)
    qcol = jnp.arange(GRID_W).reshape(nj, NA_QC)
    band0 = jnp.clip(jnp.arange(nj) * NA_QC - NA_KW // 2, 0, GRID_W - NA_KB)
    kcol = band0[:, None] + jnp.arange(NA_KB)
    cstart = jnp.clip(qcol - NA_KW // 2, 0, GRID_W - NA_KW)
    col_valid = (kcol[:, None, :] >= cstart[..., None]) & (kcol[:, None, :] < cstart[..., None] + NA_KW)
    col_off = kcol[:, None, :] - qcol[..., None] + NA_KW - 1
    n_loc = kh * NA_KB

    def row_block(r):
        rs = jnp.clip(r - kh // 2, 0, rows - kh)
        q_r = lax.dynamic_index_in_dim(qg, r, axis=1, keepdims=False).reshape(
            B, nj, NA_QC, NA_HEADS, NA_HEAD_DIM)
        k_b = lax.dynamic_slice_in_dim(kg, rs, kh, axis=1)[:, :, kcol]
        v_b = lax.dynamic_slice_in_dim(vg, rs, kh, axis=1)[:, :, kcol]
        row_off = rs + jnp.arange(kh) - r + NA_KH_MAX - 1
        bias = rpb[:, row_off[None, None, :, None], col_off[:, :, None, :]].astype(jnp.float32)
        bias = jnp.where(col_valid[None, :, :, None, :], bias, NEG_INF)
        s_loc = jnp.einsum('bjqhd,bijkhd->bhjqik', q_r, k_b,
                           preferred_element_type=jnp.float32) * scale + bias
        s_ctx = jnp.einsum('bjqhd,bchd->bhjqc', q_r, k_ctx,
                           preferred_element_type=jnp.float32) * scale
        s = jnp.concatenate([s_loc.reshape(B, NA_HEADS, nj, NA_QC, n_loc), s_ctx], axis=-1)
        p = jax.nn.softmax(s, axis=-1).astype(v.dtype)
        p_loc = p[..., :n_loc].reshape(B, NA_HEADS, nj, NA_QC, kh, NA_KB)
        o = (jnp.einsum('bhjqik,bijkhd->bjqhd', p_loc, v_b)
             + jnp.einsum('bhjqc,bchd->bjqhd', p[..., n_loc:], v_ctx))
        return o.reshape(B, GRID_W, NA_CH)

    out = lax.map(row_block, jnp.arange(rows))
    return out.transpose(1, 0, 2, 3).reshape(B, S, NA_CH)


def heads(t):
    return t.reshape(t.shape[0], t.shape[1], NA_HEADS, NA_HEAD_DIM)


def setup_inputs(seed: int = 0) -> dict:
    key = jax.random.key(seed)
    ks = jax.random.split(key, 16)
    f32 = jnp.float32
    nrm = lambda k, shape, s: jax.random.normal(k, shape, f32) * s
    return {
        'x': nrm(ks[0], (BATCH, SEQ, D_MODEL), 1.0),
        'c': nrm(ks[1], (BATCH, D_MODEL), 1.0),
        'ctx': nrm(ks[2], (BATCH, CTX_LEN, D_MODEL), 1.0),
        'c_ctx': nrm(ks[3], (D_MODEL,), 1.0),
        'w_mod': nrm(ks[4], (DEPTH, D_MODEL, N_MOD * D_MODEL), 0.5 * D_MODEL ** -0.5),
        'b_mod': nrm(ks[5], (DEPTH, N_MOD * D_MODEL), 0.01),
        'norm_g': 1.0 + nrm(ks[6], (DEPTH, 3, D_MODEL), 0.02),
        'ffn_w_in': nrm(ks[7], (DEPTH, 2, D_MODEL, 2 * D_FF), D_MODEL ** -0.5),
        'ffn_w_out': nrm(ks[8], (DEPTH, 2, D_FF, D_MODEL), D_FF ** -0.5),
        'w_in': nrm(ks[9], (DEPTH, D_MODEL, D_IN), D_MODEL ** -0.5),
        'conv_w': nrm(ks[10], (DEPTH, CONV_K, CONV_CH), CONV_K ** -0.5),
        'pool_w': nrm(ks[11], (DEPTH, len(POOL_WINDOWS), POOL_GROUP, POOL_GROUP), POOL_GROUP ** -0.5),
        'pool_scale': 1.0 + nrm(ks[12], (DEPTH, POOL_CH), 0.1),
        'rpb': nrm(ks[13], (DEPTH, NA_HEADS, 2 * NA_KH_MAX - 1, 2 * NA_KW - 1), 0.1),
        'w_out': nrm(ks[14], (DEPTH, D_MIX, D_MODEL), D_MIX ** -0.5),
        'final_g': 1.0 + nrm(ks[15], (D_MODEL,), 0.02),
    }


def reference(x, c, ctx, c_ctx, w_mod, b_mod, norm_g, ffn_w_in, ffn_w_out, w_in,
              conv_w, pool_w, pool_scale, rpb, w_out, final_g):
    B = x.shape[0]
    xc = ctx
    for l in range(DEPTH):
        last = l == DEPTH - 1
        m = (jax.nn.silu(c) @ w_mod[l] + b_mod[l]).reshape(B, N_MOD, D_MODEL)
        mc = (jax.nn.silu(c_ctx) @ w_mod[l] + b_mod[l]).reshape(1, N_MOD, D_MODEL)

        x = ffn_sublayer(x, m, 0, norm_g[l, 0], ffn_w_in[l, 0], ffn_w_out[l, 0])
        xc = ffn_sublayer(xc, mc, 0, norm_g[l, 0], ffn_w_in[l, 0], ffn_w_out[l, 0])

        h = mod_norm(x, m, 1, norm_g[l, 1])
        hc = mod_norm(xc, mc, 1, norm_g[l, 1])
        u = h @ w_in[l]
        if last:
            uc_kv = hc @ w_in[l][:, OFF_K:]
            kc, vc = heads(uc_kv[..., :NA_CH]), heads(uc_kv[..., NA_CH:])
        else:
            uc = hc @ w_in[l]
            kc, vc = heads(uc[..., OFF_K:OFF_V]), heads(uc[..., OFF_V:])
            yc = jnp.concatenate([
                gated_conv_mixer(uc[..., :OFF_B], uc[..., OFF_B:OFF_C], uc[..., OFF_C:OFF_P], conv_w[l]),
                multiscale_pool_mixer(uc[..., OFF_P:OFF_Q], pool_w[l], pool_scale[l]),
                context_attention(heads(uc[..., OFF_Q:OFF_K]), kc, vc),
            ], axis=-1) @ w_out[l]
            xc = xc + mc[:, 5, None] * yc
            xc = ffn_sublayer(xc, mc, 2, norm_g[l, 2], ffn_w_in[l, 1], ffn_w_out[l, 1])
        y = jnp.concatenate([
            gated_conv_mixer(u[..., :OFF_B], u[..., OFF_B:OFF_C], u[..., OFF_C:OFF_P], conv_w[l]),
            multiscale_pool_mixer(u[..., OFF_P:OFF_Q], pool_w[l], pool_scale[l]),
            neighbourhood_attention(heads(u[..., OFF_Q:OFF_K]), heads(u[..., OFF_K:OFF_V]),
                                    heads(u[..., OFF_V:]), kc, vc, rpb[l]),
        ], axis=-1) @ w_out[l]
        x = x + m[:, 5, None] * y

        x = ffn_sublayer(x, m, 2, norm_g[l, 2], ffn_w_in[l, 1], ffn_w_out[l, 1])
    return rms_norm(x, final_g)
```

```python
import functools

import numpy as np
import jax
import jax.numpy as jnp
from jax import lax
from jax.experimental import pallas as pl
from jax.experimental.pallas import tpu as pltpu

F32 = jnp.float32
BF16 = jnp.bfloat16

D_MODEL = 1024
GRID_W = 64
D_FF = 2816
N_MOD = 9
RMS_EPS = 1e-6
NEG_INF = -1e30
CONV_CH = D_MODEL // 4
POOL_CH = D_MODEL // 4
POOL_WINDOWS = (2, 4, 8, 16)
POOL_GROUP = POOL_CH // 4
NA_HEAD_DIM = 64
NA_CH = D_MODEL // 2
NA_HEADS = NA_CH // NA_HEAD_DIM
NA_KH_MAX = 8
NA_KW = 16
OFF_Q = 3 * CONV_CH + POOL_CH
D_IN = OFF_Q + 3 * NA_CH

FFN_TM = 512
FFN_CHUNKS = ((0, 1536), (1536, 2816))
TOK = 256
HALO = 8
QROWS = TOK // GRID_W
KROWS = QROWS + NA_KH_MAX
NKB = KROWS * GRID_W // TOK
MOD_ROWS = 16
MOD_TN = 2304
VMEM_LIMIT = 56 * 1024 * 1024


def _rms_mod(x, g, scale, shift):
    ms = jnp.mean(x * x, axis=-1, keepdims=True)
    return (x * lax.rsqrt(ms + RMS_EPS) * g) * (1.0 + scale) + shift


def _mod_kernel(c_ref, w_ref, b_ref, o_ref):
    c = c_ref[...]
    s = (c * jax.nn.sigmoid(c)).astype(BF16)
    o_ref[...] = jnp.dot(s, w_ref[...].astype(BF16), preferred_element_type=F32) + b_ref[...]


def _modulation(cc, w_mod, b_mod):
    depth = w_mod.shape[0]
    n = N_MOD * D_MODEL
    return pl.pallas_call(
        _mod_kernel,
        grid=(depth, n // MOD_TN),
        in_specs=[
            pl.BlockSpec((MOD_ROWS, D_MODEL), lambda l, j: (0, 0)),
            pl.BlockSpec((None, D_MODEL, MOD_TN), lambda l, j: (l, 0, j)),
            pl.BlockSpec((None, 1, MOD_TN), lambda l, j: (l, 0, j)),
        ],
        out_specs=pl.BlockSpec((None, MOD_ROWS, MOD_TN), lambda l, j: (l, 0, j)),
        out_shape=jax.ShapeDtypeStruct((depth, MOD_ROWS, n), F32),
        compiler_params=pltpu.CompilerParams(
            dimension_semantics=("arbitrary", "arbitrary"), vmem_limit_bytes=VMEM_LIMIT),
        name="modulation",
    )(cc, w_mod, b_mod.reshape(depth, 1, n))


def _ffn_kernel(x_ref, mod_ref, g_ref, w1_ref, w2_ref, fg_ref, o_ref, *, sub, final):
    x = x_ref[...]
    m = mod_ref[...]
    shift = m[3 * sub:3 * sub + 1]
    scale = m[3 * sub + 1:3 * sub + 2]
    gate = m[3 * sub + 2:3 * sub + 3]
    hb = _rms_mod(x, g_ref[...], scale, shift).astype(BF16)
    acc = None
    for lo, hi in FFN_CHUNKS:
        a = jnp.dot(hb, w1_ref[:, lo:hi], preferred_element_type=F32)
        b = jnp.dot(hb, w1_ref[:, D_FF + lo:D_FF + hi], preferred_element_type=F32)
        gg = (a * jax.nn.sigmoid(a) * b).astype(BF16)
        part = jnp.dot(gg, w2_ref[lo:hi, :], preferred_element_type=F32)
        acc = part if acc is None else acc + part
    y = x + (0.5 * gate) * acc
    if final:
        ms = jnp.mean(y * y, axis=-1, keepdims=True)
        y = y * lax.rsqrt(ms + RMS_EPS) * fg_ref[...]
    o_ref[...] = y


def _ffn(xs, n_tok, mod_l, g, w1, w2, final_g, *, sub, final, lat_tiles_per_batch, n_batch):
    const = lambda i: (0, 0)
    return pl.pallas_call(
        functools.partial(_ffn_kernel, sub=sub, final=final),
        grid=(n_tok // FFN_TM,),
        in_specs=[
            pl.BlockSpec((FFN_TM, D_MODEL), lambda i: (i, 0)),
            pl.BlockSpec((None, N_MOD, D_MODEL),
                         lambda i: (jnp.minimum(i // lat_tiles_per_batch, n_batch), 0, 0)),
            pl.BlockSpec((1, D_MODEL), const),
            pl.BlockSpec((D_MODEL, 2 * D_FF), const, pipeline_mode=pl.Buffered(1)),
            pl.BlockSpec((D_FF, D_MODEL), const, pipeline_mode=pl.Buffered(1)),
            pl.BlockSpec((1, D_MODEL), const),
        ],
        out_specs=pl.BlockSpec((FFN_TM, D_MODEL), lambda i: (i, 0)),
        out_shape=jax.ShapeDtypeStruct((n_tok, D_MODEL), F32),
        compiler_params=pltpu.CompilerParams(
            dimension_semantics=("arbitrary",), vmem_limit_bytes=VMEM_LIMIT),
        name="ffn",
    )(xs, mod_l, g.reshape(1, D_MODEL), w1, w2, final_g.reshape(1, D_MODEL))


def _inproj_kernel(xp_ref, x_ref, xn_ref, mod_ref, g_ref, w_ref, cw_ref, pw_ref, ps_ref, o_ref,
                   *, n_lat_tiles, lat_tiles_per_seq):
    i = pl.program_id(0)
    is_lat = i < n_lat_tiles
    pos = jnp.where(is_lat, i % lat_tiles_per_seq, 0)
    ntile = jnp.where(is_lat, lat_tiles_per_seq, 1)
    first = pos == 0
    last = pos == ntile - 1
    t0 = pos * TOK
    seq_len = ntile * TOK

    m = mod_ref[...]
    shift, scale = m[3:4], m[4:5]
    g = g_ref[...]
    ext = TOK + 2 * HALO
    xe = jnp.concatenate([xp_ref[...], x_ref[...], xn_ref[...]], axis=0)
    he = _rms_mod(xe, g, scale, shift).astype(BF16)

    u = jnp.dot(he, w_ref[:, :OFF_Q], preferred_element_type=F32)
    row = lax.broadcasted_iota(jnp.int32, (ext, 1), 0)
    keep_lo = jnp.where(first, HALO, 0)
    keep_hi = jnp.where(last, HALO + TOK, ext)
    u = jnp.where((row >= keep_lo) & (row < keep_hi), u, 0.0)

    def shifted(a, s):
        return pltpu.roll(a, s % ext, 0)

    def centre(a):
        return a[HALO:HALO + TOK]

    z = u[:, 2 * CONV_CH:3 * CONV_CH] * u[:, :CONV_CH]
    cw = cw_ref[...]
    conv = (centre(shifted(z, 1)) * cw[0:1] + centre(z) * cw[1:2] + centre(shifted(z, -1)) * cw[2:3])
    y_conv = centre(u[:, CONV_CH:2 * CONV_CH]) * conv

    v = u[:, 3 * CONV_CH:]
    a1 = v + shifted(v, 1)
    a2 = a1 + shifted(a1, 2)
    a3 = a2 + shifted(a2, 4)
    a4 = a3 + shifted(a3, 8)
    sums = (centre(a1), centre(shifted(a2, -1)), centre(shifted(a3, -3)), centre(shifted(a4, -7)))
    t = t0 + lax.broadcasted_iota(jnp.int32, (TOK, 1), 0)
    lane = lax.broadcasted_iota(jnp.int32, (TOK, POOL_CH), 1)
    mean = None
    for gi in reversed(range(len(POOL_WINDOWS))):
        w = POOL_WINDOWS[gi]
        left = w // 2
        right = w - 1 - left
        cnt = jnp.minimum(t + right + 1, seq_len) - jnp.maximum(t - left, 0)
        mg = sums[gi] / cnt.astype(F32)
        mean = mg if mean is None else jnp.where(lane < (gi + 1) * POOL_GROUP, mg, mean)
    dpool = (mean - centre(v)).astype(BF16)
    y_pool = jnp.dot(dpool, pw_ref[...], preferred_element_type=F32) * ps_ref[...]

    hc = he[HALO:HALO + TOK]
    qkv = jnp.dot(hc, w_ref[:, OFF_Q:], preferred_element_type=F32)
    o_ref[:, :CONV_CH] = y_conv.astype(BF16)
    o_ref[:, CONV_CH:CONV_CH + POOL_CH] = y_pool.astype(BF16)
    o_ref[:, 2 * CONV_CH:2 * CONV_CH + NA_CH] = (qkv[:, :NA_CH] * (NA_HEAD_DIM ** -0.5)).astype(BF16)
    o_ref[:, 2 * CONV_CH + NA_CH:] = qkv[:, NA_CH:].astype(BF16)


def _inproj(xs, mod_l, g, w_in, conv_w, pool_bd, pool_scale, *, n_lat_tiles, lat_tiles_per_seq,
            n_batch):
    nt = xs.shape[0]
    n_tiles = nt // TOK
    hb = TOK // HALO
    const = lambda i: (0, 0)
    kern = functools.partial(_inproj_kernel, n_lat_tiles=n_lat_tiles,
                             lat_tiles_per_seq=lat_tiles_per_seq)
    return pl.pallas_call(
        kern,
        grid=(n_tiles,),
        in_specs=[
            pl.BlockSpec((HALO, D_MODEL), lambda i: (jnp.maximum(i * hb - 1, 0), 0)),
            pl.BlockSpec((TOK, D_MODEL), lambda i: (i, 0)),
            pl.BlockSpec((HALO, D_MODEL), lambda i: (jnp.minimum((i + 1) * hb, nt // HALO - 1), 0)),
            pl.BlockSpec((None, N_MOD, D_MODEL),
                         lambda i: (jnp.minimum(i // lat_tiles_per_seq, n_batch), 0, 0)),
            pl.BlockSpec((1, D_MODEL), const),
            pl.BlockSpec((D_MODEL, D_IN), const, pipeline_mode=pl.Buffered(1)),
            pl.BlockSpec((3, CONV_CH), const),
            pl.BlockSpec((POOL_CH, POOL_CH), const),
            pl.BlockSpec((1, POOL_CH), const),
        ],
        out_specs=pl.BlockSpec((TOK, 2 * D_MODEL), lambda i: (i, 0)),
        out_shape=jax.ShapeDtypeStruct((nt, 2 * D_MODEL), BF16),
        compiler_params=pltpu.CompilerParams(
            dimension_semantics=("arbitrary",), vmem_limit_bytes=VMEM_LIMIT),
        name="inproj",
    )(xs, xs, xs, mod_l, g.reshape(1, D_MODEL), w_in, conv_w, pool_bd,
      pool_scale.reshape(1, POOL_CH))


def _softmax_pv(s, v_all):
    mx = jnp.max(s, axis=-1, keepdims=True)
    p = jnp.exp(s - mx)
    l = jnp.sum(p, axis=-1, keepdims=True)
    o = jnp.dot(p.astype(BF16), v_all, preferred_element_type=F32)
    return o / l


def _out_residual(x, ycp, o_heads, wo_ref, gate):
    y = jnp.concatenate([ycp] + [o.astype(BF16) for o in o_heads], axis=-1)
    return x + gate * jnp.dot(y, wo_ref[...], preferred_element_type=F32)


def _attn_kernel(x_ref, ycp_ref, q_ref, k0_ref, k1_ref, k2_ref, v0_ref, v1_ref, v2_ref,
                 kc_ref, vc_ref, bias_ref, mod_ref, wo_ref, o_ref):
    n_loc = NKB * TOK
    o_heads = []
    for h in range(NA_HEADS):
        hs = slice(h * NA_HEAD_DIM, (h + 1) * NA_HEAD_DIM)
        k_all = jnp.concatenate([k0_ref[:, hs], k1_ref[:, hs], k2_ref[:, hs], kc_ref[:, hs]], axis=0)
        v_all = jnp.concatenate([v0_ref[:, hs], v1_ref[:, hs], v2_ref[:, hs], vc_ref[:, hs]], axis=0)
        s = lax.dot_general(q_ref[:, hs], k_all, (((1,), (1,)), ((), ())),
                            preferred_element_type=F32)
        s = jnp.concatenate([s[:, :n_loc] + bias_ref[h], s[:, n_loc:]], axis=-1)
        o_heads.append(_softmax_pv(s, v_all))
    gate = mod_ref[5:6]
    o_ref[...] = _out_residual(x_ref[...], ycp_ref[...], o_heads, wo_ref, gate)


def _attention(xs, mix, bias, mod_l, w_out, *, n_batch, rows):
    nrb = rows // QROWS
    tiles_per_seq = nrb
    n_lat_tiles = n_batch * tiles_per_seq
    tok_blk = lambda rb, b: b * tiles_per_seq + rb
    ksb = lambda rb: jnp.clip(rb - 1, 0, nrb - NKB)
    geom = lambda rb: jnp.where(rb == 0, 0, jnp.where(rb == nrb - 1, 2, 1))

    def kv_spec(c, j):
        return pl.BlockSpec((TOK, NA_CH), lambda rb, b: (b * tiles_per_seq + ksb(rb) + j, c))

    in_specs = [
        pl.BlockSpec((TOK, D_MODEL), lambda rb, b: (tok_blk(rb, b), 0)),
        pl.BlockSpec((TOK, NA_CH), lambda rb, b: (tok_blk(rb, b), 0)),
        pl.BlockSpec((TOK, NA_CH), lambda rb, b: (tok_blk(rb, b), 1)),
        kv_spec(2, 0), kv_spec(2, 1), kv_spec(2, 2),
        kv_spec(3, 0), kv_spec(3, 1), kv_spec(3, 2),
        pl.BlockSpec((TOK, NA_CH), lambda rb, b: (n_lat_tiles + b, 2)),
        pl.BlockSpec((TOK, NA_CH), lambda rb, b: (n_lat_tiles + b, 3)),
        pl.BlockSpec((None, NA_HEADS, TOK, NKB * TOK), lambda rb, b: (geom(rb), 0, 0, 0)),
        pl.BlockSpec((None, N_MOD, D_MODEL), lambda rb, b: (b, 0, 0)),
        pl.BlockSpec((D_MODEL, D_MODEL), lambda rb, b: (0, 0)),
    ]
    return pl.pallas_call(
        _attn_kernel,
        grid=(nrb, n_batch),
        in_specs=in_specs,
        out_specs=pl.BlockSpec((TOK, D_MODEL), lambda rb, b: (tok_blk(rb, b), 0)),
        out_shape=jax.ShapeDtypeStruct(xs.shape, F32),
        input_output_aliases={0: 0},
        compiler_params=pltpu.CompilerParams(
            dimension_semantics=("arbitrary", "arbitrary"), vmem_limit_bytes=VMEM_LIMIT),
        name="attention",
    )(xs, mix, mix, mix, mix, mix, mix, mix, mix, mix, mix, bias, mod_l, w_out)


def _ctx_kernel(x_ref, ycp_ref, q_ref, k_ref, v_ref, mod_ref, wo_ref, o_ref):
    o_heads = []
    for h in range(NA_HEADS):
        hs = slice(h * NA_HEAD_DIM, (h + 1) * NA_HEAD_DIM)
        s = lax.dot_general(q_ref[:, hs], k_ref[:, hs], (((1,), (1,)), ((), ())),
                            preferred_element_type=F32)
        o_heads.append(_softmax_pv(s, v_ref[:, hs]))
    gate = mod_ref[5:6]
    o_ref[...] = _out_residual(x_ref[...], ycp_ref[...], o_heads, wo_ref, gate)


def _ctx_attention(xs, mix, mod_l, w_out, *, n_batch, n_lat_tiles):
    blk = lambda b: n_lat_tiles + b
    return pl.pallas_call(
        _ctx_kernel,
        grid=(n_batch,),
        in_specs=[
            pl.BlockSpec((TOK, D_MODEL), lambda b: (blk(b), 0)),
            pl.BlockSpec((TOK, NA_CH), lambda b: (blk(b), 0)),
            pl.BlockSpec((TOK, NA_CH), lambda b: (blk(b), 1)),
            pl.BlockSpec((TOK, NA_CH), lambda b: (blk(b), 2)),
            pl.BlockSpec((TOK, NA_CH), lambda b: (blk(b), 3)),
            pl.BlockSpec((None, N_MOD, D_MODEL), lambda b: (n_batch, 0, 0)),
            pl.BlockSpec((D_MODEL, D_MODEL), lambda b: (0, 0)),
        ],
        out_specs=pl.BlockSpec((TOK, D_MODEL), lambda b: (blk(b), 0)),
        out_shape=jax.ShapeDtypeStruct(xs.shape, F32),
        input_output_aliases={0: 0},
        compiler_params=pltpu.CompilerParams(
            dimension_semantics=("arbitrary",), vmem_limit_bytes=VMEM_LIMIT),
        name="ctx_attention",
    )(xs, mix, mix, mix, mix, mod_l, w_out)


def _bias_index_tables(rows):
    kh = min(NA_KH_MAX, rows)
    nrb = rows // QROWS
    qc = np.arange(GRID_W)
    kc = np.arange(GRID_W)
    cstart = np.clip(qc - NA_KW // 2, 0, GRID_W - NA_KW)
    col_valid = (kc[None, :] >= cstart[:, None]) & (kc[None, :] < cstart[:, None] + NA_KW)
    col_off = np.clip(kc[None, :] - qc[:, None] + NA_KW - 1, 0, 2 * NA_KW - 2)
    ro, co, ok = [], [], []
    for rb in (0, 1, nrb - 1):
        r = rb * QROWS + np.arange(QROWS)
        ks = QROWS * int(np.clip(rb - 1, 0, nrb - NKB))
        kr = ks + np.arange(KROWS)
        rs = np.clip(r - kh // 2, 0, rows - kh)
        row_valid = (kr[None, :] >= rs[:, None]) & (kr[None, :] < rs[:, None] + kh)
        row_off = np.clip(kr[None, :] - r[:, None] + NA_KH_MAX - 1, 0, 2 * NA_KH_MAX - 2)
        shape = (QROWS, GRID_W, KROWS, GRID_W)
        ro.append(np.broadcast_to(row_off[:, None, :, None], shape).reshape(TOK, KROWS * GRID_W))
        co.append(np.broadcast_to(col_off[None, :, None, :], shape).reshape(TOK, KROWS * GRID_W))
        ok.append(np.broadcast_to(row_valid[:, None, :, None] & col_valid[None, :, None, :],
                                  shape).reshape(TOK, KROWS * GRID_W))
    return np.stack(ro), np.stack(co), np.stack(ok)


def _expand_bias(rpb_l, tables):
    ro, co, ok = tables
    b = rpb_l[:, ro, co]
    return jnp.where(ok[None], b, NEG_INF).transpose(1, 0, 2, 3)


def _pool_block_diag(pool_w_l):
    n = len(POOL_WINDOWS)
    out = jnp.zeros((POOL_CH, POOL_CH), pool_w_l.dtype)
    for gi in range(n):
        sl = slice(gi * POOL_GROUP, (gi + 1) * POOL_GROUP)
        out = out.at[sl, sl].set(pool_w_l[gi])
    return out


def kernel(x, c, ctx, c_ctx, w_mod, b_mod, norm_g, ffn_w_in, ffn_w_out, w_in, conv_w, pool_w,
           pool_scale, rpb, w_out, final_g):
    n_batch, seq, d = x.shape
    ctx_len = ctx.shape[1]
    depth = w_mod.shape[0]
    rows = seq // GRID_W
    assert d == D_MODEL and ctx_len == TOK and seq % TOK == 0 and rows >= KROWS
    assert n_batch + 1 <= MOD_ROWS and w_in.shape[-1] == D_IN and ffn_w_in.shape[-1] == 2 * D_FF
    n_lat = n_batch * seq
    n_lat_tiles = n_lat // TOK
    tiles_per_seq = seq // TOK

    xs = jnp.concatenate([x.reshape(n_lat, d), ctx.reshape(n_batch * ctx_len, d)], axis=0)
    nt = xs.shape[0]

    cc = jnp.zeros((MOD_ROWS, d), F32).at[:n_batch].set(c).at[n_batch].set(c_ctx)
    mod = _modulation(cc, w_mod, b_mod).reshape(depth, MOD_ROWS, N_MOD, d)

    tables = _bias_index_tables(rows)
    ffn_kw = dict(lat_tiles_per_batch=seq // FFN_TM, n_batch=n_batch)
    for l in range(depth):
        last = l == depth - 1
        mod_l = mod[l]
        w1 = ffn_w_in[l].astype(BF16)
        w2 = ffn_w_out[l].astype(BF16)
        w_in_l = w_in[l].astype(BF16)
        w_out_l = w_out[l].astype(BF16)
        bias = _expand_bias(rpb[l], tables)
        pool_bd = _pool_block_diag(pool_w[l]).astype(BF16)

        xs = _ffn(xs, nt, mod_l, norm_g[l, 0], w1[0], w2[0], final_g, sub=0, final=False, **ffn_kw)
        mix = _inproj(xs, mod_l, norm_g[l, 1], w_in_l, conv_w[l], pool_bd, pool_scale[l],
                      n_lat_tiles=n_lat_tiles, lat_tiles_per_seq=tiles_per_seq, n_batch=n_batch)
        xs = _attention(xs, mix, bias, mod_l, w_out_l, n_batch=n_batch, rows=rows)
        if not last:
            xs = _ctx_attention(xs, mix, mod_l, w_out_l, n_batch=n_batch, n_lat_tiles=n_lat_tiles)
        n_tok = n_lat if last else nt
        xs = _ffn(xs, n_tok, mod_l, norm_g[l, 2], w1[1], w2[1], final_g, sub=2, final=last,
                  **ffn_kw)
    return xs.reshape(n_batch, seq, d)
```

```python
import functools

import numpy as np
import jax
import jax.numpy as jnp
from jax import lax
from jax.experimental import pallas as pl
from jax.experimental.pallas import tpu as pltpu

F32 = jnp.float32
BF16 = jnp.bfloat16

D_MODEL = 1024
GRID_W = 64
D_FF = 2816
N_MOD = 9
RMS_EPS = 1e-6
NEG_INF = -1e30
CONV_CH = D_MODEL // 4
POOL_CH = D_MODEL // 4
POOL_WINDOWS = (2, 4, 8, 16)
POOL_GROUP = POOL_CH // 4
NA_HEAD_DIM = 64
NA_CH = D_MODEL // 2
NA_HEADS = NA_CH // NA_HEAD_DIM
NA_KH_MAX = 8
NA_KW = 16
OFF_Q = 3 * CONV_CH + POOL_CH
D_IN = OFF_Q + 3 * NA_CH

FFN_TM = 512
FFN_CHUNKS = ((0, 1536), (1536, 2816))
TOK = 256
HALO = 8
QROWS = TOK // GRID_W
KROWS = QROWS + NA_KH_MAX
NKB = KROWS * GRID_W // TOK
MOD_ROWS = 16
MOD_TN = 2304
VMEM_LIMIT = 56 * 1024 * 1024


def _rms_mod(x, g, scale, shift):
    ms = jnp.mean(x * x, axis=-1, keepdims=True)
    return (x * lax.rsqrt(ms + RMS_EPS) * g) * (1.0 + scale) + shift


def _mod_kernel(c_ref, w_ref, b_ref, o_ref):
    c = c_ref[...]
    s = (c * jax.nn.sigmoid(c)).astype(BF16)
    o_ref[...] = jnp.dot(s, w_ref[...].astype(BF16), preferred_element_type=F32) + b_ref[...]


def _modulation(cc, w_mod, b_mod):
    depth = w_mod.shape[0]
    n = N_MOD * D_MODEL
    return pl.pallas_call(
        _mod_kernel,
        grid=(depth, n // MOD_TN),
        in_specs=[
            pl.BlockSpec((MOD_ROWS, D_MODEL), lambda l, j: (0, 0)),
            pl.BlockSpec((None, D_MODEL, MOD_TN), lambda l, j: (l, 0, j)),
            pl.BlockSpec((None, 1, MOD_TN), lambda l, j: (l, 0, j)),
        ],
        out_specs=pl.BlockSpec((None, MOD_ROWS, MOD_TN), lambda l, j: (l, 0, j)),
        out_shape=jax.ShapeDtypeStruct((depth, MOD_ROWS, n), F32),
        compiler_params=pltpu.CompilerParams(
            dimension_semantics=("arbitrary", "arbitrary"), vmem_limit_bytes=VMEM_LIMIT),
        name="modulation",
    )(cc, w_mod, b_mod.reshape(depth, 1, n))


def _ffn_kernel(x_ref, mod_ref, g_ref, w1_ref, w2_ref, fg_ref, o_ref, *, sub, final):
    x = x_ref[...]
    m = mod_ref[...]
    shift = m[3 * sub:3 * sub + 1]
    scale = m[3 * sub + 1:3 * sub + 2]
    gate = m[3 * sub + 2:3 * sub + 3]
    hb = _rms_mod(x, g_ref[...], scale, shift).astype(BF16)
    acc = None
    for lo, hi in FFN_CHUNKS:
        a = jnp.dot(hb, w1_ref[:, lo:hi], preferred_element_type=F32)
        b = jnp.dot(hb, w1_ref[:, D_FF + lo:D_FF + hi], preferred_element_type=F32)
        gg = (a * jax.nn.sigmoid(a) * b).astype(BF16)
        part = jnp.dot(gg, w2_ref[lo:hi, :], preferred_element_type=F32)
        acc = part if acc is None else acc + part
    y = x + (0.5 * gate) * acc
    if final:
        ms = jnp.mean(y * y, axis=-1, keepdims=True)
        y = y * lax.rsqrt(ms + RMS_EPS) * fg_ref[...]
    o_ref[...] = y


def _ffn(xs, n_tok, mod_l, g, w1, w2, final_g, *, sub, final, lat_tiles_per_batch, n_batch):
    const = lambda i: (0, 0)
    return pl.pallas_call(
        functools.partial(_ffn_kernel, sub=sub, final=final),
        grid=(n_tok // FFN_TM,),
        in_specs=[
            pl.BlockSpec((FFN_TM, D_MODEL), lambda i: (i, 0)),
            pl.BlockSpec((None, N_MOD, D_MODEL),
                         lambda i: (jnp.minimum(i // lat_tiles_per_batch, n_batch), 0, 0)),
            pl.BlockSpec((1, D_MODEL), const),
            pl.BlockSpec((D_MODEL, 2 * D_FF), const, pipeline_mode=pl.Buffered(1)),
            pl.BlockSpec((D_FF, D_MODEL), const, pipeline_mode=pl.Buffered(1)),
            pl.BlockSpec((1, D_MODEL), const),
        ],
        out_specs=pl.BlockSpec((FFN_TM, D_MODEL), lambda i: (i, 0)),
        out_shape=jax.ShapeDtypeStruct((n_tok, D_MODEL), F32),
        compiler_params=pltpu.CompilerParams(
            dimension_semantics=("arbitrary",), vmem_limit_bytes=VMEM_LIMIT),
        name="ffn",
    )(xs, mod_l, g.reshape(1, D_MODEL), w1, w2, final_g.reshape(1, D_MODEL))


def _inproj_kernel(xp_ref, x_ref, xn_ref, mod_ref, g_ref, w_ref, cw_ref, pw_ref, ps_ref, o_ref,
                   *, n_lat_tiles, lat_tiles_per_seq):
    i = pl.program_id(0)
    is_lat = i < n_lat_tiles
    pos = jnp.where(is_lat, i % lat_tiles_per_seq, 0)
    ntile = jnp.where(is_lat, lat_tiles_per_seq, 1)
    first = pos == 0
    last = pos == ntile - 1
    t0 = pos * TOK
    seq_len = ntile * TOK

    m = mod_ref[...]
    shift, scale = m[3:4], m[4:5]
    g = g_ref[...]
    ext = TOK + 2 * HALO
    xe = jnp.concatenate([xp_ref[...], x_ref[...], xn_ref[...]], axis=0)
    he = _rms_mod(xe, g, scale, shift).astype(BF16)

    u = jnp.dot(he, w_ref[:, :OFF_Q], preferred_element_type=F32)
    row = lax.broadcasted_iota(jnp.int32, (ext, 1), 0)
    keep_lo = jnp.where(first, HALO, 0)
    keep_hi = jnp.where(last, HALO + TOK, ext)
    u = jnp.where((row >= keep_lo) & (row < keep_hi), u, 0.0)

    def shifted(a, s):
        return pltpu.roll(a, s % ext, 0)

    def centre(a):
        return a[HALO:HALO + TOK]

    z = u[:, 2 * CONV_CH:3 * CONV_CH] * u[:, :CONV_CH]
    cw = cw_ref[...]
    conv = (centre(shifted(z, 1)) * cw[0:1] + centre(z) * cw[1:2] + centre(shifted(z, -1)) * cw[2:3])
    y_conv = centre(u[:, CONV_CH:2 * CONV_CH]) * conv

    v = u[:, 3 * CONV_CH:]
    a1 = v + shifted(v, 1)
    a2 = a1 + shifted(a1, 2)
    a3 = a2 + shifted(a2, 4)
    a4 = a3 + shifted(a3, 8)
    sums = (centre(a1), centre(shifted(a2, -1)), centre(shifted(a3, -3)), centre(shifted(a4, -7)))
    t = t0 + lax.broadcasted_iota(jnp.int32, (TOK, 1), 0)
    lane = lax.broadcasted_iota(jnp.int32, (TOK, POOL_CH), 1)
    mean = None
    for gi in reversed(range(len(POOL_WINDOWS))):
        w = POOL_WINDOWS[gi]
        left = w // 2
        right = w - 1 - left
        cnt = jnp.minimum(t + right + 1, seq_len) - jnp.maximum(t - left, 0)
        mg = sums[gi] / cnt.astype(F32)
        mean = mg if mean is None else jnp.where(lane < (gi + 1) * POOL_GROUP, mg, mean)
    dpool = (mean - centre(v)).astype(BF16)
    y_pool = jnp.dot(dpool, pw_ref[...], preferred_element_type=F32) * ps_ref[...]

    hc = he[HALO:HALO + TOK]
    qkv = jnp.dot(hc, w_ref[:, OFF_Q:], preferred_element_type=F32)
    o_ref[:, :CONV_CH] = y_conv.astype(BF16)
    o_ref[:, CONV_CH:CONV_CH + POOL_CH] = y_pool.astype(BF16)
    o_ref[:, 2 * CONV_CH:2 * CONV_CH + NA_CH] = (qkv[:, :NA_CH] * (NA_HEAD_DIM ** -0.5)).astype(BF16)
    o_ref[:, 2 * CONV_CH + NA_CH:] = qkv[:, NA_CH:].astype(BF16)


def _inproj(xs, mod_l, g, w_in, conv_w, pool_bd, pool_scale, *, n_lat_tiles, lat_tiles_per_seq,
            n_batch):
    nt = xs.shape[0]
    n_tiles = nt // TOK
    hb = TOK // HALO
    const = lambda i: (0, 0)
    kern = functools.partial(_inproj_kernel, n_lat_tiles=n_lat_tiles,
                             lat_tiles_per_seq=lat_tiles_per_seq)
    return pl.pallas_call(
        kern,
        grid=(n_tiles,),
        in_specs=[
            pl.BlockSpec((HALO, D_MODEL), lambda i: (jnp.maximum(i * hb - 1, 0), 0)),
            pl.BlockSpec((TOK, D_MODEL), lambda i: (i, 0)),
            pl.BlockSpec((HALO, D_MODEL), lambda i: (jnp.minimum((i + 1) * hb, nt // HALO - 1), 0)),
            pl.BlockSpec((None, N_MOD, D_MODEL),
                         lambda i: (jnp.minimum(i // lat_tiles_per_seq, n_batch), 0, 0)),
            pl.BlockSpec((1, D_MODEL), const),
            pl.BlockSpec((D_MODEL, D_IN), const, pipeline_mode=pl.Buffered(1)),
            pl.BlockSpec((3, CONV_CH), const),
            pl.BlockSpec((POOL_CH, POOL_CH), const),
            pl.BlockSpec((1, POOL_CH), const),
        ],
        out_specs=pl.BlockSpec((TOK, 2 * D_MODEL), lambda i: (i, 0)),
        out_shape=jax.ShapeDtypeStruct((nt, 2 * D_MODEL), BF16),
        compiler_params=pltpu.CompilerParams(
            dimension_semantics=("arbitrary",), vmem_limit_bytes=VMEM_LIMIT),
        name="inproj",
    )(xs, xs, xs, mod_l, g.reshape(1, D_MODEL), w_in, conv_w, pool_bd,
      pool_scale.reshape(1, POOL_CH))


def _softmax_pv(s, v_all):
    mx = jnp.max(s, axis=-1, keepdims=True)
    p = jnp.exp(s - mx)
    l = jnp.sum(p, axis=-1, keepdims=True)
    o = jnp.dot(p.astype(BF16), v_all, preferred_element_type=F32)
    return o / l


def _out_residual(x, ycp, o_heads, wo_ref, gate):
    y = jnp.concatenate([ycp] + [o.astype(BF16) for o in o_heads], axis=-1)
    return x + gate * jnp.dot(y, wo_ref[...], preferred_element_type=F32)


def _attn_kernel(x_ref, ycp_ref, q_ref, k0_ref, k1_ref, k2_ref, v0_ref, v1_ref, v2_ref,
                 kc_ref, vc_ref, bias_ref, mod_ref, wo_ref, o_ref):
    n_loc = NKB * TOK
    o_heads = []
    for h in range(NA_HEADS):
        hs = slice(h * NA_HEAD_DIM, (h + 1) * NA_HEAD_DIM)
        k_all = jnp.concatenate([k0_ref[:, hs], k1_ref[:, hs], k2_ref[:, hs], kc_ref[:, hs]], axis=0)
        v_all = jnp.concatenate([v0_ref[:, hs], v1_ref[:, hs], v2_ref[:, hs], vc_ref[:, hs]], axis=0)
        s = lax.dot_general(q_ref[:, hs], k_all, (((1,), (1,)), ((), ())),
                            preferred_element_type=F32)
        s = jnp.concatenate([s[:, :n_loc] + bias_ref[h], s[:, n_loc:]], axis=-1)
        o_heads.append(_softmax_pv(s, v_all))
    gate = mod_ref[5:6]
    o_ref[...] = _out_residual(x_ref[...], ycp_ref[...], o_heads, wo_ref, gate)


def _attention(xs, mix, bias, mod_l, w_out, *, n_batch, rows):
    nrb = rows // QROWS
    tiles_per_seq = nrb
    n_lat_tiles = n_batch * tiles_per_seq
    tok_blk = lambda rb, b: b * tiles_per_seq + rb
    ksb = lambda rb: jnp.clip(rb - 1, 0, nrb - NKB)
    geom = lambda rb: jnp.where(rb == 0, 0, jnp.where(rb == nrb - 1, 2, 1))

    def kv_spec(c, j):
        return pl.BlockSpec((TOK, NA_CH), lambda rb, b: (b * tiles_per_seq + ksb(rb) + j, c))

    in_specs = [
        pl.BlockSpec((TOK, D_MODEL), lambda rb, b: (tok_blk(rb, b), 0)),
        pl.BlockSpec((TOK, NA_CH), lambda rb, b: (tok_blk(rb, b), 0)),
        pl.BlockSpec((TOK, NA_CH), lambda rb, b: (tok_blk(rb, b), 1)),
        kv_spec(2, 0), kv_spec(2, 1), kv_spec(2, 2),
        kv_spec(3, 0), kv_spec(3, 1), kv_spec(3, 2),
        pl.BlockSpec((TOK, NA_CH), lambda rb, b: (n_lat_tiles + b, 2)),
        pl.BlockSpec((TOK, NA_CH), lambda rb, b: (n_lat_tiles + b, 3)),
        pl.BlockSpec((None, NA_HEADS, TOK, NKB * TOK), lambda rb, b: (geom(rb), 0, 0, 0)),
        pl.BlockSpec((None, N_MOD, D_MODEL), lambda rb, b: (b, 0, 0)),
        pl.BlockSpec((D_MODEL, D_MODEL), lambda rb, b: (0, 0)),
    ]
    return pl.pallas_call(
        _attn_kernel,
        grid=(nrb, n_batch),
        in_specs=in_specs,
        out_specs=pl.BlockSpec((TOK, D_MODEL), lambda rb, b: (tok_blk(rb, b), 0)),
        out_shape=jax.ShapeDtypeStruct(xs.shape, F32),
        input_output_aliases={0: 0},
        compiler_params=pltpu.CompilerParams(
            dimension_semantics=("arbitrary", "arbitrary"), vmem_limit_bytes=VMEM_LIMIT),
        name="attention",
    )(xs, mix, mix, mix, mix, mix, mix, mix, mix, mix, mix, bias, mod_l, w_out)


def _ctx_kernel(x_ref, ycp_ref, q_ref, k_ref, v_ref, mod_ref, wo_ref, o_ref):
    o_heads = []
    for h in range(NA_HEADS):
        hs = slice(h * NA_HEAD_DIM, (h + 1) * NA_HEAD_DIM)
        s = lax.dot_general(q_ref[:, hs], k_ref[:, hs], (((1,), (1,)), ((), ())),
                            preferred_element_type=F32)
        o_heads.append(_softmax_pv(s, v_ref[:, hs]))
    gate = mod_ref[5:6]
    o_ref[...] = _out_residual(x_ref[...], ycp_ref[...], o_heads, wo_ref, gate)


def _ctx_attention(xs, mix, mod_l, w_out, *, n_batch, n_lat_tiles):
    blk = lambda b: n_lat_tiles + b
    return pl.pallas_call(
        _ctx_kernel,
        grid=(n_batch,),
        in_specs=[
            pl.BlockSpec((TOK, D_MODEL), lambda b: (blk(b), 0)),
            pl.BlockSpec((TOK, NA_CH), lambda b: (blk(b), 0)),
            pl.BlockSpec((TOK, NA_CH), lambda b: (blk(b), 1)),
            pl.BlockSpec((TOK, NA_CH), lambda b: (blk(b), 2)),
            pl.BlockSpec((TOK, NA_CH), lambda b: (blk(b), 3)),
            pl.BlockSpec((None, N_MOD, D_MODEL), lambda b: (n_batch, 0, 0)),
            pl.BlockSpec((D_MODEL, D_MODEL), lambda b: (0, 0)),
        ],
        out_specs=pl.BlockSpec((TOK, D_MODEL), lambda b: (blk(b), 0)),
        out_shape=jax.ShapeDtypeStruct(xs.shape, F32),
        input_output_aliases={0: 0},
        compiler_params=pltpu.CompilerParams(
            dimension_semantics=("arbitrary",), vmem_limit_bytes=VMEM_LIMIT),
        name="ctx_attention",
    )(xs, mix, mix, mix, mix, mod_l, w_out)


def _bias_row_tables(rows):
    kh = min(NA_KH_MAX, rows)
    nrb = rows // QROWS
    n_off = 2 * NA_KH_MAX - 1
    out = []
    for rb in (0, 1, nrb - 1):
        r = rb * QROWS + np.arange(QROWS)
        ks = QROWS * int(np.clip(rb - 1, 0, nrb - NKB))
        kr = ks + np.arange(KROWS)
        rs = np.clip(r - kh // 2, 0, rows - kh)
        row_valid = (kr[None, :] >= rs[:, None]) & (kr[None, :] < rs[:, None] + kh)
        row_off = kr[None, :] - r[:, None] + NA_KH_MAX - 1
        out.append(np.where(row_valid, row_off, n_off))
    return np.stack(out)


def _expand_bias(rpb_l, row_tables):
    n_off = 2 * NA_KH_MAX - 1
    n_col = 2 * NA_KW - 1
    width = 2 * GRID_W
    start = GRID_W - NA_KW
    w = jnp.full((NA_HEADS, n_off, width), NEG_INF, F32)
    w = w.at[:, :, start:start + n_col].set(rpb_l)
    flat = jnp.tile(w, (1, 1, GRID_W))[..., :GRID_W * (width - 1)]
    t = flat.reshape(NA_HEADS, n_off, GRID_W, width - 1)[..., GRID_W - 1:]
    qc = np.arange(GRID_W)
    cstart = np.clip(qc - NA_KW // 2, 0, GRID_W - NA_KW)
    col_valid = (qc[None, :] >= cstart[:, None]) & (qc[None, :] < cstart[:, None] + NA_KW)
    t = jnp.where(col_valid, t, NEG_INF)
    t = jnp.concatenate([t, jnp.full((NA_HEADS, 1, GRID_W, GRID_W), NEG_INF, F32)], axis=1)
    geoms = []
    for tab in row_tables:
        qrows = [jnp.concatenate([t[:, int(tab[i, j])] for j in range(KROWS)], axis=-1)
                 for i in range(QROWS)]
        geoms.append(jnp.concatenate(qrows, axis=-2))
    return jnp.stack(geoms)


def _pool_block_diag(pool_w_l):
    n = len(POOL_WINDOWS)
    out = jnp.zeros((POOL_CH, POOL_CH), pool_w_l.dtype)
    for gi in range(n):
        sl = slice(gi * POOL_GROUP, (gi + 1) * POOL_GROUP)
        out = out.at[sl, sl].set(pool_w_l[gi])
    return out


def kernel(x, c, ctx, c_ctx, w_mod, b_mod, norm_g, ffn_w_in, ffn_w_out, w_in, conv_w, pool_w,
           pool_scale, rpb, w_out, final_g):
    n_batch, seq, d = x.shape
    ctx_len = ctx.shape[1]
    depth = w_mod.shape[0]
    rows = seq // GRID_W
    assert d == D_MODEL and ctx_len == TOK and seq % TOK == 0 and rows >= KROWS
    assert n_batch + 1 <= MOD_ROWS and w_in.shape[-1] == D_IN and ffn_w_in.shape[-1] == 2 * D_FF
    n_lat = n_batch * seq
    n_lat_tiles = n_lat // TOK
    tiles_per_seq = seq // TOK

    xs = jnp.concatenate([x.reshape(n_lat, d), ctx.reshape(n_batch * ctx_len, d)], axis=0)
    nt = xs.shape[0]

    cc = jnp.zeros((MOD_ROWS, d), F32).at[:n_batch].set(c).at[n_batch].set(c_ctx)
    mod = _modulation(cc, w_mod, b_mod).reshape(depth, MOD_ROWS, N_MOD, d)

    tables = _bias_row_tables(rows)
    ffn_kw = dict(lat_tiles_per_batch=seq // FFN_TM, n_batch=n_batch)
    for l in range(depth):
        last = l == depth - 1
        mod_l = mod[l]
        w1 = ffn_w_in[l].astype(BF16)
        w2 = ffn_w_out[l].astype(BF16)
        w_in_l = w_in[l].astype(BF16)
        w_out_l = w_out[l].astype(BF16)
        bias = _expand_bias(rpb[l], tables)
        pool_bd = _pool_block_diag(pool_w[l]).astype(BF16)

        xs = _ffn(xs, nt, mod_l, norm_g[l, 0], w1[0], w2[0], final_g, sub=0, final=False, **ffn_kw)
        mix = _inproj(xs, mod_l, norm_g[l, 1], w_in_l, conv_w[l], pool_bd, pool_scale[l],
                      n_lat_tiles=n_lat_tiles, lat_tiles_per_seq=tiles_per_seq, n_batch=n_batch)
        xs = _attention(xs, mix, bias, mod_l, w_out_l, n_batch=n_batch, rows=rows)
        if not last:
            xs = _ctx_attention(xs, mix, mod_l, w_out_l, n_batch=n_batch, n_lat_tiles=n_lat_tiles)
        n_tok = n_lat if last else nt
        xs = _ffn(xs, n_tok, mod_l, norm_g[l, 2], w1[1], w2[1], final_g, sub=2, final=last,
                  **ffn_kw)
    return xs.reshape(n_batch, seq, d)
```

```python
import functools

import numpy as np
import jax
import jax.numpy as jnp
from jax import lax
from jax.experimental import pallas as pl
from jax.experimental.pallas import tpu as pltpu

F32 = jnp.float32
BF16 = jnp.bfloat16

D_MODEL = 1024
GRID_W = 64
D_FF = 2816
N_MOD = 9
RMS_EPS = 1e-6
NEG_INF = -1e30
CONV_CH = D_MODEL // 4
POOL_CH = D_MODEL // 4
POOL_WINDOWS = (2, 4, 8, 16)
POOL_GROUP = POOL_CH // 4
NA_HEAD_DIM = 64
NA_CH = D_MODEL // 2
NA_HEADS = NA_CH // NA_HEAD_DIM
NA_KH_MAX = 8
NA_KW = 16
OFF_Q = 3 * CONV_CH + POOL_CH
D_IN = OFF_Q + 3 * NA_CH

FFN_TM = 512
FFN_CHUNKS = ((0, 1536), (1536, 2816))
TOK = 256
HALO = 8
QROWS = TOK // GRID_W
KROWS = QROWS + NA_KH_MAX
NKB = KROWS * GRID_W // TOK
MOD_ROWS = 16
MOD_TN = 2304
VMEM_LIMIT = 56 * 1024 * 1024


def _rms_mod(x, g, scale, shift):
    ms = jnp.mean(x * x, axis=-1, keepdims=True)
    return (x * lax.rsqrt(ms + RMS_EPS) * g) * (1.0 + scale) + shift


def _mod_kernel(c_ref, w_ref, b_ref, o_ref):
    c = c_ref[...]
    s = (c * jax.nn.sigmoid(c)).astype(BF16)
    o_ref[...] = jnp.dot(s, w_ref[...].astype(BF16), preferred_element_type=F32) + b_ref[...]


def _modulation(cc, w_mod, b_mod):
    depth = w_mod.shape[0]
    n = N_MOD * D_MODEL
    return pl.pallas_call(
        _mod_kernel,
        grid=(depth, n // MOD_TN),
        in_specs=[
            pl.BlockSpec((MOD_ROWS, D_MODEL), lambda l, j: (0, 0)),
            pl.BlockSpec((None, D_MODEL, MOD_TN), lambda l, j: (l, 0, j)),
            pl.BlockSpec((None, 1, MOD_TN), lambda l, j: (l, 0, j)),
        ],
        out_specs=pl.BlockSpec((None, MOD_ROWS, MOD_TN), lambda l, j: (l, 0, j)),
        out_shape=jax.ShapeDtypeStruct((depth, MOD_ROWS, n), F32),
        compiler_params=pltpu.CompilerParams(
            dimension_semantics=("arbitrary", "arbitrary"), vmem_limit_bytes=VMEM_LIMIT),
        name="modulation",
    )(cc, w_mod, b_mod.reshape(depth, 1, n))


def _ffn_kernel(xa_ref, xb_ref, mod_ref, g_ref, w1_ref, w2_ref, fg_ref, o_ref, *, sub, final,
                n_a_tiles, n_b_tiles):
    if n_b_tiles:
        x = jnp.where(pl.program_id(0) < n_a_tiles, xa_ref[...], xb_ref[...])
    else:
        x = xa_ref[...]
    m = mod_ref[...]
    shift = m[3 * sub:3 * sub + 1]
    scale = m[3 * sub + 1:3 * sub + 2]
    gate = m[3 * sub + 2:3 * sub + 3]
    hb = _rms_mod(x, g_ref[...], scale, shift).astype(BF16)
    acc = None
    for lo, hi in FFN_CHUNKS:
        a = jnp.dot(hb, w1_ref[:, lo:hi], preferred_element_type=F32)
        b = jnp.dot(hb, w1_ref[:, D_FF + lo:D_FF + hi], preferred_element_type=F32)
        gg = (a * jax.nn.sigmoid(a) * b).astype(BF16)
        part = jnp.dot(gg, w2_ref[lo:hi, :], preferred_element_type=F32)
        acc = part if acc is None else acc + part
    y = x + (0.5 * gate) * acc
    if final:
        ms = jnp.mean(y * y, axis=-1, keepdims=True)
        y = y * lax.rsqrt(ms + RMS_EPS) * fg_ref[...]
    o_ref[...] = y


def _ffn(xa, xb, xb_off, n_a_tiles, n_b_tiles, mod, norm_g4, w1, w2, final_g, *, layer, which, sub,
         final, lat_tiles_per_batch, n_batch):
    n_tiles = n_a_tiles + n_b_tiles
    const = lambda i: (0, 0)
    kern = functools.partial(_ffn_kernel, sub=sub, final=final, n_a_tiles=n_a_tiles,
                             n_b_tiles=n_b_tiles)
    return pl.pallas_call(
        kern,
        grid=(n_tiles,),
        in_specs=[
            pl.BlockSpec((FFN_TM, D_MODEL), lambda i: (jnp.minimum(i, n_a_tiles - 1), 0)),
            pl.BlockSpec((FFN_TM, D_MODEL), lambda i: (jnp.maximum(i - n_a_tiles, 0) + xb_off, 0)),
            pl.BlockSpec((None, None, N_MOD, D_MODEL),
                         lambda i: (layer, jnp.minimum(i // lat_tiles_per_batch, n_batch), 0, 0)),
            pl.BlockSpec((None, None, 1, D_MODEL), lambda i: (layer, sub, 0, 0)),
            pl.BlockSpec((None, None, D_MODEL, 2 * D_FF), lambda i: (layer, which, 0, 0),
                         pipeline_mode=pl.Buffered(1)),
            pl.BlockSpec((None, None, D_FF, D_MODEL), lambda i: (layer, which, 0, 0),
                         pipeline_mode=pl.Buffered(1)),
            pl.BlockSpec((1, D_MODEL), const),
        ],
        out_specs=pl.BlockSpec((FFN_TM, D_MODEL), lambda i: (i, 0)),
        out_shape=jax.ShapeDtypeStruct((n_tiles * FFN_TM, D_MODEL), F32),
        compiler_params=pltpu.CompilerParams(
            dimension_semantics=("arbitrary",), vmem_limit_bytes=VMEM_LIMIT),
        name="ffn",
    )(xa, xb, mod, norm_g4, w1, w2, final_g.reshape(1, D_MODEL))


def _inproj_kernel(xp_ref, x_ref, xn_ref, mod_ref, g_ref, w_ref, cw_ref, pw_ref, ps_ref, o_ref,
                   *, n_lat_tiles, lat_tiles_per_seq):
    i = pl.program_id(0)
    is_lat = i < n_lat_tiles
    pos = jnp.where(is_lat, i % lat_tiles_per_seq, 0)
    ntile = jnp.where(is_lat, lat_tiles_per_seq, 1)
    first = pos == 0
    last = pos == ntile - 1
    t0 = pos * TOK
    seq_len = ntile * TOK

    m = mod_ref[...]
    shift, scale = m[3:4], m[4:5]
    g = g_ref[...]
    ext = TOK + 2 * HALO
    xe = jnp.concatenate([xp_ref[...], x_ref[...], xn_ref[...]], axis=0)
    he = _rms_mod(xe, g, scale, shift).astype(BF16)

    u = jnp.dot(he, w_ref[:, :OFF_Q], preferred_element_type=F32)
    row = lax.broadcasted_iota(jnp.int32, (ext, 1), 0)
    keep_lo = jnp.where(first, HALO, 0)
    keep_hi = jnp.where(last, HALO + TOK, ext)
    u = jnp.where((row >= keep_lo) & (row < keep_hi), u, 0.0)

    def shifted(a, s):
        return pltpu.roll(a, s % ext, 0)

    def centre(a):
        return a[HALO:HALO + TOK]

    z = u[:, 2 * CONV_CH:3 * CONV_CH] * u[:, :CONV_CH]
    cw = cw_ref[...]
    conv = (centre(shifted(z, 1)) * cw[0:1] + centre(z) * cw[1:2] + centre(shifted(z, -1)) * cw[2:3])
    y_conv = centre(u[:, CONV_CH:2 * CONV_CH]) * conv

    v = u[:, 3 * CONV_CH:]
    a1 = v + shifted(v, 1)
    a2 = a1 + shifted(a1, 2)
    a3 = a2 + shifted(a2, 4)
    a4 = a3 + shifted(a3, 8)
    sums = (centre(a1), centre(shifted(a2, -1)), centre(shifted(a3, -3)), centre(shifted(a4, -7)))
    t = t0 + lax.broadcasted_iota(jnp.int32, (TOK, 1), 0)
    lane = lax.broadcasted_iota(jnp.int32, (TOK, POOL_CH), 1)
    mean = None
    for gi in reversed(range(len(POOL_WINDOWS))):
        w = POOL_WINDOWS[gi]
        left = w // 2
        right = w - 1 - left
        cnt = jnp.minimum(t + right + 1, seq_len) - jnp.maximum(t - left, 0)
        mg = sums[gi] / cnt.astype(F32)
        mean = mg if mean is None else jnp.where(lane < (gi + 1) * POOL_GROUP, mg, mean)
    dpool = (mean - centre(v)).astype(BF16)
    y_pool = jnp.dot(dpool, pw_ref[...], preferred_element_type=F32) * ps_ref[...]

    hc = he[HALO:HALO + TOK]
    qkv = jnp.dot(hc, w_ref[:, OFF_Q:], preferred_element_type=F32)
    o_ref[:, :CONV_CH] = y_conv.astype(BF16)
    o_ref[:, CONV_CH:CONV_CH + POOL_CH] = y_pool.astype(BF16)
    o_ref[:, 2 * CONV_CH:2 * CONV_CH + NA_CH] = (qkv[:, :NA_CH] * (NA_HEAD_DIM ** -0.5)).astype(BF16)
    o_ref[:, 2 * CONV_CH + NA_CH:] = qkv[:, NA_CH:].astype(BF16)


def _inproj(xs, mod, norm_g4, w_in, conv_w, pool_bd, pool_scale, *, layer, n_lat_tiles,
            lat_tiles_per_seq, n_batch):
    nt = xs.shape[0]
    n_tiles = nt // TOK
    hb = TOK // HALO
    per_layer = lambda i: (layer, 0, 0)
    kern = functools.partial(_inproj_kernel, n_lat_tiles=n_lat_tiles,
                             lat_tiles_per_seq=lat_tiles_per_seq)
    return pl.pallas_call(
        kern,
        grid=(n_tiles,),
        in_specs=[
            pl.BlockSpec((HALO, D_MODEL), lambda i: (jnp.maximum(i * hb - 1, 0), 0)),
            pl.BlockSpec((TOK, D_MODEL), lambda i: (i, 0)),
            pl.BlockSpec((HALO, D_MODEL), lambda i: (jnp.minimum((i + 1) * hb, nt // HALO - 1), 0)),
            pl.BlockSpec((None, None, N_MOD, D_MODEL),
                         lambda i: (layer, jnp.minimum(i // lat_tiles_per_seq, n_batch), 0, 0)),
            pl.BlockSpec((None, None, 1, D_MODEL), lambda i: (layer, 1, 0, 0)),
            pl.BlockSpec((None, D_MODEL, D_IN), per_layer, pipeline_mode=pl.Buffered(1)),
            pl.BlockSpec((None, 3, CONV_CH), per_layer),
            pl.BlockSpec((None, POOL_CH, POOL_CH), per_layer),
            pl.BlockSpec((None, 1, POOL_CH), per_layer),
        ],
        out_specs=pl.BlockSpec((TOK, 2 * D_MODEL), lambda i: (i, 0)),
        out_shape=jax.ShapeDtypeStruct((nt, 2 * D_MODEL), BF16),
        compiler_params=pltpu.CompilerParams(
            dimension_semantics=("arbitrary",), vmem_limit_bytes=VMEM_LIMIT),
        name="inproj",
    )(xs, xs, xs, mod, norm_g4, w_in, conv_w, pool_bd, pool_scale)


def _softmax_pv(s, v_all):
    mx = jnp.max(s, axis=-1, keepdims=True)
    p = jnp.exp(s - mx)
    l = jnp.sum(p, axis=-1, keepdims=True)
    o = jnp.dot(p.astype(BF16), v_all, preferred_element_type=F32)
    return o / l


def _out_residual(x, ycp, o_heads, wo_ref, gate):
    y = jnp.concatenate([ycp] + [o.astype(BF16) for o in o_heads], axis=-1)
    return x + gate * jnp.dot(y, wo_ref[...], preferred_element_type=F32)


def _attn_kernel(x_ref, ycp_ref, q_ref, k0_ref, k1_ref, k2_ref, v0_ref, v1_ref, v2_ref,
                 kc_ref, vc_ref, bias_ref, mod_ref, wo_ref, o_ref):
    n_loc = NKB * TOK
    o_heads = []
    for h in range(NA_HEADS):
        hs = slice(h * NA_HEAD_DIM, (h + 1) * NA_HEAD_DIM)
        k_all = jnp.concatenate([k0_ref[:, hs], k1_ref[:, hs], k2_ref[:, hs], kc_ref[:, hs]], axis=0)
        v_all = jnp.concatenate([v0_ref[:, hs], v1_ref[:, hs], v2_ref[:, hs], vc_ref[:, hs]], axis=0)
        s = lax.dot_general(q_ref[:, hs], k_all, (((1,), (1,)), ((), ())),
                            preferred_element_type=F32)
        s = jnp.concatenate([s[:, :n_loc] + bias_ref[h], s[:, n_loc:]], axis=-1)
        o_heads.append(_softmax_pv(s, v_all))
    gate = mod_ref[5:6]
    o_ref[...] = _out_residual(x_ref[...], ycp_ref[...], o_heads, wo_ref, gate)


def _attention(xs, mix, bias, mod, w_out, *, layer, n_batch, rows):
    nrb = rows // QROWS
    tiles_per_seq = nrb
    n_lat_tiles = n_batch * tiles_per_seq
    tok_blk = lambda rb, b: b * tiles_per_seq + rb
    ksb = lambda rb: jnp.clip(rb - 1, 0, nrb - NKB)
    geom = lambda rb: jnp.where(rb == 0, 0, jnp.where(rb == nrb - 1, 2, 1))

    def kv_spec(c, j):
        return pl.BlockSpec((TOK, NA_CH), lambda rb, b: (b * tiles_per_seq + ksb(rb) + j, c))

    in_specs = [
        pl.BlockSpec((TOK, D_MODEL), lambda rb, b: (tok_blk(rb, b), 0)),
        pl.BlockSpec((TOK, NA_CH), lambda rb, b: (tok_blk(rb, b), 0)),
        pl.BlockSpec((TOK, NA_CH), lambda rb, b: (tok_blk(rb, b), 1)),
        kv_spec(2, 0), kv_spec(2, 1), kv_spec(2, 2),
        kv_spec(3, 0), kv_spec(3, 1), kv_spec(3, 2),
        pl.BlockSpec((TOK, NA_CH), lambda rb, b: (n_lat_tiles + b, 2)),
        pl.BlockSpec((TOK, NA_CH), lambda rb, b: (n_lat_tiles + b, 3)),
        pl.BlockSpec((None, None, NA_HEADS, TOK, NKB * TOK),
                     lambda rb, b: (geom(rb), layer, 0, 0, 0)),
        pl.BlockSpec((None, None, N_MOD, D_MODEL), lambda rb, b: (layer, b, 0, 0)),
        pl.BlockSpec((None, D_MODEL, D_MODEL), lambda rb, b: (layer, 0, 0)),
    ]
    return pl.pallas_call(
        _attn_kernel,
        grid=(nrb, n_batch),
        in_specs=in_specs,
        out_specs=pl.BlockSpec((TOK, D_MODEL), lambda rb, b: (tok_blk(rb, b), 0)),
        out_shape=jax.ShapeDtypeStruct(xs.shape, F32),
        input_output_aliases={0: 0},
        compiler_params=pltpu.CompilerParams(
            dimension_semantics=("arbitrary", "arbitrary"), vmem_limit_bytes=VMEM_LIMIT),
        name="attention",
    )(xs, mix, mix, mix, mix, mix, mix, mix, mix, mix, mix, bias, mod, w_out)


def _ctx_kernel(x_ref, ycp_ref, q_ref, k_ref, v_ref, mod_ref, wo_ref, o_ref):
    o_heads = []
    for h in range(NA_HEADS):
        hs = slice(h * NA_HEAD_DIM, (h + 1) * NA_HEAD_DIM)
        s = lax.dot_general(q_ref[:, hs], k_ref[:, hs], (((1,), (1,)), ((), ())),
                            preferred_element_type=F32)
        o_heads.append(_softmax_pv(s, v_ref[:, hs]))
    gate = mod_ref[5:6]
    o_ref[...] = _out_residual(x_ref[...], ycp_ref[...], o_heads, wo_ref, gate)


def _ctx_attention(xs, mix, mod, w_out, *, layer, n_batch, n_lat_tiles):
    blk = lambda b: n_lat_tiles + b
    return pl.pallas_call(
        _ctx_kernel,
        grid=(n_batch,),
        in_specs=[
            pl.BlockSpec((TOK, D_MODEL), lambda b: (blk(b), 0)),
            pl.BlockSpec((TOK, NA_CH), lambda b: (blk(b), 0)),
            pl.BlockSpec((TOK, NA_CH), lambda b: (blk(b), 1)),
            pl.BlockSpec((TOK, NA_CH), lambda b: (blk(b), 2)),
            pl.BlockSpec((TOK, NA_CH), lambda b: (blk(b), 3)),
            pl.BlockSpec((None, None, N_MOD, D_MODEL), lambda b: (layer, n_batch, 0, 0)),
            pl.BlockSpec((None, D_MODEL, D_MODEL), lambda b: (layer, 0, 0)),
        ],
        out_specs=pl.BlockSpec((TOK, D_MODEL), lambda b: (blk(b), 0)),
        out_shape=jax.ShapeDtypeStruct(xs.shape, F32),
        input_output_aliases={0: 0},
        compiler_params=pltpu.CompilerParams(
            dimension_semantics=("arbitrary",), vmem_limit_bytes=VMEM_LIMIT),
        name="ctx_attention",
    )(xs, mix, mix, mix, mix, mod, w_out)


def _runs(idx):
    runs, a = [], 0
    for p in range(1, len(idx) + 1):
        if p == len(idx) or idx[p] != idx[p - 1] + 1:
            runs.append((idx[a], idx[p - 1] + 1))
            a = p
    return runs


def _expand_bias(rpb, rows):
    depth = rpb.shape[0]
    hh = depth * NA_HEADS
    kh = min(NA_KH_MAX, rows)
    nrb = rows // QROWS
    n_off = 2 * NA_KH_MAX - 1
    n_col = 2 * NA_KW - 1
    width = 2 * GRID_W
    start = GRID_W - NA_KW
    w = jnp.full((hh, n_off, width), NEG_INF, F32)
    w = w.at[:, :, start:start + n_col].set(rpb.reshape(hh, n_off, n_col))
    flat = jnp.tile(w, (1, 1, GRID_W))[..., :GRID_W * (width - 1)]
    t = flat.reshape(hh, n_off, GRID_W, width - 1)[..., GRID_W - 1:]
    qc = np.arange(GRID_W)
    cstart = np.clip(qc - NA_KW // 2, 0, GRID_W - NA_KW)
    col_valid = (qc[None, :] >= cstart[:, None]) & (qc[None, :] < cstart[:, None] + NA_KW)
    t = jnp.where(col_valid, t, NEG_INF)
    t = jnp.concatenate([t, jnp.full((hh, 1, GRID_W, GRID_W), NEG_INF, F32)], axis=1)

    period = QROWS + KROWS
    geoms = []
    for rb in (0, 1, nrb - 1):
        r = rb * QROWS + np.arange(QROWS)
        ks = QROWS * int(np.clip(rb - 1, 0, nrb - NKB))
        kr = ks + np.arange(KROWS)
        rs = np.clip(r - kh // 2, 0, rows - kh)
        row_valid = (kr[None, :] >= rs[:, None]) & (kr[None, :] < rs[:, None] + kh)
        idx = []
        for p in range(period):
            dr = (p if p < KROWS else p - period) + ks - rb * QROWS
            idx.append(dr + NA_KH_MAX - 1 if abs(dr) < NA_KH_MAX else n_off)
        wblk = jnp.concatenate([t[:, a:b] for a, b in _runs(idx)], axis=1)
        sk = jnp.tile(wblk, (1, QROWS, 1, 1))[:, :QROWS * (period - 1)]
        sk = sk.reshape(hh, QROWS, period - 1, GRID_W, GRID_W)[:, :, :KROWS]
        sk = jnp.where(row_valid[None, :, :, None, None], sk, NEG_INF)
        geoms.append(sk.transpose(0, 1, 3, 2, 4).reshape(depth, NA_HEADS, TOK, KROWS * GRID_W))
    return jnp.stack(geoms)


def _pool_block_diag(pool_w):
    depth, n = pool_w.shape[:2]
    out = jnp.zeros((depth, POOL_CH, POOL_CH), pool_w.dtype)
    for gi in range(n):
        sl = slice(gi * POOL_GROUP, (gi + 1) * POOL_GROUP)
        out = out.at[:, sl, sl].set(pool_w[:, gi])
    return out


def kernel(x, c, ctx, c_ctx, w_mod, b_mod, norm_g, ffn_w_in, ffn_w_out, w_in, conv_w, pool_w,
           pool_scale, rpb, w_out, final_g):
    n_batch, seq, d = x.shape
    ctx_len = ctx.shape[1]
    depth = w_mod.shape[0]
    rows = seq // GRID_W
    assert d == D_MODEL and ctx_len == TOK and seq % FFN_TM == 0 and rows >= KROWS
    assert (n_batch * ctx_len) % FFN_TM == 0
    assert n_batch + 1 <= MOD_ROWS and w_in.shape[-1] == D_IN and ffn_w_in.shape[-1] == 2 * D_FF
    n_lat = n_batch * seq
    n_lat_tiles = n_lat // TOK
    tiles_per_seq = seq // TOK
    lat_ffn_tiles = n_lat // FFN_TM
    ctx_ffn_tiles = n_batch * ctx_len // FFN_TM

    cc = jnp.zeros((MOD_ROWS, d), F32).at[:n_batch].set(c).at[n_batch].set(c_ctx)
    mod = _modulation(cc, w_mod, b_mod).reshape(depth, MOD_ROWS, N_MOD, d)

    w1 = ffn_w_in.astype(BF16)
    w2 = ffn_w_out.astype(BF16)
    w_in_b = w_in.astype(BF16)
    w_out_b = w_out.astype(BF16)
    pool_bd = _pool_block_diag(pool_w).astype(BF16)
    pool_sc = pool_scale.reshape(depth, 1, POOL_CH)
    norm_g4 = norm_g.reshape(depth, 3, 1, d)
    bias = _expand_bias(rpb, rows)

    ffn_kw = dict(lat_tiles_per_batch=seq // FFN_TM, n_batch=n_batch)
    xa, xb, xb_off = x.reshape(n_lat, d), ctx.reshape(n_batch * ctx_len, d), 0
    for l in range(depth):
        last = l == depth - 1
        xs = _ffn(xa, xb, xb_off, lat_ffn_tiles, ctx_ffn_tiles, mod, norm_g4, w1, w2, final_g,
                  layer=l, which=0, sub=0, final=False, **ffn_kw)
        mix = _inproj(xs, mod, norm_g4, w_in_b, conv_w, pool_bd, pool_sc, layer=l,
                      n_lat_tiles=n_lat_tiles, lat_tiles_per_seq=tiles_per_seq, n_batch=n_batch)
        xs = _attention(xs, mix, bias, mod, w_out_b, layer=l, n_batch=n_batch, rows=rows)
        if not last:
            xs = _ctx_attention(xs, mix, mod, w_out_b, layer=l, n_batch=n_batch,
                                n_lat_tiles=n_lat_tiles)
        xs = _ffn(xs, xs, lat_ffn_tiles, lat_ffn_tiles, 0 if last else ctx_ffn_tiles, mod, norm_g4,
                  w1, w2, final_g, layer=l, which=1, sub=2, final=last, **ffn_kw)
        xa, xb, xb_off = xs, xs, lat_ffn_tiles
    return xs.reshape(n_batch, seq, d)
```

```python
import functools

import numpy as np
import jax
import jax.numpy as jnp
from jax import lax
from jax.experimental import pallas as pl
from jax.experimental.pallas import tpu as pltpu

F32 = jnp.float32
BF16 = jnp.bfloat16

D_MODEL = 1024
GRID_W = 64
D_FF = 2816
N_MOD = 9
RMS_EPS = 1e-6
NEG_INF = -1e30
CONV_CH = D_MODEL // 4
POOL_CH = D_MODEL // 4
POOL_WINDOWS = (2, 4, 8, 16)
POOL_GROUP = POOL_CH // 4
NA_HEAD_DIM = 64
NA_CH = D_MODEL // 2
NA_HEADS = NA_CH // NA_HEAD_DIM
NA_KH_MAX = 8
NA_KW = 16
OFF_Q = 3 * CONV_CH + POOL_CH
D_IN = OFF_Q + 3 * NA_CH

FFN_TM = 512
FFN_CHUNKS = ((0, 1536), (1536, 2816))
TOK = 256
HALO = 8
QROWS = TOK // GRID_W
KROWS = QROWS + NA_KH_MAX
NKB = KROWS * GRID_W // TOK
MOD_ROWS = 16
MOD_TN = 2304
VMEM_LIMIT = 56 * 1024 * 1024


def _rms_mod(x, g, scale, shift):
    ms = jnp.mean(x * x, axis=-1, keepdims=True)
    return (x * lax.rsqrt(ms + RMS_EPS) * g) * (1.0 + scale) + shift


def _mod_kernel(c_ref, w_ref, b_ref, o_ref):
    c = c_ref[...]
    s = (c * jax.nn.sigmoid(c)).astype(BF16)
    o_ref[...] = jnp.dot(s, w_ref[...].astype(BF16), preferred_element_type=F32) + b_ref[...]


def _modulation(cc, w_mod, b_mod):
    depth = w_mod.shape[0]
    n = N_MOD * D_MODEL
    return pl.pallas_call(
        _mod_kernel,
        grid=(depth, n // MOD_TN),
        in_specs=[
            pl.BlockSpec((MOD_ROWS, D_MODEL), lambda l, j: (0, 0)),
            pl.BlockSpec((None, D_MODEL, MOD_TN), lambda l, j: (l, 0, j)),
            pl.BlockSpec((None, 1, MOD_TN), lambda l, j: (l, 0, j)),
        ],
        out_specs=pl.BlockSpec((None, MOD_ROWS, MOD_TN), lambda l, j: (l, 0, j)),
        out_shape=jax.ShapeDtypeStruct((depth, MOD_ROWS, n), F32),
        compiler_params=pltpu.CompilerParams(
            dimension_semantics=("arbitrary", "arbitrary"), vmem_limit_bytes=VMEM_LIMIT),
        name="modulation",
    )(cc, w_mod, b_mod.reshape(depth, 1, n))


def _ffn_kernel(xa_ref, xb_ref, mod_ref, g_ref, w1_ref, w2_ref, fg_ref, o_ref, *, sub, final,
                n_a_tiles, n_b_tiles):
    if n_b_tiles:
        x = jnp.where(pl.program_id(0) < n_a_tiles, xa_ref[...], xb_ref[...])
    else:
        x = xa_ref[...]
    m = mod_ref[...]
    shift = m[3 * sub:3 * sub + 1]
    scale = m[3 * sub + 1:3 * sub + 2]
    gate = m[3 * sub + 2:3 * sub + 3]
    hb = _rms_mod(x, g_ref[...], scale, shift).astype(BF16)
    acc = None
    for lo, hi in FFN_CHUNKS:
        a = jnp.dot(hb, w1_ref[:, lo:hi], preferred_element_type=F32)
        b = jnp.dot(hb, w1_ref[:, D_FF + lo:D_FF + hi], preferred_element_type=F32)
        gg = (a * jax.nn.sigmoid(a) * b).astype(BF16)
        part = jnp.dot(gg, w2_ref[lo:hi, :], preferred_element_type=F32)
        acc = part if acc is None else acc + part
    y = x + (0.5 * gate) * acc
    if final:
        ms = jnp.mean(y * y, axis=-1, keepdims=True)
        y = y * lax.rsqrt(ms + RMS_EPS) * fg_ref[...]
    o_ref[...] = y


def _ffn(xa, xb, xb_off, n_a_tiles, n_b_tiles, mod, norm_g4, w1, w2, final_g, *, layer, which, sub,
         final, lat_tiles_per_batch, n_batch):
    n_tiles = n_a_tiles + n_b_tiles
    const = lambda i: (0, 0)
    kern = functools.partial(_ffn_kernel, sub=sub, final=final, n_a_tiles=n_a_tiles,
                             n_b_tiles=n_b_tiles)
    return pl.pallas_call(
        kern,
        grid=(n_tiles,),
        in_specs=[
            pl.BlockSpec((FFN_TM, D_MODEL), lambda i: (jnp.minimum(i, n_a_tiles - 1), 0)),
            pl.BlockSpec((FFN_TM, D_MODEL), lambda i: (jnp.maximum(i - n_a_tiles, 0) + xb_off, 0)),
            pl.BlockSpec((None, None, N_MOD, D_MODEL),
                         lambda i: (layer, jnp.minimum(i // lat_tiles_per_batch, n_batch), 0, 0)),
            pl.BlockSpec((None, None, 1, D_MODEL), lambda i: (layer, sub, 0, 0)),
            pl.BlockSpec((None, None, D_MODEL, 2 * D_FF), lambda i: (layer, which, 0, 0),
                         pipeline_mode=pl.Buffered(1)),
            pl.BlockSpec((None, None, D_FF, D_MODEL), lambda i: (layer, which, 0, 0),
                         pipeline_mode=pl.Buffered(1)),
            pl.BlockSpec((1, D_MODEL), const),
        ],
        out_specs=pl.BlockSpec((FFN_TM, D_MODEL), lambda i: (i, 0)),
        out_shape=jax.ShapeDtypeStruct((n_tiles * FFN_TM, D_MODEL), F32),
        compiler_params=pltpu.CompilerParams(
            dimension_semantics=("arbitrary",), vmem_limit_bytes=VMEM_LIMIT),
        name="ffn",
    )(xa, xb, mod, norm_g4, w1, w2, final_g.reshape(1, D_MODEL))


def _inproj_kernel(xp_ref, x_ref, xn_ref, mod_ref, g_ref, w_ref, cw_ref, pw_ref, ps_ref, o_ref,
                   *, n_lat_tiles, lat_tiles_per_seq):
    i = pl.program_id(0)
    is_lat = i < n_lat_tiles
    pos = jnp.where(is_lat, i % lat_tiles_per_seq, 0)
    ntile = jnp.where(is_lat, lat_tiles_per_seq, 1)
    first = pos == 0
    last = pos == ntile - 1
    t0 = pos * TOK
    seq_len = ntile * TOK

    m = mod_ref[...]
    shift, scale = m[3:4], m[4:5]
    g = g_ref[...]
    ext = TOK + 2 * HALO
    xe = jnp.concatenate([xp_ref[...], x_ref[...], xn_ref[...]], axis=0)
    he = _rms_mod(xe, g, scale, shift).astype(BF16)

    u = jnp.dot(he, w_ref[:, :OFF_Q], preferred_element_type=F32)
    row = lax.broadcasted_iota(jnp.int32, (ext, 1), 0)
    keep_lo = jnp.where(first, HALO, 0)
    keep_hi = jnp.where(last, HALO + TOK, ext)
    u = jnp.where((row >= keep_lo) & (row < keep_hi), u, 0.0)

    def shifted(a, s):
        return pltpu.roll(a, s % ext, 0)

    def centre(a):
        return a[HALO:HALO + TOK]

    z = u[:, 2 * CONV_CH:3 * CONV_CH] * u[:, :CONV_CH]
    cw = cw_ref[...]
    conv = (centre(shifted(z, 1)) * cw[0:1] + centre(z) * cw[1:2] + centre(shifted(z, -1)) * cw[2:3])
    y_conv = centre(u[:, CONV_CH:2 * CONV_CH]) * conv

    v = u[:, 3 * CONV_CH:]
    a1 = v + shifted(v, 1)
    a2 = a1 + shifted(a1, 2)
    a3 = a2 + shifted(a2, 4)
    a4 = a3 + shifted(a3, 8)
    sums = (centre(a1), centre(shifted(a2, -1)), centre(shifted(a3, -3)), centre(shifted(a4, -7)))
    t = t0 + lax.broadcasted_iota(jnp.int32, (TOK, 1), 0)
    lane = lax.broadcasted_iota(jnp.int32, (TOK, POOL_CH), 1)
    mean = None
    for gi in reversed(range(len(POOL_WINDOWS))):
        w = POOL_WINDOWS[gi]
        left = w // 2
        right = w - 1 - left
        cnt = jnp.minimum(t + right + 1, seq_len) - jnp.maximum(t - left, 0)
        mg = sums[gi] / cnt.astype(F32)
        mean = mg if mean is None else jnp.where(lane < (gi + 1) * POOL_GROUP, mg, mean)
    dpool = (mean - centre(v)).astype(BF16)
    y_pool = jnp.dot(dpool, pw_ref[...], preferred_element_type=F32) * ps_ref[...]

    hc = he[HALO:HALO + TOK]
    qkv = jnp.dot(hc, w_ref[:, OFF_Q:], preferred_element_type=F32)
    o_ref[:, :CONV_CH] = y_conv.astype(BF16)
    o_ref[:, CONV_CH:CONV_CH + POOL_CH] = y_pool.astype(BF16)
    o_ref[:, 2 * CONV_CH:2 * CONV_CH + NA_CH] = (qkv[:, :NA_CH] * (NA_HEAD_DIM ** -0.5)).astype(BF16)
    o_ref[:, 2 * CONV_CH + NA_CH:] = qkv[:, NA_CH:].astype(BF16)


def _inproj(xs, mod, norm_g4, w_in, conv_w, pool_bd, pool_scale, *, layer, n_lat_tiles,
            lat_tiles_per_seq, n_batch):
    nt = xs.shape[0]
    n_tiles = nt // TOK
    hb = TOK // HALO
    per_layer = lambda i: (layer, 0, 0)
    kern = functools.partial(_inproj_kernel, n_lat_tiles=n_lat_tiles,
                             lat_tiles_per_seq=lat_tiles_per_seq)
    return pl.pallas_call(
        kern,
        grid=(n_tiles,),
        in_specs=[
            pl.BlockSpec((HALO, D_MODEL), lambda i: (jnp.maximum(i * hb - 1, 0), 0)),
            pl.BlockSpec((TOK, D_MODEL), lambda i: (i, 0)),
            pl.BlockSpec((HALO, D_MODEL), lambda i: (jnp.minimum((i + 1) * hb, nt // HALO - 1), 0)),
            pl.BlockSpec((None, None, N_MOD, D_MODEL),
                         lambda i: (layer, jnp.minimum(i // lat_tiles_per_seq, n_batch), 0, 0)),
            pl.BlockSpec((None, None, 1, D_MODEL), lambda i: (layer, 1, 0, 0)),
            pl.BlockSpec((None, D_MODEL, D_IN), per_layer, pipeline_mode=pl.Buffered(1)),
            pl.BlockSpec((None, 3, CONV_CH), per_layer),
            pl.BlockSpec((None, POOL_CH, POOL_CH), per_layer),
            pl.BlockSpec((None, 1, POOL_CH), per_layer),
        ],
        out_specs=pl.BlockSpec((TOK, 2 * D_MODEL), lambda i: (i, 0)),
        out_shape=jax.ShapeDtypeStruct((nt, 2 * D_MODEL), BF16),
        compiler_params=pltpu.CompilerParams(
            dimension_semantics=("arbitrary",), vmem_limit_bytes=VMEM_LIMIT),
        name="inproj",
    )(xs, xs, xs, mod, norm_g4, w_in, conv_w, pool_bd, pool_scale)


def _pair_attention(q_pair, k_pair, v_pair, add_bias):
    lo = lax.broadcasted_iota(jnp.int32, q_pair.shape, 1) < NA_HEAD_DIM
    qf = q_pair.astype(F32)
    outs = []
    for half in range(2):
        qm = jnp.where(lo if half == 0 else jnp.logical_not(lo), qf, 0.0).astype(BF16)
        s = lax.dot_general(qm, k_pair, (((1,), (1,)), ((), ())), preferred_element_type=F32)
        s = add_bias(half, s)
        mx = jnp.max(s, axis=-1, keepdims=True)
        p = jnp.exp(s - mx)
        l = jnp.sum(p, axis=-1, keepdims=True)
        o = jnp.dot(p.astype(BF16), v_pair, preferred_element_type=F32)
        outs.append(o / l)
    return jnp.where(lo, outs[0], outs[1])


def _out_residual(x, ycp, o_pairs, wo_ref, gate):
    y = jnp.concatenate([ycp] + [o.astype(BF16) for o in o_pairs], axis=-1)
    return x + gate * jnp.dot(y, wo_ref[...], preferred_element_type=F32)


def _fill_bias(bias_buf, rp_ref, rb, nrb):
    width = 2 * GRID_W
    lane = lax.broadcasted_iota(jnp.int32, (GRID_W, width), 1)
    qc = lax.broadcasted_iota(jnp.int32, (GRID_W, width), 0)
    kc = lane & (GRID_W - 1)
    cstart = jnp.clip(qc - NA_KW // 2, 0, GRID_W - NA_KW)
    col_ok = (kc >= cstart) & (kc < cstart + NA_KW)
    lo_half = lane < GRID_W
    is_first = rb == 0
    is_last = rb == nrb - 1
    base = jnp.where(is_first, 0, jnp.where(is_last, -2 * QROWS, -QROWS))
    shift0 = (width - (NA_KW - 1)) % width
    shift1 = (shift0 + GRID_W) % width

    def per_head(h, carry):
        for i in range(QROWS):
            jlo = jnp.where(is_first, 0, jnp.where(is_last, QROWS, i))
            for jj in range(KROWS // 2):
                halves = []
                for j, sh in ((2 * jj, shift0), (2 * jj + 1, shift1)):
                    ok = (j >= jlo) & (j < jlo + NA_KH_MAX)
                    idx = jnp.clip(j - i + base + NA_KH_MAX - 1, 0, 2 * NA_KH_MAX - 2)
                    row = jnp.where(ok, rp_ref[h, pl.ds(idx, 1), :], NEG_INF)
                    halves.append(pltpu.roll(jnp.broadcast_to(row, (GRID_W, width)), sh, 1,
                                             stride=1, stride_axis=0))
                tile = jnp.where(col_ok, jnp.where(lo_half, halves[0], halves[1]), NEG_INF)
                bias_buf[h, i * GRID_W:(i + 1) * GRID_W, jj * width:(jj + 1) * width] = tile
        return carry

    lax.fori_loop(0, NA_HEADS, per_head, 0)


def _attn_kernel(x_ref, yq_ref, kv0_ref, kv1_ref, kv2_ref, kvc_ref, rp_ref, mod_ref, wo_ref, o_ref,
                 kvbuf, bias_buf, *, nrb):
    rb = pl.program_id(0)
    n_loc = NKB * TOK

    @pl.when((pl.program_id(1) == 0) & ((rb <= 1) | (rb == nrb - 1)))
    def _():
        _fill_bias(bias_buf, rp_ref, rb, nrb)

    for j, kv_ref in enumerate((kv0_ref, kv1_ref, kv2_ref, kvc_ref)):
        kvbuf[j * TOK:(j + 1) * TOK] = kv_ref[...]

    o_pairs = []
    for p in range(NA_HEADS // 2):
        lo, hi = 2 * p * NA_HEAD_DIM, 2 * (p + 1) * NA_HEAD_DIM

        def add_bias(half, s, p=p):
            return jnp.concatenate([s[:, :n_loc] + bias_buf[2 * p + half], s[:, n_loc:]], axis=-1)

        o_pairs.append(_pair_attention(yq_ref[:, NA_CH + lo:NA_CH + hi], kvbuf[:, lo:hi],
                                       kvbuf[:, NA_CH + lo:NA_CH + hi], add_bias))
    gate = mod_ref[5:6]
    o_ref[...] = _out_residual(x_ref[...], yq_ref[:, :NA_CH], o_pairs, wo_ref, gate)


def _attention(xs, mix, rp, mod, w_out, *, layer, n_batch, rows):
    nrb = rows // QROWS
    tiles_per_seq = nrb
    n_lat_tiles = n_batch * tiles_per_seq
    tok_blk = lambda rb, b: b * tiles_per_seq + rb
    ksb = lambda rb: jnp.clip(rb - 1, 0, nrb - NKB)

    def kv_spec(j):
        return pl.BlockSpec((TOK, 2 * NA_CH), lambda rb, b: (b * tiles_per_seq + ksb(rb) + j, 1))

    in_specs = [
        pl.BlockSpec((TOK, D_MODEL), lambda rb, b: (tok_blk(rb, b), 0)),
        pl.BlockSpec((TOK, 2 * NA_CH), lambda rb, b: (tok_blk(rb, b), 0)),
        kv_spec(0), kv_spec(1), kv_spec(2),
        pl.BlockSpec((TOK, 2 * NA_CH), lambda rb, b: (n_lat_tiles + b, 1)),
        pl.BlockSpec((None,) + rp.shape[1:], lambda rb, b: (layer, 0, 0, 0)),
        pl.BlockSpec((None, None, N_MOD, D_MODEL), lambda rb, b: (layer, b, 0, 0)),
        pl.BlockSpec((None, D_MODEL, D_MODEL), lambda rb, b: (layer, 0, 0)),
    ]
    return pl.pallas_call(
        functools.partial(_attn_kernel, nrb=nrb),
        grid=(nrb, n_batch),
        in_specs=in_specs,
        out_specs=pl.BlockSpec((TOK, D_MODEL), lambda rb, b: (tok_blk(rb, b), 0)),
        out_shape=jax.ShapeDtypeStruct(xs.shape, F32),
        scratch_shapes=[
            pltpu.VMEM((NKB * TOK + TOK, 2 * NA_CH), BF16),
            pltpu.VMEM((NA_HEADS, TOK, NKB * TOK), F32),
        ],
        input_output_aliases={0: 0},
        compiler_params=pltpu.CompilerParams(
            dimension_semantics=("arbitrary", "arbitrary"), vmem_limit_bytes=VMEM_LIMIT),
        name="attention",
    )(xs, mix, mix, mix, mix, mix, rp, mod, w_out)


def _ctx_kernel(x_ref, yq_ref, kv_ref, mod_ref, wo_ref, o_ref):
    o_pairs = []
    for p in range(NA_HEADS // 2):
        lo, hi = 2 * p * NA_HEAD_DIM, 2 * (p + 1) * NA_HEAD_DIM
        o_pairs.append(_pair_attention(yq_ref[:, NA_CH + lo:NA_CH + hi], kv_ref[:, lo:hi],
                                       kv_ref[:, NA_CH + lo:NA_CH + hi], lambda half, s: s))
    gate = mod_ref[5:6]
    o_ref[...] = _out_residual(x_ref[...], yq_ref[:, :NA_CH], o_pairs, wo_ref, gate)


def _ctx_attention(xs, mix, mod, w_out, *, layer, n_batch, n_lat_tiles):
    blk = lambda b: n_lat_tiles + b
    return pl.pallas_call(
        _ctx_kernel,
        grid=(n_batch,),
        in_specs=[
            pl.BlockSpec((TOK, D_MODEL), lambda b: (blk(b), 0)),
            pl.BlockSpec((TOK, 2 * NA_CH), lambda b: (blk(b), 0)),
            pl.BlockSpec((TOK, 2 * NA_CH), lambda b: (blk(b), 1)),
            pl.BlockSpec((None, None, N_MOD, D_MODEL), lambda b: (layer, n_batch, 0, 0)),
            pl.BlockSpec((None, D_MODEL, D_MODEL), lambda b: (layer, 0, 0)),
        ],
        out_specs=pl.BlockSpec((TOK, D_MODEL), lambda b: (blk(b), 0)),
        out_shape=jax.ShapeDtypeStruct(xs.shape, F32),
        input_output_aliases={0: 0},
        compiler_params=pltpu.CompilerParams(
            dimension_semantics=("arbitrary",), vmem_limit_bytes=VMEM_LIMIT),
        name="ctx_attention",
    )(xs, mix, mix, mod, w_out)


def _pad_rpb(rpb):
    depth, heads, n_off, n_col = rpb.shape
    out = jnp.full((depth, heads, n_off + 1, 2 * GRID_W), NEG_INF, F32)
    return out.at[:, :, :n_off, :n_col].set(rpb)


def _pool_block_diag(pool_w):
    depth, n = pool_w.shape[:2]
    out = jnp.zeros((depth, POOL_CH, POOL_CH), pool_w.dtype)
    for gi in range(n):
        sl = slice(gi * POOL_GROUP, (gi + 1) * POOL_GROUP)
        out = out.at[:, sl, sl].set(pool_w[:, gi])
    return out


def kernel(x, c, ctx, c_ctx, w_mod, b_mod, norm_g, ffn_w_in, ffn_w_out, w_in, conv_w, pool_w,
           pool_scale, rpb, w_out, final_g):
    n_batch, seq, d = x.shape
    ctx_len = ctx.shape[1]
    depth = w_mod.shape[0]
    rows = seq // GRID_W
    assert d == D_MODEL and ctx_len == TOK and seq % FFN_TM == 0 and rows >= KROWS
    assert (n_batch * ctx_len) % FFN_TM == 0 and rows // QROWS >= NKB + 1
    assert NA_KH_MAX // 2 == QROWS and rpb.shape[2:] == (2 * NA_KH_MAX - 1, 2 * NA_KW - 1)
    assert n_batch + 1 <= MOD_ROWS and w_in.shape[-1] == D_IN and ffn_w_in.shape[-1] == 2 * D_FF
    n_lat = n_batch * seq
    n_lat_tiles = n_lat // TOK
    tiles_per_seq = seq // TOK
    lat_ffn_tiles = n_lat // FFN_TM
    ctx_ffn_tiles = n_batch * ctx_len // FFN_TM

    cc = jnp.zeros((MOD_ROWS, d), F32).at[:n_batch].set(c).at[n_batch].set(c_ctx)
    mod = _modulation(cc, w_mod, b_mod).reshape(depth, MOD_ROWS, N_MOD, d)

    w1 = ffn_w_in.astype(BF16)
    w2 = ffn_w_out.astype(BF16)
    w_in_b = w_in.astype(BF16)
    w_out_b = w_out.astype(BF16)
    pool_bd = _pool_block_diag(pool_w).astype(BF16)
    pool_sc = pool_scale.reshape(depth, 1, POOL_CH)
    norm_g4 = norm_g.reshape(depth, 3, 1, d)
    rp = _pad_rpb(rpb)

    ffn_kw = dict(lat_tiles_per_batch=seq // FFN_TM, n_batch=n_batch)
    xa, xb, xb_off = x.reshape(n_lat, d), ctx.reshape(n_batch * ctx_len, d), 0
    for l in range(depth):
        last = l == depth - 1
        xs = _ffn(xa, xb, xb_off, lat_ffn_tiles, ctx_ffn_tiles, mod, norm_g4, w1, w2, final_g,
                  layer=l, which=0, sub=0, final=False, **ffn_kw)
        mix = _inproj(xs, mod, norm_g4, w_in_b, conv_w, pool_bd, pool_sc, layer=l,
                      n_lat_tiles=n_lat_tiles, lat_tiles_per_seq=tiles_per_seq, n_batch=n_batch)
        xs = _attention(xs, mix, rp, mod, w_out_b, layer=l, n_batch=n_batch, rows=rows)
        if not last:
            xs = _ctx_attention(xs, mix, mod, w_out_b, layer=l, n_batch=n_batch,
                                n_lat_tiles=n_lat_tiles)
        xs = _ffn(xs, xs, lat_ffn_tiles, lat_ffn_tiles, 0 if last else ctx_ffn_tiles, mod, norm_g4,
                  w1, w2, final_g, layer=l, which=1, sub=2, final=last, **ffn_kw)
        xa, xb, xb_off = xs, xs, lat_ffn_tiles
    return xs.reshape(n_batch, seq, d)
```

```python
import functools

import numpy as np
import jax
import jax.numpy as jnp
from jax import lax
from jax.experimental import pallas as pl
from jax.experimental.pallas import tpu as pltpu

F32 = jnp.float32
BF16 = jnp.bfloat16

D_MODEL = 1024
GRID_W = 64
D_FF = 2816
N_MOD = 9
RMS_EPS = 1e-6
NEG_INF = -1e30
CONV_CH = D_MODEL // 4
POOL_CH = D_MODEL // 4
POOL_WINDOWS = (2, 4, 8, 16)
POOL_GROUP = POOL_CH // 4
NA_HEAD_DIM = 64
NA_CH = D_MODEL // 2
NA_HEADS = NA_CH // NA_HEAD_DIM
NA_KH_MAX = 8
NA_KW = 16
OFF_Q = 3 * CONV_CH + POOL_CH
D_IN = OFF_Q + 3 * NA_CH

FFN_TM = 512
FFN_CHUNKS = ((0, 1536), (1536, 2816))
TOK = 256
HALO = 8
QROWS = TOK // GRID_W
KROWS = QROWS + NA_KH_MAX
NKB = KROWS * GRID_W // TOK
SCORE_LOOKAHEAD = 1
MOD_ROWS = 16
MOD_TN = 2304
VMEM_LIMIT = 56 * 1024 * 1024


def _rms_mod(x, g, scale, shift):
    ms = jnp.mean(x * x, axis=-1, keepdims=True)
    return (x * lax.rsqrt(ms + RMS_EPS) * g) * (1.0 + scale) + shift


def _mod_kernel(c_ref, w_ref, b_ref, o_ref):
    c = c_ref[...]
    s = (c * jax.nn.sigmoid(c)).astype(BF16)
    o_ref[...] = jnp.dot(s, w_ref[...].astype(BF16), preferred_element_type=F32) + b_ref[...]


def _modulation(cc, w_mod, b_mod):
    depth = w_mod.shape[0]
    n = N_MOD * D_MODEL
    return pl.pallas_call(
        _mod_kernel,
        grid=(depth, n // MOD_TN),
        in_specs=[
            pl.BlockSpec((MOD_ROWS, D_MODEL), lambda l, j: (0, 0)),
            pl.BlockSpec((None, D_MODEL, MOD_TN), lambda l, j: (l, 0, j)),
            pl.BlockSpec((None, 1, MOD_TN), lambda l, j: (l, 0, j)),
        ],
        out_specs=pl.BlockSpec((None, MOD_ROWS, MOD_TN), lambda l, j: (l, 0, j)),
        out_shape=jax.ShapeDtypeStruct((depth, MOD_ROWS, n), F32),
        compiler_params=pltpu.CompilerParams(
            dimension_semantics=("arbitrary", "arbitrary"), vmem_limit_bytes=VMEM_LIMIT),
        name="modulation",
    )(cc, w_mod, b_mod.reshape(depth, 1, n))


def _ffn_kernel(xa_ref, xb_ref, mod_ref, g_ref, w1_ref, w2_ref, fg_ref, o_ref, *, sub, final,
                n_a_tiles, n_b_tiles):
    if n_b_tiles:
        x = jnp.where(pl.program_id(0) < n_a_tiles, xa_ref[...], xb_ref[...])
    else:
        x = xa_ref[...]
    m = mod_ref[...]
    shift = m[3 * sub:3 * sub + 1]
    scale = m[3 * sub + 1:3 * sub + 2]
    gate = m[3 * sub + 2:3 * sub + 3]
    hb = _rms_mod(x, g_ref[...], scale, shift).astype(BF16)
    acc = None
    for lo, hi in FFN_CHUNKS:
        a = jnp.dot(hb, w1_ref[:, lo:hi], preferred_element_type=F32)
        b = jnp.dot(hb, w1_ref[:, D_FF + lo:D_FF + hi], preferred_element_type=F32)
        gg = (a * jax.nn.sigmoid(a) * b).astype(BF16)
        part = jnp.dot(gg, w2_ref[lo:hi, :], preferred_element_type=F32)
        acc = part if acc is None else acc + part
    y = x + (0.5 * gate) * acc
    if final:
        ms = jnp.mean(y * y, axis=-1, keepdims=True)
        y = y * lax.rsqrt(ms + RMS_EPS) * fg_ref[...]
    o_ref[...] = y


def _ffn(xa, xb, xb_off, n_a_tiles, n_b_tiles, mod, norm_g4, w1, w2, final_g, *, layer, which, sub,
         final, lat_tiles_per_batch, n_batch):
    n_tiles = n_a_tiles + n_b_tiles
    const = lambda i: (0, 0)
    kern = functools.partial(_ffn_kernel, sub=sub, final=final, n_a_tiles=n_a_tiles,
                             n_b_tiles=n_b_tiles)
    return pl.pallas_call(
        kern,
        grid=(n_tiles,),
        in_specs=[
            pl.BlockSpec((FFN_TM, D_MODEL), lambda i: (jnp.minimum(i, n_a_tiles - 1), 0)),
            pl.BlockSpec((FFN_TM, D_MODEL), lambda i: (jnp.maximum(i - n_a_tiles, 0) + xb_off, 0)),
            pl.BlockSpec((None, None, N_MOD, D_MODEL),
                         lambda i: (layer, jnp.minimum(i // lat_tiles_per_batch, n_batch), 0, 0)),
            pl.BlockSpec((None, None, 1, D_MODEL), lambda i: (layer, sub, 0, 0)),
            pl.BlockSpec((None, None, D_MODEL, 2 * D_FF), lambda i: (layer, which, 0, 0),
                         pipeline_mode=pl.Buffered(1)),
            pl.BlockSpec((None, None, D_FF, D_MODEL), lambda i: (layer, which, 0, 0),
                         pipeline_mode=pl.Buffered(1)),
            pl.BlockSpec((1, D_MODEL), const),
        ],
        out_specs=pl.BlockSpec((FFN_TM, D_MODEL), lambda i: (i, 0)),
        out_shape=jax.ShapeDtypeStruct((n_tiles * FFN_TM, D_MODEL), F32),
        compiler_params=pltpu.CompilerParams(
            dimension_semantics=("arbitrary",), vmem_limit_bytes=VMEM_LIMIT),
        name="ffn",
    )(xa, xb, mod, norm_g4, w1, w2, final_g.reshape(1, D_MODEL))


def _inproj_kernel(xp_ref, x_ref, xn_ref, mod_ref, g_ref, w_ref, cw_ref, pw_ref, ps_ref, o_ref,
                   *, n_lat_tiles, lat_tiles_per_seq):
    i = pl.program_id(0)
    is_lat = i < n_lat_tiles
    pos = jnp.where(is_lat, i % lat_tiles_per_seq, 0)
    ntile = jnp.where(is_lat, lat_tiles_per_seq, 1)
    first = pos == 0
    last = pos == ntile - 1
    t0 = pos * TOK
    seq_len = ntile * TOK

    m = mod_ref[...]
    shift, scale = m[3:4], m[4:5]
    g = g_ref[...]
    ext = TOK + 2 * HALO
    xe = jnp.concatenate([xp_ref[...], x_ref[...], xn_ref[...]], axis=0)
    he = _rms_mod(xe, g, scale, shift).astype(BF16)

    u = jnp.dot(he, w_ref[:, :OFF_Q], preferred_element_type=F32)
    row = lax.broadcasted_iota(jnp.int32, (ext, 1), 0)
    keep_lo = jnp.where(first, HALO, 0)
    keep_hi = jnp.where(last, HALO + TOK, ext)
    u = jnp.where((row >= keep_lo) & (row < keep_hi), u, 0.0)

    def shifted(a, s):
        return pltpu.roll(a, s % ext, 0)

    def centre(a):
        return a[HALO:HALO + TOK]

    z = u[:, 2 * CONV_CH:3 * CONV_CH] * u[:, :CONV_CH]
    cw = cw_ref[...]
    conv = (centre(shifted(z, 1)) * cw[0:1] + centre(z) * cw[1:2] + centre(shifted(z, -1)) * cw[2:3])
    y_conv = centre(u[:, CONV_CH:2 * CONV_CH]) * conv

    v = u[:, 3 * CONV_CH:]
    a1 = v + shifted(v, 1)
    a2 = a1 + shifted(a1, 2)
    a3 = a2 + shifted(a2, 4)
    a4 = a3 + shifted(a3, 8)
    sums = (centre(a1), centre(shifted(a2, -1)), centre(shifted(a3, -3)), centre(shifted(a4, -7)))
    t = t0 + lax.broadcasted_iota(jnp.int32, (TOK, 1), 0)
    lane = lax.broadcasted_iota(jnp.int32, (TOK, POOL_CH), 1)
    mean = None
    for gi in reversed(range(len(POOL_WINDOWS))):
        w = POOL_WINDOWS[gi]
        left = w // 2
        right = w - 1 - left
        cnt = jnp.minimum(t + right + 1, seq_len) - jnp.maximum(t - left, 0)
        mg = sums[gi] / cnt.astype(F32)
        mean = mg if mean is None else jnp.where(lane < (gi + 1) * POOL_GROUP, mg, mean)
    dpool = (mean - centre(v)).astype(BF16)
    y_pool = jnp.dot(dpool, pw_ref[...], preferred_element_type=F32) * ps_ref[...]

    hc = he[HALO:HALO + TOK]
    qkv = jnp.dot(hc, w_ref[:, OFF_Q:], preferred_element_type=F32)
    o_ref[:, :CONV_CH] = y_conv.astype(BF16)
    o_ref[:, CONV_CH:CONV_CH + POOL_CH] = y_pool.astype(BF16)
    o_ref[:, 2 * CONV_CH:2 * CONV_CH + NA_CH] = (qkv[:, :NA_CH] * (NA_HEAD_DIM ** -0.5)).astype(BF16)
    o_ref[:, 2 * CONV_CH + NA_CH:] = qkv[:, NA_CH:].astype(BF16)


def _inproj(xs, mod, norm_g4, w_in, conv_w, pool_bd, pool_scale, *, layer, n_lat_tiles,
            lat_tiles_per_seq, n_batch):
    nt = xs.shape[0]
    n_tiles = nt // TOK
    hb = TOK // HALO
    per_layer = lambda i: (layer, 0, 0)
    kern = functools.partial(_inproj_kernel, n_lat_tiles=n_lat_tiles,
                             lat_tiles_per_seq=lat_tiles_per_seq)
    return pl.pallas_call(
        kern,
        grid=(n_tiles,),
        in_specs=[
            pl.BlockSpec((HALO, D_MODEL), lambda i: (jnp.maximum(i * hb - 1, 0), 0)),
            pl.BlockSpec((TOK, D_MODEL), lambda i: (i, 0)),
            pl.BlockSpec((HALO, D_MODEL), lambda i: (jnp.minimum((i + 1) * hb, nt // HALO - 1), 0)),
            pl.BlockSpec((None, None, N_MOD, D_MODEL),
                         lambda i: (layer, jnp.minimum(i // lat_tiles_per_seq, n_batch), 0, 0)),
            pl.BlockSpec((None, None, 1, D_MODEL), lambda i: (layer, 1, 0, 0)),
            pl.BlockSpec((None, D_MODEL, D_IN), per_layer, pipeline_mode=pl.Buffered(1)),
            pl.BlockSpec((None, 3, CONV_CH), per_layer),
            pl.BlockSpec((None, POOL_CH, POOL_CH), per_layer),
            pl.BlockSpec((None, 1, POOL_CH), per_layer),
        ],
        out_specs=pl.BlockSpec((TOK, 2 * D_MODEL), lambda i: (i, 0)),
        out_shape=jax.ShapeDtypeStruct((nt, 2 * D_MODEL), BF16),
        compiler_params=pltpu.CompilerParams(
            dimension_semantics=("arbitrary",), vmem_limit_bytes=VMEM_LIMIT),
        name="inproj",
    )(xs, xs, xs, mod, norm_g4, w_in, conv_w, pool_bd, pool_scale)


def _pair_lanes(p, offset=0):
    return slice(offset + 2 * p * NA_HEAD_DIM, offset + 2 * (p + 1) * NA_HEAD_DIM)


def _attend_heads(q_of, k_of, v_of, add_bias):
    lane = lax.broadcasted_iota(jnp.int32, (1, 2 * NA_HEAD_DIM), 1)
    own = [jnp.where(lane < NA_HEAD_DIM, 1.0, 0.0).astype(BF16),
           jnp.where(lane < NA_HEAD_DIM, 0.0, 1.0).astype(BF16)]

    def scores(t):
        p, half = divmod(t, 2)
        s = lax.dot_general(q_of(p) * own[half], k_of(p), (((1,), (1,)), ((), ())),
                            preferred_element_type=F32)
        return add_bias(t, s)

    outs = []
    pending = [scores(t) for t in range(SCORE_LOOKAHEAD)]
    for t in range(NA_HEADS):
        s = pending.pop(0)
        if t + SCORE_LOOKAHEAD < NA_HEADS:
            pending.append(scores(t + SCORE_LOOKAHEAD))
        vm = v_of(t // 2) * own[t % 2] + own[1 - t % 2]
        mx = jnp.max(s, axis=-1, keepdims=True)
        p = jnp.exp(s - mx).astype(BF16)
        outs.append(jnp.dot(p, vm, preferred_element_type=F32))
    o_pairs = []
    for p in range(NA_HEADS // 2):
        o0, o1 = outs[2 * p], outs[2 * p + 1]
        lo = lax.broadcasted_iota(jnp.int32, o0.shape, 1) < NA_HEAD_DIM
        num = jnp.where(lo, o0, o1)
        den = jnp.where(lo, pltpu.roll(o0, NA_HEAD_DIM, 1), pltpu.roll(o1, NA_HEAD_DIM, 1))
        o_pairs.append(num / den)
    return o_pairs


def _out_residual(x, ycp, o_pairs, wo_ref, gate):
    y = jnp.concatenate([ycp] + [o.astype(BF16) for o in o_pairs], axis=-1)
    return x + gate * jnp.dot(y, wo_ref[...], preferred_element_type=F32)


def _fill_bias(bias_buf, rp_ref, rb, nrb):
    width = 2 * GRID_W
    lane = lax.broadcasted_iota(jnp.int32, (GRID_W, width), 1)
    qc = lax.broadcasted_iota(jnp.int32, (GRID_W, width), 0)
    kc = lane & (GRID_W - 1)
    cstart = jnp.clip(qc - NA_KW // 2, 0, GRID_W - NA_KW)
    col_ok = (kc >= cstart) & (kc < cstart + NA_KW)
    lo_half = lane < GRID_W
    is_first = rb == 0
    is_last = rb == nrb - 1
    base = jnp.where(is_first, 0, jnp.where(is_last, -2 * QROWS, -QROWS))
    shift0 = (width - (NA_KW - 1)) % width
    shift1 = (shift0 + GRID_W) % width

    def per_head(h, carry):
        for i in range(QROWS):
            jlo = jnp.where(is_first, 0, jnp.where(is_last, QROWS, i))
            for jj in range(KROWS // 2):
                halves = []
                for j, sh in ((2 * jj, shift0), (2 * jj + 1, shift1)):
                    ok = (j >= jlo) & (j < jlo + NA_KH_MAX)
                    idx = jnp.clip(j - i + base + NA_KH_MAX - 1, 0, 2 * NA_KH_MAX - 2)
                    row = jnp.where(ok, rp_ref[h, pl.ds(idx, 1), :], NEG_INF)
                    halves.append(pltpu.roll(jnp.broadcast_to(row, (GRID_W, width)), sh, 1,
                                             stride=1, stride_axis=0))
                tile = jnp.where(col_ok, jnp.where(lo_half, halves[0], halves[1]), NEG_INF)
                bias_buf[h, i * GRID_W:(i + 1) * GRID_W, jj * width:(jj + 1) * width] = tile
        return carry

    lax.fori_loop(0, NA_HEADS, per_head, 0)


def _attn_kernel(x_ref, yq_ref, kv0_ref, kv1_ref, kv2_ref, kvc_ref, rp_ref, mod_ref, wo_ref, o_ref,
                 kvbuf, bias_buf, *, nrb):
    rb = pl.program_id(0)
    n_loc = NKB * TOK

    @pl.when((pl.program_id(1) == 0) & ((rb <= 1) | (rb == nrb - 1)))
    def _():
        _fill_bias(bias_buf, rp_ref, rb, nrb)

    for j, kv_ref in enumerate((kv0_ref, kv1_ref, kv2_ref, kvc_ref)):
        kvbuf[j * TOK:(j + 1) * TOK] = kv_ref[...]

    def add_bias(h, s):
        return jnp.concatenate([s[:, :n_loc] + bias_buf[h], s[:, n_loc:]], axis=-1)

    o_pairs = _attend_heads(lambda p: yq_ref[:, _pair_lanes(p, NA_CH)],
                            lambda p: kvbuf[:, _pair_lanes(p)],
                            lambda p: kvbuf[:, _pair_lanes(p, NA_CH)], add_bias)
    gate = mod_ref[5:6]
    o_ref[...] = _out_residual(x_ref[...], yq_ref[:, :NA_CH], o_pairs, wo_ref, gate)


def _attention(xs, mix, rp, mod, w_out, *, layer, n_batch, rows):
    nrb = rows // QROWS
    tiles_per_seq = nrb
    n_lat_tiles = n_batch * tiles_per_seq
    tok_blk = lambda rb, b: b * tiles_per_seq + rb
    ksb = lambda rb: jnp.clip(rb - 1, 0, nrb - NKB)

    def kv_spec(j):
        return pl.BlockSpec((TOK, 2 * NA_CH), lambda rb, b: (b * tiles_per_seq + ksb(rb) + j, 1))

    in_specs = [
        pl.BlockSpec((TOK, D_MODEL), lambda rb, b: (tok_blk(rb, b), 0)),
        pl.BlockSpec((TOK, 2 * NA_CH), lambda rb, b: (tok_blk(rb, b), 0)),
        kv_spec(0), kv_spec(1), kv_spec(2),
        pl.BlockSpec((TOK, 2 * NA_CH), lambda rb, b: (n_lat_tiles + b, 1)),
        pl.BlockSpec((None,) + rp.shape[1:], lambda rb, b: (layer, 0, 0, 0)),
        pl.BlockSpec((None, None, N_MOD, D_MODEL), lambda rb, b: (layer, b, 0, 0)),
        pl.BlockSpec((None, D_MODEL, D_MODEL), lambda rb, b: (layer, 0, 0)),
    ]
    return pl.pallas_call(
        functools.partial(_attn_kernel, nrb=nrb),
        grid=(nrb, n_batch),
        in_specs=in_specs,
        out_specs=pl.BlockSpec((TOK, D_MODEL), lambda rb, b: (tok_blk(rb, b), 0)),
        out_shape=jax.ShapeDtypeStruct(xs.shape, F32),
        scratch_shapes=[
            pltpu.VMEM((NKB * TOK + TOK, 2 * NA_CH), BF16),
            pltpu.VMEM((NA_HEADS, TOK, NKB * TOK), F32),
        ],
        input_output_aliases={0: 0},
        compiler_params=pltpu.CompilerParams(
            dimension_semantics=("arbitrary", "arbitrary"), vmem_limit_bytes=VMEM_LIMIT),
        name="attention",
    )(xs, mix, mix, mix, mix, mix, rp, mod, w_out)


def _ctx_kernel(x_ref, yq_ref, kv_ref, mod_ref, wo_ref, o_ref):
    o_pairs = _attend_heads(lambda p: yq_ref[:, _pair_lanes(p, NA_CH)],
                            lambda p: kv_ref[:, _pair_lanes(p)],
                            lambda p: kv_ref[:, _pair_lanes(p, NA_CH)], lambda h, s: s)
    gate = mod_ref[5:6]
    o_ref[...] = _out_residual(x_ref[...], yq_ref[:, :NA_CH], o_pairs, wo_ref, gate)


def _ctx_attention(xs, mix, mod, w_out, *, layer, n_batch, n_lat_tiles):
    blk = lambda b: n_lat_tiles + b
    return pl.pallas_call(
        _ctx_kernel,
        grid=(n_batch,),
        in_specs=[
            pl.BlockSpec((TOK, D_MODEL), lambda b: (blk(b), 0)),
            pl.BlockSpec((TOK, 2 * NA_CH), lambda b: (blk(b), 0)),
            pl.BlockSpec((TOK, 2 * NA_CH), lambda b: (blk(b), 1)),
            pl.BlockSpec((None, None, N_MOD, D_MODEL), lambda b: (layer, n_batch, 0, 0)),
            pl.BlockSpec((None, D_MODEL, D_MODEL), lambda b: (layer, 0, 0)),
        ],
        out_specs=pl.BlockSpec((TOK, D_MODEL), lambda b: (blk(b), 0)),
        out_shape=jax.ShapeDtypeStruct(xs.shape, F32),
        input_output_aliases={0: 0},
        compiler_params=pltpu.CompilerParams(
            dimension_semantics=("arbitrary",), vmem_limit_bytes=VMEM_LIMIT),
        name="ctx_attention",
    )(xs, mix, mix, mod, w_out)


def _pad_rpb(rpb):
    depth, heads, n_off, n_col = rpb.shape
    out = jnp.full((depth, heads, n_off + 1, 2 * GRID_W), NEG_INF, F32)
    return out.at[:, :, :n_off, :n_col].set(rpb)


def _pool_block_diag(pool_w):
    depth, n = pool_w.shape[:2]
    out = jnp.zeros((depth, POOL_CH, POOL_CH), pool_w.dtype)
    for gi in range(n):
        sl = slice(gi * POOL_GROUP, (gi + 1) * POOL_GROUP)
        out = out.at[:, sl, sl].set(pool_w[:, gi])
    return out


def kernel(x, c, ctx, c_ctx, w_mod, b_mod, norm_g, ffn_w_in, ffn_w_out, w_in, conv_w, pool_w,
           pool_scale, rpb, w_out, final_g):
    n_batch, seq, d = x.shape
    ctx_len = ctx.shape[1]
    depth = w_mod.shape[0]
    rows = seq // GRID_W
    assert d == D_MODEL and ctx_len == TOK and seq % FFN_TM == 0 and rows >= KROWS
    assert (n_batch * ctx_len) % FFN_TM == 0 and rows // QROWS >= NKB + 1
    assert NA_KH_MAX // 2 == QROWS and rpb.shape[2:] == (2 * NA_KH_MAX - 1, 2 * NA_KW - 1)
    assert n_batch + 1 <= MOD_ROWS and w_in.shape[-1] == D_IN and ffn_w_in.shape[-1] == 2 * D_FF
    n_lat = n_batch * seq
    n_lat_tiles = n_lat // TOK
    tiles_per_seq = seq // TOK
    lat_ffn_tiles = n_lat // FFN_TM
    ctx_ffn_tiles = n_batch * ctx_len // FFN_TM

    cc = jnp.zeros((MOD_ROWS, d), F32).at[:n_batch].set(c).at[n_batch].set(c_ctx)
    mod = _modulation(cc, w_mod, b_mod).reshape(depth, MOD_ROWS, N_MOD, d)

    w1 = ffn_w_in.astype(BF16)
    w2 = ffn_w_out.astype(BF16)
    w_in_b = w_in.astype(BF16)
    w_out_b = w_out.astype(BF16)
    pool_bd = _pool_block_diag(pool_w).astype(BF16)
    pool_sc = pool_scale.reshape(depth, 1, POOL_CH)
    norm_g4 = norm_g.reshape(depth, 3, 1, d)
    rp = _pad_rpb(rpb)

    ffn_kw = dict(lat_tiles_per_batch=seq // FFN_TM, n_batch=n_batch)
    xa, xb, xb_off = x.reshape(n_lat, d), ctx.reshape(n_batch * ctx_len, d), 0
    for l in range(depth):
        last = l == depth - 1
        xs = _ffn(xa, xb, xb_off, lat_ffn_tiles, ctx_ffn_tiles, mod, norm_g4, w1, w2, final_g,
                  layer=l, which=0, sub=0, final=False, **ffn_kw)
        mix = _inproj(xs, mod, norm_g4, w_in_b, conv_w, pool_bd, pool_sc, layer=l,
                      n_lat_tiles=n_lat_tiles, lat_tiles_per_seq=tiles_per_seq, n_batch=n_batch)
        xs = _attention(xs, mix, rp, mod, w_out_b, layer=l, n_batch=n_batch, rows=rows)
        if not last:
            xs = _ctx_attention(xs, mix, mod, w_out_b, layer=l, n_batch=n_batch,
                                n_lat_tiles=n_lat_tiles)
        xs = _ffn(xs, xs, lat_ffn_tiles, lat_ffn_tiles, 0 if last else ctx_ffn_tiles, mod, norm_g4,
                  w1, w2, final_g, layer=l, which=1, sub=2, final=last, **ffn_kw)
        xa, xb, xb_off = xs, xs, lat_ffn_tiles
    return xs.reshape(n_batch, seq, d)
```

```python
import functools

import numpy as np
import jax
import jax.numpy as jnp
from jax import lax
from jax.experimental import pallas as pl
from jax.experimental.pallas import tpu as pltpu

F32 = jnp.float32
BF16 = jnp.bfloat16

D_MODEL = 1024
GRID_W = 64
D_FF = 2816
N_MOD = 9
RMS_EPS = 1e-6
NEG_INF = -1e30
CONV_CH = D_MODEL // 4
POOL_CH = D_MODEL // 4
POOL_WINDOWS = (2, 4, 8, 16)
POOL_GROUP = POOL_CH // 4
NA_HEAD_DIM = 64
NA_CH = D_MODEL // 2
NA_HEADS = NA_CH // NA_HEAD_DIM
NA_KH_MAX = 8
NA_KW = 16
OFF_Q = 3 * CONV_CH + POOL_CH
D_IN = OFF_Q + 3 * NA_CH

FFN_TM = 1024
FFN_CHUNKS = ((0, 1536), (1536, 2816))
FFN_SUBTILES = 2
TOK = 256
HALO = 8
INPROJ_SUBTILES = 4
QROWS = TOK // GRID_W
KROWS = QROWS + NA_KH_MAX
NKB = KROWS * GRID_W // TOK
SCORE_LOOKAHEAD = 1
MOD_ROWS = 16
MOD_TN = 2304
VMEM_LIMIT = 56 * 1024 * 1024


def _rms_mod(x, g, scale, shift):
    ms = jnp.mean(x * x, axis=-1, keepdims=True)
    return (x * lax.rsqrt(ms + RMS_EPS) * g) * (1.0 + scale) + shift


def _mod_kernel(c_ref, w_ref, b_ref, o_ref):
    c = c_ref[...]
    s = (c * jax.nn.sigmoid(c)).astype(BF16)
    o_ref[...] = jnp.dot(s, w_ref[...].astype(BF16), preferred_element_type=F32) + b_ref[...]


def _modulation(cc, w_mod, b_mod):
    depth = w_mod.shape[0]
    n = N_MOD * D_MODEL
    return pl.pallas_call(
        _mod_kernel,
        grid=(depth, n // MOD_TN),
        in_specs=[
            pl.BlockSpec((MOD_ROWS, D_MODEL), lambda l, j: (0, 0)),
            pl.BlockSpec((None, D_MODEL, MOD_TN), lambda l, j: (l, 0, j)),
            pl.BlockSpec((None, 1, MOD_TN), lambda l, j: (l, 0, j)),
        ],
        out_specs=pl.BlockSpec((None, MOD_ROWS, MOD_TN), lambda l, j: (l, 0, j)),
        out_shape=jax.ShapeDtypeStruct((depth, MOD_ROWS, n), F32),
        compiler_params=pltpu.CompilerParams(
            dimension_semantics=("arbitrary", "arbitrary"), vmem_limit_bytes=VMEM_LIMIT),
        name="modulation",
    )(cc, w_mod, b_mod.reshape(depth, 1, n))


def _ffn_kernel(xa_ref, xb_ref, mod_ref, g_ref, w1_ref, w2_ref, fg_ref, o_ref, *, sub, final,
                n_a_tiles, n_b_tiles):
    m = mod_ref[...]
    shift = m[3 * sub:3 * sub + 1]
    scale = m[3 * sub + 1:3 * sub + 2]
    gate = m[3 * sub + 2:3 * sub + 3]
    rows = FFN_TM // FFN_SUBTILES

    def load_x(r):
        sl = slice(r * rows, (r + 1) * rows)
        if n_b_tiles:
            return jnp.where(pl.program_id(0) < n_a_tiles, xa_ref[sl], xb_ref[sl])
        return xa_ref[sl]

    units = [(r, c) for c in range(len(FFN_CHUNKS)) for r in range(FFN_SUBTILES)]
    xs, hbs, accs = {}, {}, {}

    def up(r, c):
        if r not in xs:
            xs[r] = load_x(r)
            hbs[r] = _rms_mod(xs[r], g_ref[...], scale, shift).astype(BF16)
        lo, hi = FFN_CHUNKS[c]
        a = jnp.dot(hbs[r], w1_ref[:, lo:hi], preferred_element_type=F32)
        b = jnp.dot(hbs[r], w1_ref[:, D_FF + lo:D_FF + hi], preferred_element_type=F32)
        return a, b

    ab = up(*units[0])
    for i, (r, c) in enumerate(units):
        a, b = ab
        if i + 1 < len(units):
            ab = up(*units[i + 1])
        lo, hi = FFN_CHUNKS[c]
        gg = (a * jax.nn.sigmoid(a) * b).astype(BF16)
        part = jnp.dot(gg, w2_ref[lo:hi, :], preferred_element_type=F32)
        accs[r] = part if r not in accs else accs[r] + part
    for r in range(FFN_SUBTILES):
        y = xs[r] + (0.5 * gate) * accs[r]
        if final:
            ms = jnp.mean(y * y, axis=-1, keepdims=True)
            y = y * lax.rsqrt(ms + RMS_EPS) * fg_ref[...]
        o_ref[r * rows:(r + 1) * rows] = y


def _ffn(xa, xb, xb_off, n_a_tiles, n_b_tiles, mod, norm_g4, w1, w2, final_g, *, layer, which, sub,
         final, lat_tiles_per_batch, n_batch):
    n_tiles = n_a_tiles + n_b_tiles
    const = lambda i: (0, 0)
    kern = functools.partial(_ffn_kernel, sub=sub, final=final, n_a_tiles=n_a_tiles,
                             n_b_tiles=n_b_tiles)
    return pl.pallas_call(
        kern,
        grid=(n_tiles,),
        in_specs=[
            pl.BlockSpec((FFN_TM, D_MODEL), lambda i: (jnp.minimum(i, n_a_tiles - 1), 0)),
            pl.BlockSpec((FFN_TM, D_MODEL), lambda i: (jnp.maximum(i - n_a_tiles, 0) + xb_off, 0)),
            pl.BlockSpec((None, None, N_MOD, D_MODEL),
                         lambda i: (layer, jnp.minimum(i // lat_tiles_per_batch, n_batch), 0, 0)),
            pl.BlockSpec((None, None, 1, D_MODEL), lambda i: (layer, sub, 0, 0)),
            pl.BlockSpec((None, None, D_MODEL, 2 * D_FF), lambda i: (layer, which, 0, 0),
                         pipeline_mode=pl.Buffered(1)),
            pl.BlockSpec((None, None, D_FF, D_MODEL), lambda i: (layer, which, 0, 0),
                         pipeline_mode=pl.Buffered(1)),
            pl.BlockSpec((1, D_MODEL), const),
        ],
        out_specs=pl.BlockSpec((FFN_TM, D_MODEL), lambda i: (i, 0)),
        out_shape=jax.ShapeDtypeStruct((n_tiles * FFN_TM, D_MODEL), F32),
        compiler_params=pltpu.CompilerParams(
            dimension_semantics=("arbitrary",), vmem_limit_bytes=VMEM_LIMIT),
        name="ffn",
    )(xa, xb, mod, norm_g4, w1, w2, final_g.reshape(1, D_MODEL))


def _inproj_kernel(xp_ref, x_ref, xn_ref, mod_ref, g_ref, w_ref, cw_ref, pw_ref, ps_ref, o_ref,
                   *, n_lat_tiles, lat_tiles_per_seq):
    m = mod_ref[...]
    shift, scale = m[3:4], m[4:5]
    g = g_ref[...]
    ext = TOK + 2 * HALO

    def project(sub):
        lo = sub * TOK
        xp = xp_ref[...] if sub == 0 else x_ref[lo - HALO:lo]
        xn = xn_ref[...] if sub == INPROJ_SUBTILES - 1 else x_ref[lo + TOK:lo + TOK + HALO]
        xe = jnp.concatenate([xp, x_ref[lo:lo + TOK], xn], axis=0)
        he = _rms_mod(xe, g, scale, shift).astype(BF16)
        u = jnp.dot(he, w_ref[:, :OFF_Q], preferred_element_type=F32)
        qkv = jnp.dot(he[HALO:HALO + TOK], w_ref[:, OFF_Q:], preferred_element_type=F32)
        o_ref[lo:lo + TOK, 2 * CONV_CH:2 * CONV_CH + NA_CH] = (
            qkv[:, :NA_CH] * (NA_HEAD_DIM ** -0.5)).astype(BF16)
        o_ref[lo:lo + TOK, 2 * CONV_CH + NA_CH:] = qkv[:, NA_CH:].astype(BF16)
        return u

    def mix(sub, u):
        t = pl.program_id(0) * INPROJ_SUBTILES + sub
        is_lat = t < n_lat_tiles
        pos = jnp.where(is_lat, t % lat_tiles_per_seq, 0)
        ntile = jnp.where(is_lat, lat_tiles_per_seq, 1)
        t0 = pos * TOK
        seq_len = ntile * TOK

        row = lax.broadcasted_iota(jnp.int32, (ext, 1), 0)
        keep_lo = jnp.where(pos == 0, HALO, 0)
        keep_hi = jnp.where(pos == ntile - 1, HALO + TOK, ext)
        u = jnp.where((row >= keep_lo) & (row < keep_hi), u, 0.0)

        def shifted(a, s):
            return pltpu.roll(a, s % ext, 0)

        def centre(a):
            return a[HALO:HALO + TOK]

        z = u[:, 2 * CONV_CH:3 * CONV_CH] * u[:, :CONV_CH]
        cw = cw_ref[...]
        conv = (centre(shifted(z, 1)) * cw[0:1] + centre(z) * cw[1:2]
                + centre(shifted(z, -1)) * cw[2:3])
        y_conv = centre(u[:, CONV_CH:2 * CONV_CH]) * conv

        v = u[:, 3 * CONV_CH:]
        a1 = v + shifted(v, 1)
        a2 = a1 + shifted(a1, 2)
        a3 = a2 + shifted(a2, 4)
        a4 = a3 + shifted(a3, 8)
        sums = (centre(a1), centre(shifted(a2, -1)), centre(shifted(a3, -3)),
                centre(shifted(a4, -7)))
        tpos = t0 + lax.broadcasted_iota(jnp.int32, (TOK, 1), 0)
        lane = lax.broadcasted_iota(jnp.int32, (TOK, POOL_CH), 1)
        mean = None
        for gi in reversed(range(len(POOL_WINDOWS))):
            w = POOL_WINDOWS[gi]
            left = w // 2
            right = w - 1 - left
            cnt = jnp.minimum(tpos + right + 1, seq_len) - jnp.maximum(tpos - left, 0)
            mg = sums[gi] / cnt.astype(F32)
            mean = mg if mean is None else jnp.where(lane < (gi + 1) * POOL_GROUP, mg, mean)
        dpool = (mean - centre(v)).astype(BF16)
        y_pool = jnp.dot(dpool, pw_ref[...], preferred_element_type=F32) * ps_ref[...]
        lo = sub * TOK
        o_ref[lo:lo + TOK, :CONV_CH] = y_conv.astype(BF16)
        o_ref[lo:lo + TOK, CONV_CH:CONV_CH + POOL_CH] = y_pool.astype(BF16)

    us = [project(sub) for sub in range(INPROJ_SUBTILES)]
    for sub, u in enumerate(us):
        mix(sub, u)


def _inproj(xs, mod, norm_g4, w_in, conv_w, pool_bd, pool_scale, *, layer, n_lat_tiles,
            lat_tiles_per_seq, n_batch):
    nt = xs.shape[0]
    tm = INPROJ_SUBTILES * TOK
    n_steps = nt // tm
    hb = tm // HALO
    steps_per_seq = lat_tiles_per_seq // INPROJ_SUBTILES
    per_layer = lambda i: (layer, 0, 0)
    kern = functools.partial(_inproj_kernel, n_lat_tiles=n_lat_tiles,
                             lat_tiles_per_seq=lat_tiles_per_seq)
    return pl.pallas_call(
        kern,
        grid=(n_steps,),
        in_specs=[
            pl.BlockSpec((HALO, D_MODEL), lambda i: (jnp.maximum(i * hb - 1, 0), 0)),
            pl.BlockSpec((tm, D_MODEL), lambda i: (i, 0)),
            pl.BlockSpec((HALO, D_MODEL), lambda i: (jnp.minimum((i + 1) * hb, nt // HALO - 1), 0)),
            pl.BlockSpec((None, None, N_MOD, D_MODEL),
                         lambda i: (layer, jnp.minimum(i // steps_per_seq, n_batch), 0, 0)),
            pl.BlockSpec((None, None, 1, D_MODEL), lambda i: (layer, 1, 0, 0)),
            pl.BlockSpec((None, D_MODEL, D_IN), per_layer, pipeline_mode=pl.Buffered(1)),
            pl.BlockSpec((None, 3, CONV_CH), per_layer),
            pl.BlockSpec((None, POOL_CH, POOL_CH), per_layer),
            pl.BlockSpec((None, 1, POOL_CH), per_layer),
        ],
        out_specs=pl.BlockSpec((tm, 2 * D_MODEL), lambda i: (i, 0)),
        out_shape=jax.ShapeDtypeStruct((nt, 2 * D_MODEL), BF16),
        compiler_params=pltpu.CompilerParams(
            dimension_semantics=("arbitrary",), vmem_limit_bytes=VMEM_LIMIT),
        name="inproj",
    )(xs, xs, xs, mod, norm_g4, w_in, conv_w, pool_bd, pool_scale)


def _pair_lanes(p, offset=0):
    return slice(offset + 2 * p * NA_HEAD_DIM, offset + 2 * (p + 1) * NA_HEAD_DIM)


def _attend_heads(q_of, k_of, v_of, add_bias):
    lane = lax.broadcasted_iota(jnp.int32, (1, 2 * NA_HEAD_DIM), 1)
    own = [jnp.where(lane < NA_HEAD_DIM, 1.0, 0.0).astype(BF16),
           jnp.where(lane < NA_HEAD_DIM, 0.0, 1.0).astype(BF16)]

    def scores(t):
        p, half = divmod(t, 2)
        s = lax.dot_general(q_of(p) * own[half], k_of(p), (((1,), (1,)), ((), ())),
                            preferred_element_type=F32)
        return add_bias(t, s)

    outs = []
    pending = [scores(t) for t in range(SCORE_LOOKAHEAD)]
    for t in range(NA_HEADS):
        s = pending.pop(0)
        if t + SCORE_LOOKAHEAD < NA_HEADS:
            pending.append(scores(t + SCORE_LOOKAHEAD))
        vm = v_of(t // 2) * own[t % 2] + own[1 - t % 2]
        mx = jnp.max(s, axis=-1, keepdims=True)
        p = jnp.exp(s - mx).astype(BF16)
        outs.append(jnp.dot(p, vm, preferred_element_type=F32))
    o_pairs = []
    for p in range(NA_HEADS // 2):
        o0, o1 = outs[2 * p], outs[2 * p + 1]
        lo = lax.broadcasted_iota(jnp.int32, o0.shape, 1) < NA_HEAD_DIM
        num = jnp.where(lo, o0, o1)
        den = jnp.where(lo, pltpu.roll(o0, NA_HEAD_DIM, 1), pltpu.roll(o1, NA_HEAD_DIM, 1))
        o_pairs.append(num / den)
    return o_pairs


def _out_residual(x, ycp, o_pairs, wo_ref, gate):
    y = jnp.concatenate([ycp] + [o.astype(BF16) for o in o_pairs], axis=-1)
    return x + gate * jnp.dot(y, wo_ref[...], preferred_element_type=F32)


def _fill_bias(bias_buf, rp_ref, rb, nrb):
    width = 2 * GRID_W
    lane = lax.broadcasted_iota(jnp.int32, (GRID_W, width), 1)
    qc = lax.broadcasted_iota(jnp.int32, (GRID_W, width), 0)
    kc = lane & (GRID_W - 1)
    cstart = jnp.clip(qc - NA_KW // 2, 0, GRID_W - NA_KW)
    col_ok = (kc >= cstart) & (kc < cstart + NA_KW)
    lo_half = lane < GRID_W
    is_first = rb == 0
    is_last = rb == nrb - 1
    base = jnp.where(is_first, 0, jnp.where(is_last, -2 * QROWS, -QROWS))
    shift0 = (width - (NA_KW - 1)) % width
    shift1 = (shift0 + GRID_W) % width

    def per_head(h, carry):
        for i in range(QROWS):
            jlo = jnp.where(is_first, 0, jnp.where(is_last, QROWS, i))
            for jj in range(KROWS // 2):
                halves = []
                for j, sh in ((2 * jj, shift0), (2 * jj + 1, shift1)):
                    ok = (j >= jlo) & (j < jlo + NA_KH_MAX)
                    idx = jnp.clip(j - i + base + NA_KH_MAX - 1, 0, 2 * NA_KH_MAX - 2)
                    row = jnp.where(ok, rp_ref[h, pl.ds(idx, 1), :], NEG_INF)
                    halves.append(pltpu.roll(jnp.broadcast_to(row, (GRID_W, width)), sh, 1,
                                             stride=1, stride_axis=0))
                tile = jnp.where(col_ok, jnp.where(lo_half, halves[0], halves[1]), NEG_INF)
                bias_buf[h, i * GRID_W:(i + 1) * GRID_W, jj * width:(jj + 1) * width] = tile
        return carry

    lax.fori_loop(0, NA_HEADS, per_head, 0)


def _attn_kernel(x_ref, yq_ref, kv0_ref, kv1_ref, kv2_ref, kvc_ref, rp_ref, mod_ref, wo_ref, o_ref,
                 kvbuf, bias_buf, *, nrb):
    rb = pl.program_id(0)
    n_loc = NKB * TOK

    @pl.when((pl.program_id(1) == 0) & ((rb <= 1) | (rb == nrb - 1)))
    def _():
        _fill_bias(bias_buf, rp_ref, rb, nrb)

    for j, kv_ref in enumerate((kv0_ref, kv1_ref, kv2_ref, kvc_ref)):
        kvbuf[j * TOK:(j + 1) * TOK] = kv_ref[...]

    def add_bias(h, s):
        return jnp.concatenate([s[:, :n_loc] + bias_buf[h], s[:, n_loc:]], axis=-1)

    o_pairs = _attend_heads(lambda p: yq_ref[:, _pair_lanes(p, NA_CH)],
                            lambda p: kvbuf[:, _pair_lanes(p)],
                            lambda p: kvbuf[:, _pair_lanes(p, NA_CH)], add_bias)
    gate = mod_ref[5:6]
    o_ref[...] = _out_residual(x_ref[...], yq_ref[:, :NA_CH], o_pairs, wo_ref, gate)


def _attention(xs, mix, rp, mod, w_out, *, layer, n_batch, rows):
    nrb = rows // QROWS
    tiles_per_seq = nrb
    n_lat_tiles = n_batch * tiles_per_seq
    tok_blk = lambda rb, b: b * tiles_per_seq + rb
    ksb = lambda rb: jnp.clip(rb - 1, 0, nrb - NKB)

    def kv_spec(j):
        return pl.BlockSpec((TOK, 2 * NA_CH), lambda rb, b: (b * tiles_per_seq + ksb(rb) + j, 1))

    in_specs = [
        pl.BlockSpec((TOK, D_MODEL), lambda rb, b: (tok_blk(rb, b), 0)),
        pl.BlockSpec((TOK, 2 * NA_CH), lambda rb, b: (tok_blk(rb, b), 0)),
        kv_spec(0), kv_spec(1), kv_spec(2),
        pl.BlockSpec((TOK, 2 * NA_CH), lambda rb, b: (n_lat_tiles + b, 1)),
        pl.BlockSpec((None,) + rp.shape[1:], lambda rb, b: (layer, 0, 0, 0)),
        pl.BlockSpec((None, None, N_MOD, D_MODEL), lambda rb, b: (layer, b, 0, 0)),
        pl.BlockSpec((None, D_MODEL, D_MODEL), lambda rb, b: (layer, 0, 0)),
    ]
    return pl.pallas_call(
        functools.partial(_attn_kernel, nrb=nrb),
        grid=(nrb, n_batch),
        in_specs=in_specs,
        out_specs=pl.BlockSpec((TOK, D_MODEL), lambda rb, b: (tok_blk(rb, b), 0)),
        out_shape=jax.ShapeDtypeStruct(xs.shape, F32),
        scratch_shapes=[
            pltpu.VMEM((NKB * TOK + TOK, 2 * NA_CH), BF16),
            pltpu.VMEM((NA_HEADS, TOK, NKB * TOK), F32),
        ],
        input_output_aliases={0: 0},
        compiler_params=pltpu.CompilerParams(
            dimension_semantics=("arbitrary", "arbitrary"), vmem_limit_bytes=VMEM_LIMIT),
        name="attention",
    )(xs, mix, mix, mix, mix, mix, rp, mod, w_out)


def _ctx_kernel(x_ref, yq_ref, kv_ref, mod_ref, wo_ref, o_ref):
    o_pairs = _attend_heads(lambda p: yq_ref[:, _pair_lanes(p, NA_CH)],
                            lambda p: kv_ref[:, _pair_lanes(p)],
                            lambda p: kv_ref[:, _pair_lanes(p, NA_CH)], lambda h, s: s)
    gate = mod_ref[5:6]
    o_ref[...] = _out_residual(x_ref[...], yq_ref[:, :NA_CH], o_pairs, wo_ref, gate)


def _ctx_attention(xs, mix, mod, w_out, *, layer, n_batch, n_lat_tiles):
    blk = lambda b: n_lat_tiles + b
    return pl.pallas_call(
        _ctx_kernel,
        grid=(n_batch,),
        in_specs=[
            pl.BlockSpec((TOK, D_MODEL), lambda b: (blk(b), 0)),
            pl.BlockSpec((TOK, 2 * NA_CH), lambda b: (blk(b), 0)),
            pl.BlockSpec((TOK, 2 * NA_CH), lambda b: (blk(b), 1)),
            pl.BlockSpec((None, None, N_MOD, D_MODEL), lambda b: (layer, n_batch, 0, 0)),
            pl.BlockSpec((None, D_MODEL, D_MODEL), lambda b: (layer, 0, 0)),
        ],
        out_specs=pl.BlockSpec((TOK, D_MODEL), lambda b: (blk(b), 0)),
        out_shape=jax.ShapeDtypeStruct(xs.shape, F32),
        input_output_aliases={0: 0},
        compiler_params=pltpu.CompilerParams(
            dimension_semantics=("arbitrary",), vmem_limit_bytes=VMEM_LIMIT),
        name="ctx_attention",
    )(xs, mix, mix, mod, w_out)


def _pad_rpb(rpb):
    depth, heads, n_off, n_col = rpb.shape
    out = jnp.full((depth, heads, n_off + 1, 2 * GRID_W), NEG_INF, F32)
    return out.at[:, :, :n_off, :n_col].set(rpb)


def _pool_block_diag(pool_w):
    depth, n = pool_w.shape[:2]
    out = jnp.zeros((depth, POOL_CH, POOL_CH), pool_w.dtype)
    for gi in range(n):
        sl = slice(gi * POOL_GROUP, (gi + 1) * POOL_GROUP)
        out = out.at[:, sl, sl].set(pool_w[:, gi])
    return out


def kernel(x, c, ctx, c_ctx, w_mod, b_mod, norm_g, ffn_w_in, ffn_w_out, w_in, conv_w, pool_w,
           pool_scale, rpb, w_out, final_g):
    n_batch, seq, d = x.shape
    ctx_len = ctx.shape[1]
    depth = w_mod.shape[0]
    rows = seq // GRID_W
    assert d == D_MODEL and ctx_len == TOK and seq % FFN_TM == 0 and rows >= KROWS
    assert (n_batch * ctx_len) % FFN_TM == 0 and rows // QROWS >= NKB + 1
    assert seq % (INPROJ_SUBTILES * TOK) == 0 and (n_batch * ctx_len) % (INPROJ_SUBTILES * TOK) == 0
    assert NA_KH_MAX // 2 == QROWS and rpb.shape[2:] == (2 * NA_KH_MAX - 1, 2 * NA_KW - 1)
    assert n_batch + 1 <= MOD_ROWS and w_in.shape[-1] == D_IN and ffn_w_in.shape[-1] == 2 * D_FF
    n_lat = n_batch * seq
    n_lat_tiles = n_lat // TOK
    tiles_per_seq = seq // TOK
    lat_ffn_tiles = n_lat // FFN_TM
    ctx_ffn_tiles = n_batch * ctx_len // FFN_TM

    cc = jnp.zeros((MOD_ROWS, d), F32).at[:n_batch].set(c).at[n_batch].set(c_ctx)
    mod = _modulation(cc, w_mod, b_mod).reshape(depth, MOD_ROWS, N_MOD, d)

    w1 = ffn_w_in.astype(BF16)
    w2 = ffn_w_out.astype(BF16)
    w_in_b = w_in.astype(BF16)
    w_out_b = w_out.astype(BF16)
    pool_bd = _pool_block_diag(pool_w).astype(BF16)
    pool_sc = pool_scale.reshape(depth, 1, POOL_CH)
    norm_g4 = norm_g.reshape(depth, 3, 1, d)
    rp = _pad_rpb(rpb)

    ffn_kw = dict(lat_tiles_per_batch=seq // FFN_TM, n_batch=n_batch)
    xa, xb, xb_off = x.reshape(n_lat, d), ctx.reshape(n_batch * ctx_len, d), 0
    for l in range(depth):
        last = l == depth - 1
        xs = _ffn(xa, xb, xb_off, lat_ffn_tiles, ctx_ffn_tiles, mod, norm_g4, w1, w2, final_g,
                  layer=l, which=0, sub=0, final=False, **ffn_kw)
        mix = _inproj(xs, mod, norm_g4, w_in_b, conv_w, pool_bd, pool_sc, layer=l,
                      n_lat_tiles=n_lat_tiles, lat_tiles_per_seq=tiles_per_seq, n_batch=n_batch)
        xs = _attention(xs, mix, rp, mod, w_out_b, layer=l, n_batch=n_batch, rows=rows)
        if not last:
            xs = _ctx_attention(xs, mix, mod, w_out_b, layer=l, n_batch=n_batch,
                                n_lat_tiles=n_lat_tiles)
        xs = _ffn(xs, xs, lat_ffn_tiles, lat_ffn_tiles, 0 if last else ctx_ffn_tiles, mod, norm_g4,
                  w1, w2, final_g, layer=l, which=1, sub=2, final=last, **ffn_kw)
        xa, xb, xb_off = xs, xs, lat_ffn_tiles
    return xs.reshape(n_batch, seq, d)
```

```python
import functools

import numpy as np
import jax
import jax.numpy as jnp
from jax import lax
from jax.experimental import pallas as pl
from jax.experimental.pallas import tpu as pltpu

F32 = jnp.float32
BF16 = jnp.bfloat16

D_MODEL = 1024
GRID_W = 64
D_FF = 2816
N_MOD = 9
RMS_EPS = 1e-6
NEG_INF = -1e30
CONV_CH = D_MODEL // 4
POOL_CH = D_MODEL // 4
POOL_WINDOWS = (2, 4, 8, 16)
POOL_GROUP = POOL_CH // 4
NA_HEAD_DIM = 64
NA_CH = D_MODEL // 2
NA_HEADS = NA_CH // NA_HEAD_DIM
NA_KH_MAX = 8
NA_KW = 16
OFF_Q = 3 * CONV_CH + POOL_CH
D_IN = OFF_Q + 3 * NA_CH

FFN_TM = 1024
FFN_CHUNKS = ((0, 1536), (1536, 2816))
FFN_SUBTILES = 2
TOK = 256
HALO = 8
INPROJ_SUBTILES = 4
QROWS = TOK // GRID_W
KROWS = QROWS + NA_KH_MAX
NKB = KROWS * GRID_W // TOK
SCORE_LOOKAHEAD = 1
MOD_ROWS = 16
MOD_TN = 2304
VMEM_LIMIT = 56 * 1024 * 1024


def _rms_mod(x, g, scale, shift):
    ms = jnp.mean(x * x, axis=-1, keepdims=True)
    return (x * lax.rsqrt(ms + RMS_EPS) * g) * (1.0 + scale) + shift


def _mod_kernel(c_ref, w_ref, b_ref, o_ref):
    c = c_ref[...]
    s = (c * jax.nn.sigmoid(c)).astype(BF16)
    o_ref[...] = jnp.dot(s, w_ref[...].astype(BF16), preferred_element_type=F32) + b_ref[...]


def _modulation(cc, w_mod, b_mod):
    depth = w_mod.shape[0]
    n = N_MOD * D_MODEL
    return pl.pallas_call(
        _mod_kernel,
        grid=(depth, n // MOD_TN),
        in_specs=[
            pl.BlockSpec((MOD_ROWS, D_MODEL), lambda l, j: (0, 0)),
            pl.BlockSpec((None, D_MODEL, MOD_TN), lambda l, j: (l, 0, j)),
            pl.BlockSpec((None, 1, MOD_TN), lambda l, j: (l, 0, j)),
        ],
        out_specs=pl.BlockSpec((None, MOD_ROWS, MOD_TN), lambda l, j: (l, 0, j)),
        out_shape=jax.ShapeDtypeStruct((depth, MOD_ROWS, n), F32),
        compiler_params=pltpu.CompilerParams(
            dimension_semantics=("arbitrary", "arbitrary"), vmem_limit_bytes=VMEM_LIMIT),
        name="modulation",
    )(cc, w_mod, b_mod.reshape(depth, 1, n))


def _ffn_kernel(xa_ref, xb_ref, mod_ref, g_ref, w1_ref, w2_ref, fg_ref, o_ref, *, sub, final,
                n_a_tiles, n_b_tiles):
    m = mod_ref[...]
    shift = m[3 * sub:3 * sub + 1]
    scale = m[3 * sub + 1:3 * sub + 2]
    gate = m[3 * sub + 2:3 * sub + 3]
    rows = FFN_TM // FFN_SUBTILES

    def load_x(r):
        sl = slice(r * rows, (r + 1) * rows)
        if n_b_tiles:
            return jnp.where(pl.program_id(0) < n_a_tiles, xa_ref[sl], xb_ref[sl])
        return xa_ref[sl]

    units = [(r, c) for c in range(len(FFN_CHUNKS)) for r in range(FFN_SUBTILES)]
    xs, hbs, accs = {}, {}, {}

    def up(r, c):
        if r not in xs:
            xs[r] = load_x(r)
            hbs[r] = _rms_mod(xs[r], g_ref[...], scale, shift).astype(BF16)
        lo, hi = FFN_CHUNKS[c]
        a = jnp.dot(hbs[r], w1_ref[:, lo:hi], preferred_element_type=F32)
        b = jnp.dot(hbs[r], w1_ref[:, D_FF + lo:D_FF + hi], preferred_element_type=F32)
        return a, b

    ab = up(*units[0])
    for i, (r, c) in enumerate(units):
        a, b = ab
        if i + 1 < len(units):
            ab = up(*units[i + 1])
        lo, hi = FFN_CHUNKS[c]
        gg = (a * jax.nn.sigmoid(a) * b).astype(BF16)
        part = jnp.dot(gg, w2_ref[lo:hi, :], preferred_element_type=F32)
        accs[r] = part if r not in accs else accs[r] + part
    for r in range(FFN_SUBTILES):
        y = xs[r] + (0.5 * gate) * accs[r]
        if final:
            ms = jnp.mean(y * y, axis=-1, keepdims=True)
            y = y * lax.rsqrt(ms + RMS_EPS) * fg_ref[...]
        o_ref[r * rows:(r + 1) * rows] = y


def _ffn(xa, xb, xb_off, n_a_tiles, n_b_tiles, mod, norm_g4, w1, w2, final_g, *, layer, which, sub,
         final, lat_tiles_per_batch, n_batch):
    n_tiles = n_a_tiles + n_b_tiles
    const = lambda i: (0, 0)
    kern = functools.partial(_ffn_kernel, sub=sub, final=final, n_a_tiles=n_a_tiles,
                             n_b_tiles=n_b_tiles)
    return pl.pallas_call(
        kern,
        grid=(n_tiles,),
        in_specs=[
            pl.BlockSpec((FFN_TM, D_MODEL), lambda i: (jnp.minimum(i, n_a_tiles - 1), 0)),
            pl.BlockSpec((FFN_TM, D_MODEL), lambda i: (jnp.maximum(i - n_a_tiles, 0) + xb_off, 0)),
            pl.BlockSpec((None, None, N_MOD, D_MODEL),
                         lambda i: (layer, jnp.minimum(i // lat_tiles_per_batch, n_batch), 0, 0)),
            pl.BlockSpec((None, None, 1, D_MODEL), lambda i: (layer, sub, 0, 0)),
            pl.BlockSpec((None, None, D_MODEL, 2 * D_FF), lambda i: (layer, which, 0, 0),
                         pipeline_mode=pl.Buffered(1)),
            pl.BlockSpec((None, None, D_FF, D_MODEL), lambda i: (layer, which, 0, 0),
                         pipeline_mode=pl.Buffered(1)),
            pl.BlockSpec((1, D_MODEL), const),
        ],
        out_specs=pl.BlockSpec((FFN_TM, D_MODEL), lambda i: (i, 0)),
        out_shape=jax.ShapeDtypeStruct((n_tiles * FFN_TM, D_MODEL), F32),
        compiler_params=pltpu.CompilerParams(
            dimension_semantics=("arbitrary",), vmem_limit_bytes=VMEM_LIMIT),
        name="ffn",
    )(xa, xb, mod, norm_g4, w1, w2, final_g.reshape(1, D_MODEL))


def _inproj_kernel(xp_ref, x_ref, xn_ref, mod_ref, g_ref, w_ref, cw_ref, pw_ref, ps_ref, o_ref,
                   *, n_lat_tiles, lat_tiles_per_seq):
    m = mod_ref[...]
    shift, scale = m[3:4], m[4:5]
    g = g_ref[...]
    ext = TOK + 2 * HALO

    def project(sub):
        lo = sub * TOK
        xp = xp_ref[...] if sub == 0 else x_ref[lo - HALO:lo]
        xn = xn_ref[...] if sub == INPROJ_SUBTILES - 1 else x_ref[lo + TOK:lo + TOK + HALO]
        xe = jnp.concatenate([xp, x_ref[lo:lo + TOK], xn], axis=0)
        he = _rms_mod(xe, g, scale, shift).astype(BF16)
        u = jnp.dot(he, w_ref[:, :OFF_Q], preferred_element_type=F32)
        qkv = jnp.dot(he[HALO:HALO + TOK], w_ref[:, OFF_Q:], preferred_element_type=F32)
        o_ref[lo:lo + TOK, 2 * CONV_CH:2 * CONV_CH + NA_CH] = (
            qkv[:, :NA_CH] * (NA_HEAD_DIM ** -0.5)).astype(BF16)
        o_ref[lo:lo + TOK, 2 * CONV_CH + NA_CH:] = qkv[:, NA_CH:].astype(BF16)
        return u

    def mix(sub, u):
        t = pl.program_id(0) * INPROJ_SUBTILES + sub
        is_lat = t < n_lat_tiles
        pos = jnp.where(is_lat, t % lat_tiles_per_seq, 0)
        ntile = jnp.where(is_lat, lat_tiles_per_seq, 1)
        t0 = pos * TOK
        seq_len = ntile * TOK

        row = lax.broadcasted_iota(jnp.int32, (ext, 1), 0)
        keep_lo = jnp.where(pos == 0, HALO, 0)
        keep_hi = jnp.where(pos == ntile - 1, HALO + TOK, ext)
        u = jnp.where((row >= keep_lo) & (row < keep_hi), u, 0.0)

        def shifted(a, s):
            return pltpu.roll(a, s % ext, 0)

        def centre(a):
            return a[HALO:HALO + TOK]

        z = u[:, 2 * CONV_CH:3 * CONV_CH] * u[:, :CONV_CH]
        cw = cw_ref[...]
        conv = (centre(shifted(z, 1)) * cw[0:1] + centre(z) * cw[1:2]
                + centre(shifted(z, -1)) * cw[2:3])
        y_conv = centre(u[:, CONV_CH:2 * CONV_CH]) * conv

        v = u[:, 3 * CONV_CH:]
        a1 = v + shifted(v, 1)
        a2 = a1 + shifted(a1, 2)
        a3 = a2 + shifted(a2, 4)
        a4 = a3 + shifted(a3, 8)
        sums = (centre(a1), centre(shifted(a2, -1)), centre(shifted(a3, -3)),
                centre(shifted(a4, -7)))
        tpos = t0 + lax.broadcasted_iota(jnp.int32, (TOK, 1), 0)
        lane = lax.broadcasted_iota(jnp.int32, (TOK, POOL_CH), 1)
        mean = None
        for gi in reversed(range(len(POOL_WINDOWS))):
            w = POOL_WINDOWS[gi]
            left = w // 2
            right = w - 1 - left
            cnt = jnp.minimum(tpos + right + 1, seq_len) - jnp.maximum(tpos - left, 0)
            mg = sums[gi] / cnt.astype(F32)
            mean = mg if mean is None else jnp.where(lane < (gi + 1) * POOL_GROUP, mg, mean)
        dpool = (mean - centre(v)).astype(BF16)
        y_pool = jnp.dot(dpool, pw_ref[...], preferred_element_type=F32) * ps_ref[...]
        lo = sub * TOK
        o_ref[lo:lo + TOK, :CONV_CH] = y_conv.astype(BF16)
        o_ref[lo:lo + TOK, CONV_CH:CONV_CH + POOL_CH] = y_pool.astype(BF16)

    us = [project(sub) for sub in range(INPROJ_SUBTILES)]
    for sub, u in enumerate(us):
        mix(sub, u)


def _inproj(xs, mod, norm_g4, w_in, conv_w, pool_bd, pool_scale, *, layer, n_lat_tiles,
            lat_tiles_per_seq, n_batch):
    nt = xs.shape[0]
    tm = INPROJ_SUBTILES * TOK
    n_steps = nt // tm
    hb = tm // HALO
    steps_per_seq = lat_tiles_per_seq // INPROJ_SUBTILES
    per_layer = lambda i: (layer, 0, 0)
    kern = functools.partial(_inproj_kernel, n_lat_tiles=n_lat_tiles,
                             lat_tiles_per_seq=lat_tiles_per_seq)
    return pl.pallas_call(
        kern,
        grid=(n_steps,),
        in_specs=[
            pl.BlockSpec((HALO, D_MODEL), lambda i: (jnp.maximum(i * hb - 1, 0), 0)),
            pl.BlockSpec((tm, D_MODEL), lambda i: (i, 0)),
            pl.BlockSpec((HALO, D_MODEL), lambda i: (jnp.minimum((i + 1) * hb, nt // HALO - 1), 0)),
            pl.BlockSpec((None, None, N_MOD, D_MODEL),
                         lambda i: (layer, jnp.minimum(i // steps_per_seq, n_batch), 0, 0)),
            pl.BlockSpec((None, None, 1, D_MODEL), lambda i: (layer, 1, 0, 0)),
            pl.BlockSpec((None, D_MODEL, D_IN), per_layer, pipeline_mode=pl.Buffered(1)),
            pl.BlockSpec((None, 3, CONV_CH), per_layer),
            pl.BlockSpec((None, POOL_CH, POOL_CH), per_layer),
            pl.BlockSpec((None, 1, POOL_CH), per_layer),
        ],
        out_specs=pl.BlockSpec((tm, 2 * D_MODEL), lambda i: (i, 0)),
        out_shape=jax.ShapeDtypeStruct((nt, 2 * D_MODEL), BF16),
        compiler_params=pltpu.CompilerParams(
            dimension_semantics=("arbitrary",), vmem_limit_bytes=VMEM_LIMIT),
        name="inproj",
    )(xs, xs, xs, mod, norm_g4, w_in, conv_w, pool_bd, pool_scale)


def _pair_lanes(p, offset=0):
    return slice(offset + 2 * p * NA_HEAD_DIM, offset + 2 * (p + 1) * NA_HEAD_DIM)


def _softmax_probs(s):
    mx = jnp.max(s, axis=-1, keepdims=True)
    return jnp.exp(s - mx).astype(BF16)


def _mix_and_project(x, ycp, q_of, k_of, v_of, probs, wo_ref, gate):
    lane = lax.broadcasted_iota(jnp.int32, (1, 2 * NA_HEAD_DIM), 1)
    own = [jnp.where(lane < NA_HEAD_DIM, 1.0, 0.0).astype(BF16),
           jnp.where(lane < NA_HEAD_DIM, 0.0, 1.0).astype(BF16)]

    def scores(t):
        p, half = divmod(t, 2)
        return lax.dot_general(q_of(p) * own[half], k_of(p), (((1,), (1,)), ((), ())),
                               preferred_element_type=F32)

    def normalised(outs, first_pair, n_pairs):
        normed = []
        for p in range(first_pair, first_pair + n_pairs):
            o0, o1 = outs[2 * p], outs[2 * p + 1]
            lo = lax.broadcasted_iota(jnp.int32, o0.shape, 1) < NA_HEAD_DIM
            num = jnp.where(lo, o0, o1)
            den = jnp.where(lo, pltpu.roll(o0, NA_HEAD_DIM, 1), pltpu.roll(o1, NA_HEAD_DIM, 1))
            normed.append((num / den).astype(BF16))
        return normed

    outs = []
    pending = [scores(t) for t in range(SCORE_LOOKAHEAD)]
    for t in range(NA_HEADS):
        s = pending.pop(0)
        if t + SCORE_LOOKAHEAD < NA_HEADS:
            pending.append(scores(t + SCORE_LOOKAHEAD))
        vm = v_of(t // 2) * own[t % 2] + own[1 - t % 2]
        outs.append(jnp.dot(probs(t, s), vm, preferred_element_type=F32))
    y = jnp.concatenate([ycp] + normalised(outs, 0, NA_HEADS // 2), axis=-1)
    return x + gate * jnp.dot(y, wo_ref[...], preferred_element_type=F32)


def _fill_bias(bias_buf, rp_ref, rb, nrb):
    width = 2 * GRID_W
    lane = lax.broadcasted_iota(jnp.int32, (GRID_W, width), 1)
    qc = lax.broadcasted_iota(jnp.int32, (GRID_W, width), 0)
    kc = lane & (GRID_W - 1)
    cstart = jnp.clip(qc - NA_KW // 2, 0, GRID_W - NA_KW)
    col_ok = (kc >= cstart) & (kc < cstart + NA_KW)
    lo_half = lane < GRID_W
    is_first = rb == 0
    is_last = rb == nrb - 1
    base = jnp.where(is_first, 0, jnp.where(is_last, -2 * QROWS, -QROWS))
    shift0 = (width - (NA_KW - 1)) % width
    shift1 = (shift0 + GRID_W) % width

    def per_head(h, carry):
        for i in range(QROWS):
            jlo = jnp.where(is_first, 0, jnp.where(is_last, QROWS, i))
            for jj in range(KROWS // 2):
                halves = []
                for j, sh in ((2 * jj, shift0), (2 * jj + 1, shift1)):
                    ok = (j >= jlo) & (j < jlo + NA_KH_MAX)
                    idx = jnp.clip(j - i + base + NA_KH_MAX - 1, 0, 2 * NA_KH_MAX - 2)
                    row = jnp.where(ok, rp_ref[h, pl.ds(idx, 1), :], NEG_INF)
                    halves.append(pltpu.roll(jnp.broadcast_to(row, (GRID_W, width)), sh, 1,
                                             stride=1, stride_axis=0))
                tile = jnp.where(col_ok, jnp.where(lo_half, halves[0], halves[1]), NEG_INF)
                bias_buf[h, i * GRID_W:(i + 1) * GRID_W, jj * width:(jj + 1) * width] = tile
        return carry

    lax.fori_loop(0, NA_HEADS, per_head, 0)


def _attn_kernel(x_ref, yq_ref, kv0_ref, kv1_ref, kv2_ref, kvc_ref, rp_ref, mod_ref, wo_ref, o_ref,
                 kvbuf, bias_buf, *, nrb):
    rb = pl.program_id(0)
    n_loc = NKB * TOK

    @pl.when((pl.program_id(1) == 0) & ((rb <= 1) | (rb == nrb - 1)))
    def _():
        _fill_bias(bias_buf, rp_ref, rb, nrb)

    for j, kv_ref in enumerate((kv0_ref, kv1_ref, kv2_ref, kvc_ref)):
        kvbuf[j * TOK:(j + 1) * TOK] = kv_ref[...]

    def mix(probs):
        o_ref[...] = _mix_and_project(
            x_ref[...], yq_ref[:, :NA_CH], lambda p: yq_ref[:, _pair_lanes(p, NA_CH)],
            lambda p: kvbuf[:, _pair_lanes(p)], lambda p: kvbuf[:, _pair_lanes(p, NA_CH)], probs,
            wo_ref, mod_ref[5:6])

    def probs_any(h, s):
        return _softmax_probs(jnp.concatenate([s[:, :n_loc] + bias_buf[h], s[:, n_loc:]], axis=-1))

    def probs_interior(h, s):
        width = 2 * GRID_W
        out_rows = []
        for i in range(QROWS):
            rs = slice(i * GRID_W, (i + 1) * GRID_W)
            lo, hi = (i // 2) * width, ((i + NA_KH_MAX - 1) // 2 + 1) * width
            s_i = jnp.concatenate([s[rs, lo:hi] + bias_buf[h, rs, lo:hi], s[rs, n_loc:]], axis=-1)
            p_i = _softmax_probs(s_i)
            pieces = [p_i[:, :hi - lo], p_i[:, hi - lo:]]
            if lo:
                pieces.insert(0, jnp.zeros((GRID_W, lo), BF16))
            if hi < n_loc:
                pieces.insert(-1, jnp.zeros((GRID_W, n_loc - hi), BF16))
            out_rows.append(jnp.concatenate(pieces, axis=-1))
        return jnp.concatenate(out_rows, axis=0)

    interior = (rb >= 1) & (rb < nrb - 1)
    pl.when(interior)(lambda: mix(probs_interior))
    pl.when(jnp.logical_not(interior))(lambda: mix(probs_any))


def _attention(xs, mix, rp, mod, w_out, *, layer, n_batch, rows):
    nrb = rows // QROWS
    tiles_per_seq = nrb
    n_lat_tiles = n_batch * tiles_per_seq
    tok_blk = lambda rb, b: b * tiles_per_seq + rb
    ksb = lambda rb: jnp.clip(rb - 1, 0, nrb - NKB)

    def kv_spec(j):
        return pl.BlockSpec((TOK, 2 * NA_CH), lambda rb, b: (b * tiles_per_seq + ksb(rb) + j, 1))

    in_specs = [
        pl.BlockSpec((TOK, D_MODEL), lambda rb, b: (tok_blk(rb, b), 0)),
        pl.BlockSpec((TOK, 2 * NA_CH), lambda rb, b: (tok_blk(rb, b), 0)),
        kv_spec(0), kv_spec(1), kv_spec(2),
        pl.BlockSpec((TOK, 2 * NA_CH), lambda rb, b: (n_lat_tiles + b, 1)),
        pl.BlockSpec((None,) + rp.shape[1:], lambda rb, b: (layer, 0, 0, 0)),
        pl.BlockSpec((None, None, N_MOD, D_MODEL), lambda rb, b: (layer, b, 0, 0)),
        pl.BlockSpec((None, D_MODEL, D_MODEL), lambda rb, b: (layer, 0, 0)),
    ]
    return pl.pallas_call(
        functools.partial(_attn_kernel, nrb=nrb),
        grid=(nrb, n_batch),
        in_specs=in_specs,
        out_specs=pl.BlockSpec((TOK, D_MODEL), lambda rb, b: (tok_blk(rb, b), 0)),
        out_shape=jax.ShapeDtypeStruct(xs.shape, F32),
        scratch_shapes=[
            pltpu.VMEM((NKB * TOK + TOK, 2 * NA_CH), BF16),
            pltpu.VMEM((NA_HEADS, TOK, NKB * TOK), F32),
        ],
        input_output_aliases={0: 0},
        compiler_params=pltpu.CompilerParams(
            dimension_semantics=("arbitrary", "arbitrary"), vmem_limit_bytes=VMEM_LIMIT),
        name="attention",
    )(xs, mix, mix, mix, mix, mix, rp, mod, w_out)


def _ctx_kernel(x_ref, yq_ref, kv_ref, mod_ref, wo_ref, o_ref):
    o_ref[...] = _mix_and_project(
        x_ref[...], yq_ref[:, :NA_CH], lambda p: yq_ref[:, _pair_lanes(p, NA_CH)],
        lambda p: kv_ref[:, _pair_lanes(p)], lambda p: kv_ref[:, _pair_lanes(p, NA_CH)],
        lambda h, s: _softmax_probs(s), wo_ref, mod_ref[5:6])


def _ctx_attention(xs, mix, mod, w_out, *, layer, n_batch, n_lat_tiles):
    blk = lambda b: n_lat_tiles + b
    return pl.pallas_call(
        _ctx_kernel,
        grid=(n_batch,),
        in_specs=[
            pl.BlockSpec((TOK, D_MODEL), lambda b: (blk(b), 0)),
            pl.BlockSpec((TOK, 2 * NA_CH), lambda b: (blk(b), 0)),
            pl.BlockSpec((TOK, 2 * NA_CH), lambda b: (blk(b), 1)),
            pl.BlockSpec((None, None, N_MOD, D_MODEL), lambda b: (layer, n_batch, 0, 0)),
            pl.BlockSpec((None, D_MODEL, D_MODEL), lambda b: (layer, 0, 0)),
        ],
        out_specs=pl.BlockSpec((TOK, D_MODEL), lambda b: (blk(b), 0)),
        out_shape=jax.ShapeDtypeStruct(xs.shape, F32),
        input_output_aliases={0: 0},
        compiler_params=pltpu.CompilerParams(
            dimension_semantics=("arbitrary",), vmem_limit_bytes=VMEM_LIMIT),
        name="ctx_attention",
    )(xs, mix, mix, mod, w_out)


def _pad_rpb(rpb):
    depth, heads, n_off, n_col = rpb.shape
    out = jnp.full((depth, heads, n_off + 1, 2 * GRID_W), NEG_INF, F32)
    return out.at[:, :, :n_off, :n_col].set(rpb)


def _pool_block_diag(pool_w):
    depth, n = pool_w.shape[:2]
    out = jnp.zeros((depth, POOL_CH, POOL_CH), pool_w.dtype)
    for gi in range(n):
        sl = slice(gi * POOL_GROUP, (gi + 1) * POOL_GROUP)
        out = out.at[:, sl, sl].set(pool_w[:, gi])
    return out


def kernel(x, c, ctx, c_ctx, w_mod, b_mod, norm_g, ffn_w_in, ffn_w_out, w_in, conv_w, pool_w,
           pool_scale, rpb, w_out, final_g):
    n_batch, seq, d = x.shape
    ctx_len = ctx.shape[1]
    depth = w_mod.shape[0]
    rows = seq // GRID_W
    assert d == D_MODEL and ctx_len == TOK and seq % FFN_TM == 0 and rows >= KROWS
    assert (n_batch * ctx_len) % FFN_TM == 0 and rows // QROWS >= NKB + 1
    assert seq % (INPROJ_SUBTILES * TOK) == 0 and (n_batch * ctx_len) % (INPROJ_SUBTILES * TOK) == 0
    assert NA_KH_MAX // 2 == QROWS and rpb.shape[2:] == (2 * NA_KH_MAX - 1, 2 * NA_KW - 1)
    assert n_batch + 1 <= MOD_ROWS and w_in.shape[-1] == D_IN and ffn_w_in.shape[-1] == 2 * D_FF
    n_lat = n_batch * seq
    n_lat_tiles = n_lat // TOK
    tiles_per_seq = seq // TOK
    lat_ffn_tiles = n_lat // FFN_TM
    ctx_ffn_tiles = n_batch * ctx_len // FFN_TM

    cc = jnp.zeros((MOD_ROWS, d), F32).at[:n_batch].set(c).at[n_batch].set(c_ctx)
    mod = _modulation(cc, w_mod, b_mod).reshape(depth, MOD_ROWS, N_MOD, d)

    w1 = ffn_w_in.astype(BF16)
    w2 = ffn_w_out.astype(BF16)
    w_in_b = w_in.astype(BF16)
    w_out_b = w_out.astype(BF16)
    pool_bd = _pool_block_diag(pool_w).astype(BF16)
    pool_sc = pool_scale.reshape(depth, 1, POOL_CH)
    norm_g4 = norm_g.reshape(depth, 3, 1, d)
    rp = _pad_rpb(rpb)

    ffn_kw = dict(lat_tiles_per_batch=seq // FFN_TM, n_batch=n_batch)
    xa, xb, xb_off = x.reshape(n_lat, d), ctx.reshape(n_batch * ctx_len, d), 0
    for l in range(depth):
        last = l == depth - 1
        xs = _ffn(xa, xb, xb_off, lat_ffn_tiles, ctx_ffn_tiles, mod, norm_g4, w1, w2, final_g,
                  layer=l, which=0, sub=0, final=False, **ffn_kw)
        mix = _inproj(xs, mod, norm_g4, w_in_b, conv_w, pool_bd, pool_sc, layer=l,
                      n_lat_tiles=n_lat_tiles, lat_tiles_per_seq=tiles_per_seq, n_batch=n_batch)
        xs = _attention(xs, mix, rp, mod, w_out_b, layer=l, n_batch=n_batch, rows=rows)
        if not last:
            xs = _ctx_attention(xs, mix, mod, w_out_b, layer=l, n_batch=n_batch,
                                n_lat_tiles=n_lat_tiles)
        xs = _ffn(xs, xs, lat_ffn_tiles, lat_ffn_tiles, 0 if last else ctx_ffn_tiles, mod, norm_g4,
                  w1, w2, final_g, layer=l, which=1, sub=2, final=last, **ffn_kw)
        xa, xb, xb_off = xs, xs, lat_ffn_tiles
    return xs.reshape(n_batch, seq, d)
```

```python
import functools

import numpy as np
import jax
import jax.numpy as jnp
from jax import lax
from jax.experimental import pallas as pl
from jax.experimental.pallas import tpu as pltpu

F32 = jnp.float32
BF16 = jnp.bfloat16

D_MODEL = 1024
GRID_W = 64
D_FF = 2816
N_MOD = 9
RMS_EPS = 1e-6
NEG_INF = -1e30
CONV_CH = D_MODEL // 4
POOL_CH = D_MODEL // 4
POOL_WINDOWS = (2, 4, 8, 16)
POOL_GROUP = POOL_CH // 4
NA_HEAD_DIM = 64
NA_CH = D_MODEL // 2
NA_HEADS = NA_CH // NA_HEAD_DIM
NA_KH_MAX = 8
NA_KW = 16
OFF_Q = 3 * CONV_CH + POOL_CH
D_IN = OFF_Q + 3 * NA_CH

FFN_TM = 512
FFN_CHUNKS = ((0, 1536), (1536, 2816))
FFN_SUBTILES = 2
CAST_BLOCKS = 16
BF16_SUBLANES = 16
TOK = 256
HALO = 8
INPROJ_SUBTILES = 4
QROWS = TOK // GRID_W
KROWS = QROWS + NA_KH_MAX
NKB = KROWS * GRID_W // TOK
SCORE_LOOKAHEAD = 1
MOD_ROWS = 16
MOD_TN = 2304
VMEM_LIMIT = 56 * 1024 * 1024


def _rms_mod(x, g, scale, shift):
    ms = jnp.mean(x * x, axis=-1, keepdims=True)
    return (x * lax.rsqrt(ms + RMS_EPS) * g) * (1.0 + scale) + shift


def _mod_kernel(c_ref, w_ref, b_ref, o_ref):
    c = c_ref[...]
    s = (c * jax.nn.sigmoid(c)).astype(BF16)
    o_ref[...] = jnp.dot(s, w_ref[...].astype(BF16), preferred_element_type=F32) + b_ref[...]


def _modulation(cc, w_mod, b_mod):
    depth = w_mod.shape[0]
    n = N_MOD * D_MODEL
    return pl.pallas_call(
        _mod_kernel,
        grid=(depth, n // MOD_TN),
        in_specs=[
            pl.BlockSpec((MOD_ROWS, D_MODEL), lambda l, j: (0, 0)),
            pl.BlockSpec((None, D_MODEL, MOD_TN), lambda l, j: (l, 0, j)),
            pl.BlockSpec((None, 1, MOD_TN), lambda l, j: (l, 0, j)),
        ],
        out_specs=pl.BlockSpec((None, MOD_ROWS, MOD_TN), lambda l, j: (l, 0, j)),
        out_shape=jax.ShapeDtypeStruct((depth, MOD_ROWS, n), F32),
        compiler_params=pltpu.CompilerParams(
            dimension_semantics=("arbitrary", "arbitrary"), vmem_limit_bytes=VMEM_LIMIT),
        name="modulation",
    )(cc, w_mod, b_mod.reshape(depth, 1, n))


def _ffn_kernel(xa_ref, xb_ref, mod_ref, g_ref, w1_ref, w2_ref, fg_ref, *rest, sub, final,
                n_a_tiles, n_b_tiles):
    n_cast = (len(rest) - 1) // 2
    o_ref = rest[n_cast]
    for src_ref, dst_ref in zip(rest[:n_cast], rest[n_cast + 1:]):
        dst_ref[...] = src_ref[...].astype(BF16)
    m = mod_ref[...]
    shift = m[3 * sub:3 * sub + 1]
    scale = m[3 * sub + 1:3 * sub + 2]
    gate = m[3 * sub + 2:3 * sub + 3]
    rows = FFN_TM // FFN_SUBTILES

    def load_x(r):
        sl = slice(r * rows, (r + 1) * rows)
        if n_b_tiles:
            return jnp.where(pl.program_id(0) < n_a_tiles, xa_ref[sl], xb_ref[sl])
        return xa_ref[sl]

    units = [(r, c) for c in range(len(FFN_CHUNKS)) for r in range(FFN_SUBTILES)]
    xs, hbs, accs = {}, {}, {}

    def up(r, c):
        if r not in xs:
            xs[r] = load_x(r)
            hbs[r] = _rms_mod(xs[r], g_ref[...], scale, shift).astype(BF16)
        lo, hi = FFN_CHUNKS[c]
        a = jnp.dot(hbs[r], w1_ref[:, lo:hi], preferred_element_type=F32)
        b = jnp.dot(hbs[r], w1_ref[:, D_FF + lo:D_FF + hi], preferred_element_type=F32)
        return a, b

    ab = up(*units[0])
    for i, (r, c) in enumerate(units):
        a, b = ab
        if i + 1 < len(units):
            ab = up(*units[i + 1])
        lo, hi = FFN_CHUNKS[c]
        gg = (a * jax.nn.sigmoid(a) * b).astype(BF16)
        part = jnp.dot(gg, w2_ref[lo:hi, :], preferred_element_type=F32)
        accs[r] = part if r not in accs else accs[r] + part
    for r in range(FFN_SUBTILES):
        y = xs[r] + (0.5 * gate) * accs[r]
        if final:
            ms = jnp.mean(y * y, axis=-1, keepdims=True)
            y = y * lax.rsqrt(ms + RMS_EPS) * fg_ref[...]
        o_ref[r * rows:(r + 1) * rows] = y


def _ffn(xa, xb, xb_off, n_a_tiles, n_b_tiles, mod, norm_g4, w1, w2, final_g, cast_jobs, *, layer,
         sub, final, lat_tiles_per_batch, n_batch):
    n_tiles = n_a_tiles + n_b_tiles
    assert n_tiles >= CAST_BLOCKS
    const = lambda i: (0, 0)
    cast_blk = lambda i: jnp.minimum(i, CAST_BLOCKS - 1)
    cast_in_specs, cast_out_specs, cast_out_shapes = [], [], []
    for src, lead in cast_jobs:
        rows, cols = src.shape[-2:]
        assert rows % (CAST_BLOCKS * BF16_SUBLANES) == 0 and len(lead) == src.ndim - 2
        blk = (rows // CAST_BLOCKS, cols)
        cast_in_specs.append(pl.BlockSpec((None,) * len(lead) + blk,
                                          lambda i, lead=lead: lead + (cast_blk(i), 0)))
        cast_out_specs.append(pl.BlockSpec(blk, lambda i: (cast_blk(i), 0)))
        cast_out_shapes.append(jax.ShapeDtypeStruct((rows, cols), BF16))
    kern = functools.partial(_ffn_kernel, sub=sub, final=final, n_a_tiles=n_a_tiles,
                             n_b_tiles=n_b_tiles)
    outs = pl.pallas_call(
        kern,
        grid=(n_tiles,),
        in_specs=[
            pl.BlockSpec((FFN_TM, D_MODEL), lambda i: (jnp.minimum(i, n_a_tiles - 1), 0)),
            pl.BlockSpec((FFN_TM, D_MODEL), lambda i: (jnp.maximum(i - n_a_tiles, 0) + xb_off, 0)),
            pl.BlockSpec((None, None, N_MOD, D_MODEL),
                         lambda i: (layer, jnp.minimum(i // lat_tiles_per_batch, n_batch), 0, 0)),
            pl.BlockSpec((None, None, 1, D_MODEL), lambda i: (layer, sub, 0, 0)),
            pl.BlockSpec((D_MODEL, 2 * D_FF), const, pipeline_mode=pl.Buffered(1)),
            pl.BlockSpec((D_FF, D_MODEL), const, pipeline_mode=pl.Buffered(1)),
            pl.BlockSpec((1, D_MODEL), const),
        ] + cast_in_specs,
        out_specs=[pl.BlockSpec((FFN_TM, D_MODEL), lambda i: (i, 0))] + cast_out_specs,
        out_shape=[jax.ShapeDtypeStruct((n_tiles * FFN_TM, D_MODEL), F32)] + cast_out_shapes,
        compiler_params=pltpu.CompilerParams(
            dimension_semantics=("arbitrary",), vmem_limit_bytes=VMEM_LIMIT),
        name="ffn",
    )(xa, xb, mod, norm_g4, w1, w2, final_g.reshape(1, D_MODEL), *[src for src, _ in cast_jobs])
    return outs[0], outs[1:]


def _inproj_kernel(xp_ref, x_ref, xn_ref, mod_ref, g_ref, w_ref, cw_ref, pw_ref, ps_ref, o_ref,
                   *, n_lat_tiles, lat_tiles_per_seq):
    m = mod_ref[...]
    shift, scale = m[3:4], m[4:5]
    g = g_ref[...]
    ext = TOK + 2 * HALO

    def project(sub):
        lo = sub * TOK
        xp = xp_ref[...] if sub == 0 else x_ref[lo - HALO:lo]
        xn = xn_ref[...] if sub == INPROJ_SUBTILES - 1 else x_ref[lo + TOK:lo + TOK + HALO]
        xe = jnp.concatenate([xp, x_ref[lo:lo + TOK], xn], axis=0)
        he = _rms_mod(xe, g, scale, shift).astype(BF16)
        u = jnp.dot(he, w_ref[:, :OFF_Q], preferred_element_type=F32)
        qkv = jnp.dot(he[HALO:HALO + TOK], w_ref[:, OFF_Q:], preferred_element_type=F32)
        o_ref[lo:lo + TOK, 2 * CONV_CH:2 * CONV_CH + NA_CH] = (
            qkv[:, :NA_CH] * (NA_HEAD_DIM ** -0.5)).astype(BF16)
        o_ref[lo:lo + TOK, 2 * CONV_CH + NA_CH:] = qkv[:, NA_CH:].astype(BF16)
        return u

    def mix(sub, u):
        t = pl.program_id(0) * INPROJ_SUBTILES + sub
        is_lat = t < n_lat_tiles
        pos = jnp.where(is_lat, t % lat_tiles_per_seq, 0)
        ntile = jnp.where(is_lat, lat_tiles_per_seq, 1)
        t0 = pos * TOK
        seq_len = ntile * TOK

        row = lax.broadcasted_iota(jnp.int32, (ext, 1), 0)
        keep_lo = jnp.where(pos == 0, HALO, 0)
        keep_hi = jnp.where(pos == ntile - 1, HALO + TOK, ext)
        u = jnp.where((row >= keep_lo) & (row < keep_hi), u, 0.0)

        def shifted(a, s):
            return pltpu.roll(a, s % ext, 0)

        def centre(a):
            return a[HALO:HALO + TOK]

        z = u[:, 2 * CONV_CH:3 * CONV_CH] * u[:, :CONV_CH]
        cw = cw_ref[...]
        conv = (centre(shifted(z, 1)) * cw[0:1] + centre(z) * cw[1:2]
                + centre(shifted(z, -1)) * cw[2:3])
        y_conv = centre(u[:, CONV_CH:2 * CONV_CH]) * conv

        v = u[:, 3 * CONV_CH:]
        a1 = v + shifted(v, 1)
        a2 = a1 + shifted(a1, 2)
        a3 = a2 + shifted(a2, 4)
        a4 = a3 + shifted(a3, 8)
        sums = (centre(a1), centre(shifted(a2, -1)), centre(shifted(a3, -3)),
                centre(shifted(a4, -7)))
        tpos = t0 + lax.broadcasted_iota(jnp.int32, (TOK, 1), 0)
        lane = lax.broadcasted_iota(jnp.int32, (TOK, POOL_CH), 1)
        mean = None
        for gi in reversed(range(len(POOL_WINDOWS))):
            w = POOL_WINDOWS[gi]
            left = w // 2
            right = w - 1 - left
            cnt = jnp.minimum(tpos + right + 1, seq_len) - jnp.maximum(tpos - left, 0)
            mg = sums[gi] / cnt.astype(F32)
            mean = mg if mean is None else jnp.where(lane < (gi + 1) * POOL_GROUP, mg, mean)
        dpool = (mean - centre(v)).astype(BF16)
        y_pool = jnp.dot(dpool, pw_ref[...], preferred_element_type=F32) * ps_ref[...]
        lo = sub * TOK
        o_ref[lo:lo + TOK, :CONV_CH] = y_conv.astype(BF16)
        o_ref[lo:lo + TOK, CONV_CH:CONV_CH + POOL_CH] = y_pool.astype(BF16)

    us = [project(sub) for sub in range(INPROJ_SUBTILES)]
    for sub, u in enumerate(us):
        mix(sub, u)


def _inproj(xs, mod, norm_g4, w_in, conv_w, pool_bd, pool_scale, *, layer, n_lat_tiles,
            lat_tiles_per_seq, n_batch):
    nt = xs.shape[0]
    tm = INPROJ_SUBTILES * TOK
    n_steps = nt // tm
    hb = tm // HALO
    steps_per_seq = lat_tiles_per_seq // INPROJ_SUBTILES
    per_layer = lambda i: (layer, 0, 0)
    kern = functools.partial(_inproj_kernel, n_lat_tiles=n_lat_tiles,
                             lat_tiles_per_seq=lat_tiles_per_seq)
    return pl.pallas_call(
        kern,
        grid=(n_steps,),
        in_specs=[
            pl.BlockSpec((HALO, D_MODEL), lambda i: (jnp.maximum(i * hb - 1, 0), 0)),
            pl.BlockSpec((tm, D_MODEL), lambda i: (i, 0)),
            pl.BlockSpec((HALO, D_MODEL), lambda i: (jnp.minimum((i + 1) * hb, nt // HALO - 1), 0)),
            pl.BlockSpec((None, None, N_MOD, D_MODEL),
                         lambda i: (layer, jnp.minimum(i // steps_per_seq, n_batch), 0, 0)),
            pl.BlockSpec((None, None, 1, D_MODEL), lambda i: (layer, 1, 0, 0)),
            pl.BlockSpec((D_MODEL, D_IN), lambda i: (0, 0), pipeline_mode=pl.Buffered(1)),
            pl.BlockSpec((None, 3, CONV_CH), per_layer),
            pl.BlockSpec((None, POOL_CH, POOL_CH), per_layer),
            pl.BlockSpec((None, 1, POOL_CH), per_layer),
        ],
        out_specs=pl.BlockSpec((tm, 2 * D_MODEL), lambda i: (i, 0)),
        out_shape=jax.ShapeDtypeStruct((nt, 2 * D_MODEL), BF16),
        compiler_params=pltpu.CompilerParams(
            dimension_semantics=("arbitrary",), vmem_limit_bytes=VMEM_LIMIT),
        name="inproj",
    )(xs, xs, xs, mod, norm_g4, w_in, conv_w, pool_bd, pool_scale)


def _pair_lanes(p, offset=0):
    return slice(offset + 2 * p * NA_HEAD_DIM, offset + 2 * (p + 1) * NA_HEAD_DIM)


def _softmax_probs(s):
    mx = jnp.max(s, axis=-1, keepdims=True)
    return jnp.exp(s - mx).astype(BF16)


def _mix_and_project(x, ycp, q_of, k_of, v_of, probs, wo_ref, gate):
    lane = lax.broadcasted_iota(jnp.int32, (1, 2 * NA_HEAD_DIM), 1)
    own = [jnp.where(lane < NA_HEAD_DIM, 1.0, 0.0).astype(BF16),
           jnp.where(lane < NA_HEAD_DIM, 0.0, 1.0).astype(BF16)]

    def scores(t):
        p, half = divmod(t, 2)
        return lax.dot_general(q_of(p) * own[half], k_of(p), (((1,), (1,)), ((), ())),
                               preferred_element_type=F32)

    def normalised(outs, first_pair, n_pairs):
        normed = []
        for p in range(first_pair, first_pair + n_pairs):
            o0, o1 = outs[2 * p], outs[2 * p + 1]
            lo = lax.broadcasted_iota(jnp.int32, o0.shape, 1) < NA_HEAD_DIM
            num = jnp.where(lo, o0, o1)
            den = jnp.where(lo, pltpu.roll(o0, NA_HEAD_DIM, 1), pltpu.roll(o1, NA_HEAD_DIM, 1))
            normed.append((num / den).astype(BF16))
        return normed

    outs = []
    pending = [scores(t) for t in range(SCORE_LOOKAHEAD)]
    for t in range(NA_HEADS):
        s = pending.pop(0)
        if t + SCORE_LOOKAHEAD < NA_HEADS:
            pending.append(scores(t + SCORE_LOOKAHEAD))
        vm = v_of(t // 2) * own[t % 2] + own[1 - t % 2]
        outs.append(jnp.dot(probs(t, s), vm, preferred_element_type=F32))
    y = jnp.concatenate([ycp] + normalised(outs, 0, NA_HEADS // 2), axis=-1)
    return x + gate * jnp.dot(y, wo_ref[...], preferred_element_type=F32)


def _fill_bias(bias_buf, rp_ref, rb, nrb):
    width = 2 * GRID_W
    lane = lax.broadcasted_iota(jnp.int32, (GRID_W, width), 1)
    qc = lax.broadcasted_iota(jnp.int32, (GRID_W, width), 0)
    kc = lane & (GRID_W - 1)
    cstart = jnp.clip(qc - NA_KW // 2, 0, GRID_W - NA_KW)
    col_ok = (kc >= cstart) & (kc < cstart + NA_KW)
    lo_half = lane < GRID_W
    is_first = rb == 0
    is_last = rb == nrb - 1
    base = jnp.where(is_first, 0, jnp.where(is_last, -2 * QROWS, -QROWS))
    shift0 = (width - (NA_KW - 1)) % width
    shift1 = (shift0 + GRID_W) % width

    def per_head(h, carry):
        for i in range(QROWS):
            jlo = jnp.where(is_first, 0, jnp.where(is_last, QROWS, i))
            for jj in range(KROWS // 2):
                halves = []
                for j, sh in ((2 * jj, shift0), (2 * jj + 1, shift1)):
                    ok = (j >= jlo) & (j < jlo + NA_KH_MAX)
                    idx = jnp.clip(j - i + base + NA_KH_MAX - 1, 0, 2 * NA_KH_MAX - 2)
                    row = jnp.where(ok, rp_ref[h, pl.ds(idx, 1), :], NEG_INF)
                    halves.append(pltpu.roll(jnp.broadcast_to(row, (GRID_W, width)), sh, 1,
                                             stride=1, stride_axis=0))
                tile = jnp.where(col_ok, jnp.where(lo_half, halves[0], halves[1]), NEG_INF)
                bias_buf[h, i * GRID_W:(i + 1) * GRID_W, jj * width:(jj + 1) * width] = tile
        return carry

    lax.fori_loop(0, NA_HEADS, per_head, 0)


def _attn_kernel(x_ref, yq_ref, kv0_ref, kv1_ref, kv2_ref, kvc_ref, rp_ref, mod_ref, wo_ref, o_ref,
                 kvbuf, bias_buf, *, nrb):
    rb = pl.program_id(0)
    n_loc = NKB * TOK

    @pl.when((pl.program_id(1) == 0) & ((rb <= 1) | (rb == nrb - 1)))
    def _():
        _fill_bias(bias_buf, rp_ref, rb, nrb)

    for j, kv_ref in enumerate((kv0_ref, kv1_ref, kv2_ref, kvc_ref)):
        kvbuf[j * TOK:(j + 1) * TOK] = kv_ref[...]

    def mix(probs):
        o_ref[...] = _mix_and_project(
            x_ref[...], yq_ref[:, :NA_CH], lambda p: yq_ref[:, _pair_lanes(p, NA_CH)],
            lambda p: kvbuf[:, _pair_lanes(p)], lambda p: kvbuf[:, _pair_lanes(p, NA_CH)], probs,
            wo_ref, mod_ref[5:6])

    def probs_any(h, s):
        return _softmax_probs(jnp.concatenate([s[:, :n_loc] + bias_buf[h], s[:, n_loc:]], axis=-1))

    def probs_interior(h, s):
        width = 2 * GRID_W
        out_rows = []
        for i in range(QROWS):
            rs = slice(i * GRID_W, (i + 1) * GRID_W)
            lo, hi = (i // 2) * width, ((i + NA_KH_MAX - 1) // 2 + 1) * width
            s_i = jnp.concatenate([s[rs, lo:hi] + bias_buf[h, rs, lo:hi], s[rs, n_loc:]], axis=-1)
            p_i = _softmax_probs(s_i)
            pieces = [p_i[:, :hi - lo], p_i[:, hi - lo:]]
            if lo:
                pieces.insert(0, jnp.zeros((GRID_W, lo), BF16))
            if hi < n_loc:
                pieces.insert(-1, jnp.zeros((GRID_W, n_loc - hi), BF16))
            out_rows.append(jnp.concatenate(pieces, axis=-1))
        return jnp.concatenate(out_rows, axis=0)

    interior = (rb >= 1) & (rb < nrb - 1)
    pl.when(interior)(lambda: mix(probs_interior))
    pl.when(jnp.logical_not(interior))(lambda: mix(probs_any))


def _attention(xs, mix, rp, mod, w_out, *, layer, n_batch, rows):
    nrb = rows // QROWS
    tiles_per_seq = nrb
    n_lat_tiles = n_batch * tiles_per_seq
    tok_blk = lambda rb, b: b * tiles_per_seq + rb
    ksb = lambda rb: jnp.clip(rb - 1, 0, nrb - NKB)

    def kv_spec(j):
        return pl.BlockSpec((TOK, 2 * NA_CH), lambda rb, b: (b * tiles_per_seq + ksb(rb) + j, 1))

    in_specs = [
        pl.BlockSpec((TOK, D_MODEL), lambda rb, b: (tok_blk(rb, b), 0)),
        pl.BlockSpec((TOK, 2 * NA_CH), lambda rb, b: (tok_blk(rb, b), 0)),
        kv_spec(0), kv_spec(1), kv_spec(2),
        pl.BlockSpec((TOK, 2 * NA_CH), lambda rb, b: (n_lat_tiles + b, 1)),
        pl.BlockSpec((None,) + rp.shape[1:], lambda rb, b: (layer, 0, 0, 0)),
        pl.BlockSpec((None, None, N_MOD, D_MODEL), lambda rb, b: (layer, b, 0, 0)),
        pl.BlockSpec((D_MODEL, D_MODEL), lambda rb, b: (0, 0)),
    ]
    return pl.pallas_call(
        functools.partial(_attn_kernel, nrb=nrb),
        grid=(nrb, n_batch),
        in_specs=in_specs,
        out_specs=pl.BlockSpec((TOK, D_MODEL), lambda rb, b: (tok_blk(rb, b), 0)),
        out_shape=jax.ShapeDtypeStruct(xs.shape, F32),
        scratch_shapes=[
            pltpu.VMEM((NKB * TOK + TOK, 2 * NA_CH), BF16),
            pltpu.VMEM((NA_HEADS, TOK, NKB * TOK), F32),
        ],
        input_output_aliases={0: 0},
        compiler_params=pltpu.CompilerParams(
            dimension_semantics=("arbitrary", "arbitrary"), vmem_limit_bytes=VMEM_LIMIT),
        name="attention",
    )(xs, mix, mix, mix, mix, mix, rp, mod, w_out)


def _ctx_kernel(x_ref, yq_ref, kv_ref, mod_ref, wo_ref, o_ref):
    o_ref[...] = _mix_and_project(
        x_ref[...], yq_ref[:, :NA_CH], lambda p: yq_ref[:, _pair_lanes(p, NA_CH)],
        lambda p: kv_ref[:, _pair_lanes(p)], lambda p: kv_ref[:, _pair_lanes(p, NA_CH)],
        lambda h, s: _softmax_probs(s), wo_ref, mod_ref[5:6])


def _ctx_attention(xs, mix, mod, w_out, *, layer, n_batch, n_lat_tiles):
    blk = lambda b: n_lat_tiles + b
    return pl.pallas_call(
        _ctx_kernel,
        grid=(n_batch,),
        in_specs=[
            pl.BlockSpec((TOK, D_MODEL), lambda b: (blk(b), 0)),
            pl.BlockSpec((TOK, 2 * NA_CH), lambda b: (blk(b), 0)),
            pl.BlockSpec((TOK, 2 * NA_CH), lambda b: (blk(b), 1)),
            pl.BlockSpec((None, None, N_MOD, D_MODEL), lambda b: (layer, n_batch, 0, 0)),
            pl.BlockSpec((D_MODEL, D_MODEL), lambda b: (0, 0)),
        ],
        out_specs=pl.BlockSpec((TOK, D_MODEL), lambda b: (blk(b), 0)),
        out_shape=jax.ShapeDtypeStruct(xs.shape, F32),
        input_output_aliases={0: 0},
        compiler_params=pltpu.CompilerParams(
            dimension_semantics=("arbitrary",), vmem_limit_bytes=VMEM_LIMIT),
        name="ctx_attention",
    )(xs, mix, mix, mod, w_out)


def _pad_rpb(rpb):
    depth, heads, n_off, n_col = rpb.shape
    out = jnp.full((depth, heads, n_off + 1, 2 * GRID_W), NEG_INF, F32)
    return out.at[:, :, :n_off, :n_col].set(rpb)


def _pool_block_diag(pool_w):
    depth, n = pool_w.shape[:2]
    out = jnp.zeros((depth, POOL_CH, POOL_CH), pool_w.dtype)
    for gi in range(n):
        sl = slice(gi * POOL_GROUP, (gi + 1) * POOL_GROUP)
        out = out.at[:, sl, sl].set(pool_w[:, gi])
    return out


def kernel(x, c, ctx, c_ctx, w_mod, b_mod, norm_g, ffn_w_in, ffn_w_out, w_in, conv_w, pool_w,
           pool_scale, rpb, w_out, final_g):
    n_batch, seq, d = x.shape
    ctx_len = ctx.shape[1]
    depth = w_mod.shape[0]
    rows = seq // GRID_W
    assert d == D_MODEL and ctx_len == TOK and seq % FFN_TM == 0 and rows >= KROWS
    assert (n_batch * ctx_len) % FFN_TM == 0 and rows // QROWS >= NKB + 1
    assert seq % (INPROJ_SUBTILES * TOK) == 0 and (n_batch * ctx_len) % (INPROJ_SUBTILES * TOK) == 0
    assert NA_KH_MAX // 2 == QROWS and rpb.shape[2:] == (2 * NA_KH_MAX - 1, 2 * NA_KW - 1)
    assert n_batch + 1 <= MOD_ROWS and w_in.shape[-1] == D_IN and ffn_w_in.shape[-1] == 2 * D_FF
    n_lat = n_batch * seq
    n_lat_tiles = n_lat // TOK
    tiles_per_seq = seq // TOK
    lat_ffn_tiles = n_lat // FFN_TM
    ctx_ffn_tiles = n_batch * ctx_len // FFN_TM

    cc = jnp.zeros((MOD_ROWS, d), F32).at[:n_batch].set(c).at[n_batch].set(c_ctx)
    mod = _modulation(cc, w_mod, b_mod).reshape(depth, MOD_ROWS, N_MOD, d)

    pool_bd = _pool_block_diag(pool_w).astype(BF16)
    pool_sc = pool_scale.reshape(depth, 1, POOL_CH)
    norm_g4 = norm_g.reshape(depth, 3, 1, d)
    rp = _pad_rpb(rpb)

    w1 = ffn_w_in[0, 0].astype(BF16)
    w2 = ffn_w_out[0, 0].astype(BF16)
    ffn_kw = dict(lat_tiles_per_batch=seq // FFN_TM, n_batch=n_batch)
    xa, xb, xb_off = x.reshape(n_lat, d), ctx.reshape(n_batch * ctx_len, d), 0
    for l in range(depth):
        last = l == depth - 1
        jobs = [(ffn_w_in, (l, 1)), (ffn_w_out, (l, 1)), (w_in, (l,)), (w_out, (l,))]
        xs, (w1, w2, w_in_b, w_out_b) = _ffn(
            xa, xb, xb_off, lat_ffn_tiles, ctx_ffn_tiles, mod, norm_g4, w1, w2, final_g, jobs,
            layer=l, sub=0, final=False, **ffn_kw)
        mix = _inproj(xs, mod, norm_g4, w_in_b, conv_w, pool_bd, pool_sc, layer=l,
                      n_lat_tiles=n_lat_tiles, lat_tiles_per_seq=tiles_per_seq, n_batch=n_batch)
        xs = _attention(xs, mix, rp, mod, w_out_b, layer=l, n_batch=n_batch, rows=rows)
        if not last:
            xs = _ctx_attention(xs, mix, mod, w_out_b, layer=l, n_batch=n_batch,
                                n_lat_tiles=n_lat_tiles)
        jobs = [] if last else [(ffn_w_in, (l + 1, 0)), (ffn_w_out, (l + 1, 0))]
        xs, next_w = _ffn(xs, xs, lat_ffn_tiles, lat_ffn_tiles, 0 if last else ctx_ffn_tiles, mod,
                          norm_g4, w1, w2, final_g, jobs, layer=l, sub=2, final=last, **ffn_kw)
        if not last:
            w1, w2 = next_w
        xa, xb, xb_off = xs, xs, lat_ffn_tiles
    return xs.reshape(n_batch, seq, d)
```

```python
import functools

import numpy as np
import jax
import jax.numpy as jnp
from jax import lax
from jax.experimental import pallas as pl
from jax.experimental.pallas import tpu as pltpu

F32 = jnp.float32
BF16 = jnp.bfloat16

D_MODEL = 1024
GRID_W = 64
D_FF = 2816
N_MOD = 9
RMS_EPS = 1e-6
NEG_INF = -1e30
CONV_CH = D_MODEL // 4
POOL_CH = D_MODEL // 4
POOL_WINDOWS = (2, 4, 8, 16)
POOL_GROUP = POOL_CH // 4
NA_HEAD_DIM = 64
NA_CH = D_MODEL // 2
NA_HEADS = NA_CH // NA_HEAD_DIM
NA_KH_MAX = 8
NA_KW = 16
OFF_Q = 3 * CONV_CH + POOL_CH
D_IN = OFF_Q + 3 * NA_CH

FFN_TM = 1024
FFN_CHUNKS = ((0, 1536), (1536, 2816))
FFN_SUBTILES = 4
CAST_BLOCKS = 16
BF16_SUBLANES = 16
TOK = 256
HALO = 8
INPROJ_SUBTILES = 4
QROWS = TOK // GRID_W
KROWS = QROWS + NA_KH_MAX
NKB = KROWS * GRID_W // TOK
SCORE_LOOKAHEAD = 1
MOD_ROWS = 16
MOD_TN = 2304
VMEM_LIMIT = 56 * 1024 * 1024


def _rms_mod(x, g, scale, shift):
    ms = jnp.mean(x * x, axis=-1, keepdims=True)
    return (x * lax.rsqrt(ms + RMS_EPS) * g) * (1.0 + scale) + shift


def _mod_kernel(c_ref, w_ref, b_ref, o_ref):
    c = c_ref[...]
    s = (c * jax.nn.sigmoid(c)).astype(BF16)
    o_ref[...] = jnp.dot(s, w_ref[...].astype(BF16), preferred_element_type=F32) + b_ref[...]


def _modulation(cc, w_mod, b_mod):
    depth = w_mod.shape[0]
    n = N_MOD * D_MODEL
    return pl.pallas_call(
        _mod_kernel,
        grid=(depth, n // MOD_TN),
        in_specs=[
            pl.BlockSpec((MOD_ROWS, D_MODEL), lambda l, j: (0, 0)),
            pl.BlockSpec((None, D_MODEL, MOD_TN), lambda l, j: (l, 0, j)),
            pl.BlockSpec((None, 1, MOD_TN), lambda l, j: (l, 0, j)),
        ],
        out_specs=pl.BlockSpec((None, MOD_ROWS, MOD_TN), lambda l, j: (l, 0, j)),
        out_shape=jax.ShapeDtypeStruct((depth, MOD_ROWS, n), F32),
        compiler_params=pltpu.CompilerParams(
            dimension_semantics=("arbitrary", "arbitrary"), vmem_limit_bytes=VMEM_LIMIT),
        name="modulation",
    )(cc, w_mod, b_mod.reshape(depth, 1, n))


def _cast_job_specs(cast_jobs, n_steps):
    assert n_steps >= CAST_BLOCKS
    cast_blk = lambda i: jnp.minimum(i, CAST_BLOCKS - 1)
    in_specs, out_specs, out_shapes = [], [], []
    for src, lead in cast_jobs:
        rows, cols = src.shape[-2:]
        assert rows % (CAST_BLOCKS * BF16_SUBLANES) == 0 and len(lead) == src.ndim - 2
        blk = (rows // CAST_BLOCKS, cols)
        in_specs.append(pl.BlockSpec((None,) * len(lead) + blk,
                                     lambda i, lead=lead: lead + (cast_blk(i), 0)))
        out_specs.append(pl.BlockSpec(blk, lambda i: (cast_blk(i), 0)))
        out_shapes.append(jax.ShapeDtypeStruct((rows, cols), BF16))
    return in_specs, out_specs, out_shapes


def _run_cast_jobs(src_refs, dst_refs):
    for src_ref, dst_ref in zip(src_refs, dst_refs):
        dst_ref[...] = src_ref[...].astype(BF16)


def _ffn_kernel(xa_ref, xb_ref, mod_ref, g_ref, w1_ref, w2_ref, fg_ref, o_ref, *, sub, final,
                n_a_tiles, n_b_tiles):
    m = mod_ref[...]
    shift = m[3 * sub:3 * sub + 1]
    scale = m[3 * sub + 1:3 * sub + 2]
    gate = m[3 * sub + 2:3 * sub + 3]
    rows = FFN_TM // FFN_SUBTILES

    def load_x(r):
        sl = slice(r * rows, (r + 1) * rows)
        if n_b_tiles:
            return jnp.where(pl.program_id(0) < n_a_tiles, xa_ref[sl], xb_ref[sl])
        return xa_ref[sl]

    units = [(r, c) for c in range(len(FFN_CHUNKS)) for r in range(FFN_SUBTILES)]
    xs, hbs, accs = {}, {}, {}

    def up(r, c):
        if r not in xs:
            xs[r] = load_x(r)
            hbs[r] = _rms_mod(xs[r], g_ref[...], scale, shift).astype(BF16)
        lo, hi = FFN_CHUNKS[c]
        a = jnp.dot(hbs[r], w1_ref[:, lo:hi], preferred_element_type=F32)
        b = jnp.dot(hbs[r], w1_ref[:, D_FF + lo:D_FF + hi], preferred_element_type=F32)
        return a, b

    ab = up(*units[0])
    for i, (r, c) in enumerate(units):
        a, b = ab
        if i + 1 < len(units):
            ab = up(*units[i + 1])
        lo, hi = FFN_CHUNKS[c]
        gg = (a * jax.nn.sigmoid(a) * b).astype(BF16)
        part = jnp.dot(gg, w2_ref[lo:hi, :], preferred_element_type=F32)
        accs[r] = part if r not in accs else accs[r] + part
    for r in range(FFN_SUBTILES):
        y = xs[r] + (0.5 * gate) * accs[r]
        if final:
            ms = jnp.mean(y * y, axis=-1, keepdims=True)
            y = y * lax.rsqrt(ms + RMS_EPS) * fg_ref[...]
        o_ref[r * rows:(r + 1) * rows] = y


def _ffn(xa, xb, xb_off, n_a_tiles, n_b_tiles, mod, norm_g4, w1, w2, final_g, *, layer, sub, final,
         lat_tiles_per_batch, n_batch):
    n_tiles = n_a_tiles + n_b_tiles
    const = lambda i: (0, 0)
    kern = functools.partial(_ffn_kernel, sub=sub, final=final, n_a_tiles=n_a_tiles,
                             n_b_tiles=n_b_tiles)
    return pl.pallas_call(
        kern,
        grid=(n_tiles,),
        in_specs=[
            pl.BlockSpec((FFN_TM, D_MODEL), lambda i: (jnp.minimum(i, n_a_tiles - 1), 0)),
            pl.BlockSpec((FFN_TM, D_MODEL), lambda i: (jnp.maximum(i - n_a_tiles, 0) + xb_off, 0)),
            pl.BlockSpec((None, None, N_MOD, D_MODEL),
                         lambda i: (layer, jnp.minimum(i // lat_tiles_per_batch, n_batch), 0, 0)),
            pl.BlockSpec((None, None, 1, D_MODEL), lambda i: (layer, sub, 0, 0)),
            pl.BlockSpec((D_MODEL, 2 * D_FF), const, pipeline_mode=pl.Buffered(1)),
            pl.BlockSpec((D_FF, D_MODEL), const, pipeline_mode=pl.Buffered(1)),
            pl.BlockSpec((1, D_MODEL), const),
        ],
        out_specs=pl.BlockSpec((FFN_TM, D_MODEL), lambda i: (i, 0)),
        out_shape=jax.ShapeDtypeStruct((n_tiles * FFN_TM, D_MODEL), F32),
        compiler_params=pltpu.CompilerParams(
            dimension_semantics=("arbitrary",), vmem_limit_bytes=VMEM_LIMIT),
        name="ffn",
    )(xa, xb, mod, norm_g4, w1, w2, final_g.reshape(1, D_MODEL))


def _inproj_kernel(xp_ref, x_ref, xn_ref, mod_ref, g_ref, w_ref, cw_ref, pw_ref, ps_ref, *rest,
                   n_lat_tiles, lat_tiles_per_seq):
    n_cast = (len(rest) - 1) // 2
    o_ref = rest[n_cast]
    _run_cast_jobs(rest[:n_cast], rest[n_cast + 1:])
    m = mod_ref[...]
    shift, scale = m[3:4], m[4:5]
    g = g_ref[...]
    ext = TOK + 2 * HALO

    def project(sub):
        lo = sub * TOK
        xp = xp_ref[...] if sub == 0 else x_ref[lo - HALO:lo]
        xn = xn_ref[...] if sub == INPROJ_SUBTILES - 1 else x_ref[lo + TOK:lo + TOK + HALO]
        xe = jnp.concatenate([xp, x_ref[lo:lo + TOK], xn], axis=0)
        he = _rms_mod(xe, g, scale, shift).astype(BF16)
        u = jnp.dot(he, w_ref[:, :OFF_Q], preferred_element_type=F32)
        qkv = jnp.dot(he[HALO:HALO + TOK], w_ref[:, OFF_Q:], preferred_element_type=F32)
        o_ref[lo:lo + TOK, 2 * CONV_CH:2 * CONV_CH + NA_CH] = (
            qkv[:, :NA_CH] * (NA_HEAD_DIM ** -0.5)).astype(BF16)
        o_ref[lo:lo + TOK, 2 * CONV_CH + NA_CH:] = qkv[:, NA_CH:].astype(BF16)
        return u

    def mix(sub, u):
        t = pl.program_id(0) * INPROJ_SUBTILES + sub
        is_lat = t < n_lat_tiles
        pos = jnp.where(is_lat, t % lat_tiles_per_seq, 0)
        ntile = jnp.where(is_lat, lat_tiles_per_seq, 1)
        t0 = pos * TOK
        seq_len = ntile * TOK

        row = lax.broadcasted_iota(jnp.int32, (ext, 1), 0)
        keep_lo = jnp.where(pos == 0, HALO, 0)
        keep_hi = jnp.where(pos == ntile - 1, HALO + TOK, ext)
        u = jnp.where((row >= keep_lo) & (row < keep_hi), u, 0.0)

        def shifted(a, s):
            return pltpu.roll(a, s % ext, 0)

        def centre(a):
            return a[HALO:HALO + TOK]

        z = u[:, 2 * CONV_CH:3 * CONV_CH] * u[:, :CONV_CH]
        cw = cw_ref[...]
        conv = (centre(shifted(z, 1)) * cw[0:1] + centre(z) * cw[1:2]
                + centre(shifted(z, -1)) * cw[2:3])
        y_conv = centre(u[:, CONV_CH:2 * CONV_CH]) * conv

        v = u[:, 3 * CONV_CH:]
        a1 = v + shifted(v, 1)
        a2 = a1 + shifted(a1, 2)
        a3 = a2 + shifted(a2, 4)
        a4 = a3 + shifted(a3, 8)
        sums = (centre(a1), centre(shifted(a2, -1)), centre(shifted(a3, -3)),
                centre(shifted(a4, -7)))
        tpos = t0 + lax.broadcasted_iota(jnp.int32, (TOK, 1), 0)
        lane = lax.broadcasted_iota(jnp.int32, (TOK, POOL_CH), 1)
        mean = None
        for gi in reversed(range(len(POOL_WINDOWS))):
            w = POOL_WINDOWS[gi]
            left = w // 2
            right = w - 1 - left
            cnt = jnp.minimum(tpos + right + 1, seq_len) - jnp.maximum(tpos - left, 0)
            mg = sums[gi] / cnt.astype(F32)
            mean = mg if mean is None else jnp.where(lane < (gi + 1) * POOL_GROUP, mg, mean)
        dpool = (mean - centre(v)).astype(BF16)
        y_pool = jnp.dot(dpool, pw_ref[...], preferred_element_type=F32) * ps_ref[...]
        lo = sub * TOK
        o_ref[lo:lo + TOK, :CONV_CH] = y_conv.astype(BF16)
        o_ref[lo:lo + TOK, CONV_CH:CONV_CH + POOL_CH] = y_pool.astype(BF16)

    us = [project(sub) for sub in range(INPROJ_SUBTILES)]
    for sub, u in enumerate(us):
        mix(sub, u)


def _inproj(xs, mod, norm_g4, w_in, conv_w, pool_bd, pool_scale, cast_jobs, *, layer,
            n_lat_tiles, lat_tiles_per_seq, n_batch):
    nt = xs.shape[0]
    tm = INPROJ_SUBTILES * TOK
    n_steps = nt // tm
    hb = tm // HALO
    steps_per_seq = lat_tiles_per_seq // INPROJ_SUBTILES
    per_layer = lambda i: (layer, 0, 0)
    kern = functools.partial(_inproj_kernel, n_lat_tiles=n_lat_tiles,
                             lat_tiles_per_seq=lat_tiles_per_seq)
    cast_in_specs, cast_out_specs, cast_out_shapes = _cast_job_specs(cast_jobs, n_steps)
    outs = pl.pallas_call(
        kern,
        grid=(n_steps,),
        in_specs=[
            pl.BlockSpec((HALO, D_MODEL), lambda i: (jnp.maximum(i * hb - 1, 0), 0)),
            pl.BlockSpec((tm, D_MODEL), lambda i: (i, 0)),
            pl.BlockSpec((HALO, D_MODEL), lambda i: (jnp.minimum((i + 1) * hb, nt // HALO - 1), 0)),
            pl.BlockSpec((None, None, N_MOD, D_MODEL),
                         lambda i: (layer, jnp.minimum(i // steps_per_seq, n_batch), 0, 0)),
            pl.BlockSpec((None, None, 1, D_MODEL), lambda i: (layer, 1, 0, 0)),
            pl.BlockSpec((D_MODEL, D_IN), lambda i: (0, 0), pipeline_mode=pl.Buffered(1)),
            pl.BlockSpec((None, 3, CONV_CH), per_layer),
            pl.BlockSpec((None, POOL_CH, POOL_CH), per_layer),
            pl.BlockSpec((None, 1, POOL_CH), per_layer),
        ] + cast_in_specs,
        out_specs=[pl.BlockSpec((tm, 2 * D_MODEL), lambda i: (i, 0))] + cast_out_specs,
        out_shape=[jax.ShapeDtypeStruct((nt, 2 * D_MODEL), BF16)] + cast_out_shapes,
        compiler_params=pltpu.CompilerParams(
            dimension_semantics=("arbitrary",), vmem_limit_bytes=VMEM_LIMIT),
        name="inproj",
    )(xs, xs, xs, mod, norm_g4, w_in, conv_w, pool_bd, pool_scale, *[src for src, _ in cast_jobs])
    return outs[0], outs[1:]


def _pair_lanes(p, offset=0):
    return slice(offset + 2 * p * NA_HEAD_DIM, offset + 2 * (p + 1) * NA_HEAD_DIM)


def _softmax_probs(s):
    mx = jnp.max(s, axis=-1, keepdims=True)
    return jnp.exp(s - mx).astype(BF16)


def _mix_and_project(x, ycp, q_of, k_of, v_of, probs, wo_ref, gate):
    lane = lax.broadcasted_iota(jnp.int32, (1, 2 * NA_HEAD_DIM), 1)
    own = [jnp.where(lane < NA_HEAD_DIM, 1.0, 0.0).astype(BF16),
           jnp.where(lane < NA_HEAD_DIM, 0.0, 1.0).astype(BF16)]

    def scores(t):
        p, half = divmod(t, 2)
        return lax.dot_general(q_of(p) * own[half], k_of(p), (((1,), (1,)), ((), ())),
                               preferred_element_type=F32)

    def normalised(outs, first_pair, n_pairs):
        normed = []
        for p in range(first_pair, first_pair + n_pairs):
            o0, o1 = outs[2 * p], outs[2 * p + 1]
            lo = lax.broadcasted_iota(jnp.int32, o0.shape, 1) < NA_HEAD_DIM
            num = jnp.where(lo, o0, o1)
            den = jnp.where(lo, pltpu.roll(o0, NA_HEAD_DIM, 1), pltpu.roll(o1, NA_HEAD_DIM, 1))
            normed.append((num / den).astype(BF16))
        return normed

    outs = []
    pending = [scores(t) for t in range(SCORE_LOOKAHEAD)]
    for t in range(NA_HEADS):
        s = pending.pop(0)
        if t + SCORE_LOOKAHEAD < NA_HEADS:
            pending.append(scores(t + SCORE_LOOKAHEAD))
        vm = v_of(t // 2) * own[t % 2] + own[1 - t % 2]
        outs.append(jnp.dot(probs(t, s), vm, preferred_element_type=F32))
    y = jnp.concatenate([ycp] + normalised(outs, 0, NA_HEADS // 2), axis=-1)
    return x + gate * jnp.dot(y, wo_ref[...], preferred_element_type=F32)


def _fill_bias(bias_buf, rp_ref, rb, nrb):
    width = 2 * GRID_W
    lane = lax.broadcasted_iota(jnp.int32, (GRID_W, width), 1)
    qc = lax.broadcasted_iota(jnp.int32, (GRID_W, width), 0)
    kc = lane & (GRID_W - 1)
    cstart = jnp.clip(qc - NA_KW // 2, 0, GRID_W - NA_KW)
    col_ok = (kc >= cstart) & (kc < cstart + NA_KW)
    lo_half = lane < GRID_W
    is_first = rb == 0
    is_last = rb == nrb - 1
    base = jnp.where(is_first, 0, jnp.where(is_last, -2 * QROWS, -QROWS))
    shift0 = (width - (NA_KW - 1)) % width
    shift1 = (shift0 + GRID_W) % width

    def per_head(h, carry):
        for i in range(QROWS):
            jlo = jnp.where(is_first, 0, jnp.where(is_last, QROWS, i))
            for jj in range(KROWS // 2):
                halves = []
                for j, sh in ((2 * jj, shift0), (2 * jj + 1, shift1)):
                    ok = (j >= jlo) & (j < jlo + NA_KH_MAX)
                    idx = jnp.clip(j - i + base + NA_KH_MAX - 1, 0, 2 * NA_KH_MAX - 2)
                    row = jnp.where(ok, rp_ref[h, pl.ds(idx, 1), :], NEG_INF)
                    halves.append(pltpu.roll(jnp.broadcast_to(row, (GRID_W, width)), sh, 1,
                                             stride=1, stride_axis=0))
                tile = jnp.where(col_ok, jnp.where(lo_half, halves[0], halves[1]), NEG_INF)
                bias_buf[h, i * GRID_W:(i + 1) * GRID_W, jj * width:(jj + 1) * width] = tile
        return carry

    lax.fori_loop(0, NA_HEADS, per_head, 0)


def _attn_kernel(x_ref, yq_ref, kv0_ref, kv1_ref, kv2_ref, kvc_ref, rp_ref, mod_ref, wo_ref, o_ref,
                 kvbuf, bias_buf, *, nrb):
    rb = pl.program_id(0)
    n_loc = NKB * TOK

    @pl.when((pl.program_id(1) == 0) & ((rb <= 1) | (rb == nrb - 1)))
    def _():
        _fill_bias(bias_buf, rp_ref, rb, nrb)

    for j, kv_ref in enumerate((kv0_ref, kv1_ref, kv2_ref, kvc_ref)):
        kvbuf[j * TOK:(j + 1) * TOK] = kv_ref[...]

    def mix(probs):
        o_ref[...] = _mix_and_project(
            x_ref[...], yq_ref[:, :NA_CH], lambda p: yq_ref[:, _pair_lanes(p, NA_CH)],
            lambda p: kvbuf[:, _pair_lanes(p)], lambda p: kvbuf[:, _pair_lanes(p, NA_CH)], probs,
            wo_ref, mod_ref[5:6])

    def probs_any(h, s):
        return _softmax_probs(jnp.concatenate([s[:, :n_loc] + bias_buf[h], s[:, n_loc:]], axis=-1))

    def probs_interior(h, s):
        width = 2 * GRID_W
        out_rows = []
        for i in range(QROWS):
            rs = slice(i * GRID_W, (i + 1) * GRID_W)
            lo, hi = (i // 2) * width, ((i + NA_KH_MAX - 1) // 2 + 1) * width
            s_i = jnp.concatenate([s[rs, lo:hi] + bias_buf[h, rs, lo:hi], s[rs, n_loc:]], axis=-1)
            p_i = _softmax_probs(s_i)
            pieces = [p_i[:, :hi - lo], p_i[:, hi - lo:]]
            if lo:
                pieces.insert(0, jnp.zeros((GRID_W, lo), BF16))
            if hi < n_loc:
                pieces.insert(-1, jnp.zeros((GRID_W, n_loc - hi), BF16))
            out_rows.append(jnp.concatenate(pieces, axis=-1))
        return jnp.concatenate(out_rows, axis=0)

    interior = (rb >= 1) & (rb < nrb - 1)
    pl.when(interior)(lambda: mix(probs_interior))
    pl.when(jnp.logical_not(interior))(lambda: mix(probs_any))


def _attention(xs, mix, rp, mod, w_out, *, layer, n_batch, rows):
    nrb = rows // QROWS
    tiles_per_seq = nrb
    n_lat_tiles = n_batch * tiles_per_seq
    tok_blk = lambda rb, b: b * tiles_per_seq + rb
    ksb = lambda rb: jnp.clip(rb - 1, 0, nrb - NKB)

    def kv_spec(j):
        return pl.BlockSpec((TOK, 2 * NA_CH), lambda rb, b: (b * tiles_per_seq + ksb(rb) + j, 1))

    in_specs = [
        pl.BlockSpec((TOK, D_MODEL), lambda rb, b: (tok_blk(rb, b), 0)),
        pl.BlockSpec((TOK, 2 * NA_CH), lambda rb, b: (tok_blk(rb, b), 0)),
        kv_spec(0), kv_spec(1), kv_spec(2),
        pl.BlockSpec((TOK, 2 * NA_CH), lambda rb, b: (n_lat_tiles + b, 1)),
        pl.BlockSpec((None,) + rp.shape[1:], lambda rb, b: (layer, 0, 0, 0)),
        pl.BlockSpec((None, None, N_MOD, D_MODEL), lambda rb, b: (layer, b, 0, 0)),
        pl.BlockSpec((D_MODEL, D_MODEL), lambda rb, b: (0, 0)),
    ]
    return pl.pallas_call(
        functools.partial(_attn_kernel, nrb=nrb),
        grid=(nrb, n_batch),
        in_specs=in_specs,
        out_specs=pl.BlockSpec((TOK, D_MODEL), lambda rb, b: (tok_blk(rb, b), 0)),
        out_shape=jax.ShapeDtypeStruct(xs.shape, F32),
        scratch_shapes=[
            pltpu.VMEM((NKB * TOK + TOK, 2 * NA_CH), BF16),
            pltpu.VMEM((NA_HEADS, TOK, NKB * TOK), F32),
        ],
        input_output_aliases={0: 0},
        compiler_params=pltpu.CompilerParams(
            dimension_semantics=("arbitrary", "arbitrary"), vmem_limit_bytes=VMEM_LIMIT),
        name="attention",
    )(xs, mix, mix, mix, mix, mix, rp, mod, w_out)


def _ctx_kernel(x_ref, yq_ref, kv_ref, mod_ref, wo_ref, o_ref):
    o_ref[...] = _mix_and_project(
        x_ref[...], yq_ref[:, :NA_CH], lambda p: yq_ref[:, _pair_lanes(p, NA_CH)],
        lambda p: kv_ref[:, _pair_lanes(p)], lambda p: kv_ref[:, _pair_lanes(p, NA_CH)],
        lambda h, s: _softmax_probs(s), wo_ref, mod_ref[5:6])


def _ctx_attention(xs, mix, mod, w_out, *, layer, n_batch, n_lat_tiles):
    blk = lambda b: n_lat_tiles + b
    return pl.pallas_call(
        _ctx_kernel,
        grid=(n_batch,),
        in_specs=[
            pl.BlockSpec((TOK, D_MODEL), lambda b: (blk(b), 0)),
            pl.BlockSpec((TOK, 2 * NA_CH), lambda b: (blk(b), 0)),
            pl.BlockSpec((TOK, 2 * NA_CH), lambda b: (blk(b), 1)),
            pl.BlockSpec((None, None, N_MOD, D_MODEL), lambda b: (layer, n_batch, 0, 0)),
            pl.BlockSpec((D_MODEL, D_MODEL), lambda b: (0, 0)),
        ],
        out_specs=pl.BlockSpec((TOK, D_MODEL), lambda b: (blk(b), 0)),
        out_shape=jax.ShapeDtypeStruct(xs.shape, F32),
        input_output_aliases={0: 0},
        compiler_params=pltpu.CompilerParams(
            dimension_semantics=("arbitrary",), vmem_limit_bytes=VMEM_LIMIT),
        name="ctx_attention",
    )(xs, mix, mix, mod, w_out)


def _pad_rpb(rpb):
    depth, heads, n_off, n_col = rpb.shape
    out = jnp.full((depth, heads, n_off + 1, 2 * GRID_W), NEG_INF, F32)
    return out.at[:, :, :n_off, :n_col].set(rpb)


def _pool_block_diag(pool_w):
    depth, n = pool_w.shape[:2]
    out = jnp.zeros((depth, POOL_CH, POOL_CH), pool_w.dtype)
    for gi in range(n):
        sl = slice(gi * POOL_GROUP, (gi + 1) * POOL_GROUP)
        out = out.at[:, sl, sl].set(pool_w[:, gi])
    return out


def kernel(x, c, ctx, c_ctx, w_mod, b_mod, norm_g, ffn_w_in, ffn_w_out, w_in, conv_w, pool_w,
           pool_scale, rpb, w_out, final_g):
    n_batch, seq, d = x.shape
    ctx_len = ctx.shape[1]
    depth = w_mod.shape[0]
    rows = seq // GRID_W
    assert d == D_MODEL and ctx_len == TOK and seq % FFN_TM == 0 and rows >= KROWS
    assert (n_batch * ctx_len) % FFN_TM == 0 and rows // QROWS >= NKB + 1
    assert seq % (INPROJ_SUBTILES * TOK) == 0 and (n_batch * ctx_len) % (INPROJ_SUBTILES * TOK) == 0
    assert NA_KH_MAX // 2 == QROWS and rpb.shape[2:] == (2 * NA_KH_MAX - 1, 2 * NA_KW - 1)
    assert n_batch + 1 <= MOD_ROWS and w_in.shape[-1] == D_IN and ffn_w_in.shape[-1] == 2 * D_FF
    n_lat = n_batch * seq
    n_lat_tiles = n_lat // TOK
    tiles_per_seq = seq // TOK
    lat_ffn_tiles = n_lat // FFN_TM
    ctx_ffn_tiles = n_batch * ctx_len // FFN_TM

    cc = jnp.zeros((MOD_ROWS, d), F32).at[:n_batch].set(c).at[n_batch].set(c_ctx)
    mod = _modulation(cc, w_mod, b_mod).reshape(depth, MOD_ROWS, N_MOD, d)

    pool_bd = _pool_block_diag(pool_w).astype(BF16)
    pool_sc = pool_scale.reshape(depth, 1, POOL_CH)
    norm_g4 = norm_g.reshape(depth, 3, 1, d)
    rp = _pad_rpb(rpb)

    bf = lambda a: a.astype(BF16)
    w1, w2, w_in_b, w_out_b = bf(ffn_w_in[0, 0]), bf(ffn_w_out[0, 0]), bf(w_in[0]), bf(w_out[0])
    ffn_kw = dict(lat_tiles_per_batch=seq // FFN_TM, n_batch=n_batch)
    xa, xb, xb_off = x.reshape(n_lat, d), ctx.reshape(n_batch * ctx_len, d), 0
    for l in range(depth):
        last = l == depth - 1
        xs = _ffn(xa, xb, xb_off, lat_ffn_tiles, ctx_ffn_tiles, mod, norm_g4, w1, w2, final_g,
                  layer=l, sub=0, final=False, **ffn_kw)
        jobs = [(ffn_w_in, (l, 1)), (ffn_w_out, (l, 1))]
        if not last:
            jobs += [(ffn_w_in, (l + 1, 0)), (ffn_w_out, (l + 1, 0)), (w_in, (l + 1,)),
                     (w_out, (l + 1,))]
        mix, cast = _inproj(xs, mod, norm_g4, w_in_b, conv_w, pool_bd, pool_sc, jobs, layer=l,
                            n_lat_tiles=n_lat_tiles, lat_tiles_per_seq=tiles_per_seq,
                            n_batch=n_batch)
        xs = _attention(xs, mix, rp, mod, w_out_b, layer=l, n_batch=n_batch, rows=rows)
        if not last:
            xs = _ctx_attention(xs, mix, mod, w_out_b, layer=l, n_batch=n_batch,
                                n_lat_tiles=n_lat_tiles)
        xs = _ffn(xs, xs, lat_ffn_tiles, lat_ffn_tiles, 0 if last else ctx_ffn_tiles, mod, norm_g4,
                  cast[0], cast[1], final_g, layer=l, sub=2, final=last, **ffn_kw)
        if not last:
            w1, w2, w_in_b, w_out_b = cast[2:]
        xa, xb, xb_off = xs, xs, lat_ffn_tiles
    return xs.reshape(n_batch, seq, d)
```

```python
import functools

import numpy as np
import jax
import jax.numpy as jnp
from jax import lax
from jax.experimental import pallas as pl
from jax.experimental.pallas import tpu as pltpu

F32 = jnp.float32
BF16 = jnp.bfloat16

D_MODEL = 1024
GRID_W = 64
D_FF = 2816
N_MOD = 9
RMS_EPS = 1e-6
NEG_INF = -1e30
CONV_CH = D_MODEL // 4
POOL_CH = D_MODEL // 4
POOL_WINDOWS = (2, 4, 8, 16)
POOL_GROUP = POOL_CH // 4
NA_HEAD_DIM = 64
NA_CH = D_MODEL // 2
NA_HEADS = NA_CH // NA_HEAD_DIM
NA_KH_MAX = 8
NA_KW = 16
OFF_Q = 3 * CONV_CH + POOL_CH
D_IN = OFF_Q + 3 * NA_CH

FFN_TM = 1024
FFN_CHUNKS = ((0, 1536), (1536, 2816))
FFN_SUBTILES = 4
CAST_BLOCKS = 16
BF16_SUBLANES = 16
TOK = 256
HALO = 8
INPROJ_SUBTILES = 4
QROWS = TOK // GRID_W
KROWS = QROWS + NA_KH_MAX
NKB = KROWS * GRID_W // TOK
SCORE_LOOKAHEAD = 1
ATTN_BATCHES = 2
MOD_ROWS = 16
MOD_TN = 2304
VMEM_LIMIT = 56 * 1024 * 1024


def _rms_mod(x, g, scale, shift):
    ms = jnp.mean(x * x, axis=-1, keepdims=True)
    return (x * lax.rsqrt(ms + RMS_EPS) * g) * (1.0 + scale) + shift


def _mod_kernel(c_ref, w_ref, b_ref, o_ref):
    c = c_ref[...]
    s = (c * jax.nn.sigmoid(c)).astype(BF16)
    o_ref[...] = jnp.dot(s, w_ref[...].astype(BF16), preferred_element_type=F32) + b_ref[...]


def _modulation(cc, w_mod, b_mod):
    depth = w_mod.shape[0]
    n = N_MOD * D_MODEL
    return pl.pallas_call(
        _mod_kernel,
        grid=(depth, n // MOD_TN),
        in_specs=[
            pl.BlockSpec((MOD_ROWS, D_MODEL), lambda l, j: (0, 0)),
            pl.BlockSpec((None, D_MODEL, MOD_TN), lambda l, j: (l, 0, j)),
            pl.BlockSpec((None, 1, MOD_TN), lambda l, j: (l, 0, j)),
        ],
        out_specs=pl.BlockSpec((None, MOD_ROWS, MOD_TN), lambda l, j: (l, 0, j)),
        out_shape=jax.ShapeDtypeStruct((depth, MOD_ROWS, n), F32),
        compiler_params=pltpu.CompilerParams(
            dimension_semantics=("arbitrary", "arbitrary"), vmem_limit_bytes=VMEM_LIMIT),
        name="modulation",
    )(cc, w_mod, b_mod.reshape(depth, 1, n))


def _cast_job_specs(cast_jobs, n_steps):
    assert n_steps >= CAST_BLOCKS
    cast_blk = lambda i: jnp.minimum(i, CAST_BLOCKS - 1)
    in_specs, out_specs, out_shapes = [], [], []
    for src, lead in cast_jobs:
        rows, cols = src.shape[-2:]
        assert rows % (CAST_BLOCKS * BF16_SUBLANES) == 0 and len(lead) == src.ndim - 2
        blk = (rows // CAST_BLOCKS, cols)
        in_specs.append(pl.BlockSpec((None,) * len(lead) + blk,
                                     lambda i, lead=lead: lead + (cast_blk(i), 0)))
        out_specs.append(pl.BlockSpec(blk, lambda i: (cast_blk(i), 0)))
        out_shapes.append(jax.ShapeDtypeStruct((rows, cols), BF16))
    return in_specs, out_specs, out_shapes


def _run_cast_jobs(src_refs, dst_refs):
    for src_ref, dst_ref in zip(src_refs, dst_refs):
        dst_ref[...] = src_ref[...].astype(BF16)


def _ffn_kernel(xa_ref, xb_ref, mod_ref, g_ref, w1_ref, w2_ref, fg_ref, o_ref, *, sub, final,
                n_a_tiles, n_b_tiles):
    m = mod_ref[...]
    shift = m[3 * sub:3 * sub + 1]
    scale = m[3 * sub + 1:3 * sub + 2]
    gate = m[3 * sub + 2:3 * sub + 3]
    rows = FFN_TM // FFN_SUBTILES

    def load_x(r):
        sl = slice(r * rows, (r + 1) * rows)
        if n_b_tiles:
            return jnp.where(pl.program_id(0) < n_a_tiles, xa_ref[sl], xb_ref[sl])
        return xa_ref[sl]

    units = [(r, c) for c in range(len(FFN_CHUNKS)) for r in range(FFN_SUBTILES)]
    xs, hbs, accs = {}, {}, {}

    def up(r, c):
        if r not in xs:
            xs[r] = load_x(r)
            hbs[r] = _rms_mod(xs[r], g_ref[...], scale, shift).astype(BF16)
        lo, hi = FFN_CHUNKS[c]
        a = jnp.dot(hbs[r], w1_ref[:, lo:hi], preferred_element_type=F32)
        b = jnp.dot(hbs[r], w1_ref[:, D_FF + lo:D_FF + hi], preferred_element_type=F32)
        return a, b

    ab = up(*units[0])
    for i, (r, c) in enumerate(units):
        a, b = ab
        if i + 1 < len(units):
            ab = up(*units[i + 1])
        lo, hi = FFN_CHUNKS[c]
        gg = (a * jax.nn.sigmoid(a) * b).astype(BF16)
        part = jnp.dot(gg, w2_ref[lo:hi, :], preferred_element_type=F32)
        accs[r] = part if r not in accs else accs[r] + part
    for r in range(FFN_SUBTILES):
        y = xs[r] + (0.5 * gate) * accs[r]
        if final:
            ms = jnp.mean(y * y, axis=-1, keepdims=True)
            y = y * lax.rsqrt(ms + RMS_EPS) * fg_ref[...]
        o_ref[r * rows:(r + 1) * rows] = y


def _ffn(xa, xb, xb_off, n_a_tiles, n_b_tiles, mod, norm_g4, w1, w2, final_g, *, layer, sub, final,
         lat_tiles_per_batch, n_batch):
    n_tiles = n_a_tiles + n_b_tiles
    const = lambda i: (0, 0)
    kern = functools.partial(_ffn_kernel, sub=sub, final=final, n_a_tiles=n_a_tiles,
                             n_b_tiles=n_b_tiles)
    return pl.pallas_call(
        kern,
        grid=(n_tiles,),
        in_specs=[
            pl.BlockSpec((FFN_TM, D_MODEL), lambda i: (jnp.minimum(i, n_a_tiles - 1), 0)),
            pl.BlockSpec((FFN_TM, D_MODEL), lambda i: (jnp.maximum(i - n_a_tiles, 0) + xb_off, 0)),
            pl.BlockSpec((None, None, N_MOD, D_MODEL),
                         lambda i: (layer, jnp.minimum(i // lat_tiles_per_batch, n_batch), 0, 0)),
            pl.BlockSpec((None, None, 1, D_MODEL), lambda i: (layer, sub, 0, 0)),
            pl.BlockSpec((D_MODEL, 2 * D_FF), const, pipeline_mode=pl.Buffered(1)),
            pl.BlockSpec((D_FF, D_MODEL), const, pipeline_mode=pl.Buffered(1)),
            pl.BlockSpec((1, D_MODEL), const),
        ],
        out_specs=pl.BlockSpec((FFN_TM, D_MODEL), lambda i: (i, 0)),
        out_shape=jax.ShapeDtypeStruct((n_tiles * FFN_TM, D_MODEL), F32),
        compiler_params=pltpu.CompilerParams(
            dimension_semantics=("arbitrary",), vmem_limit_bytes=VMEM_LIMIT),
        name="ffn",
    )(xa, xb, mod, norm_g4, w1, w2, final_g.reshape(1, D_MODEL))


def _inproj_kernel(xp_ref, x_ref, xn_ref, mod_ref, g_ref, w_ref, cw_ref, pw_ref, ps_ref, *rest,
                   n_lat_tiles, lat_tiles_per_seq):
    n_cast = (len(rest) - 1) // 2
    o_ref = rest[n_cast]
    _run_cast_jobs(rest[:n_cast], rest[n_cast + 1:])
    m = mod_ref[...]
    shift, scale = m[3:4], m[4:5]
    g = g_ref[...]
    ext = TOK + 2 * HALO

    def project(sub):
        lo = sub * TOK
        xp = xp_ref[...] if sub == 0 else x_ref[lo - HALO:lo]
        xn = xn_ref[...] if sub == INPROJ_SUBTILES - 1 else x_ref[lo + TOK:lo + TOK + HALO]
        xe = jnp.concatenate([xp, x_ref[lo:lo + TOK], xn], axis=0)
        he = _rms_mod(xe, g, scale, shift).astype(BF16)
        u = jnp.dot(he, w_ref[:, :OFF_Q], preferred_element_type=F32)
        qkv = jnp.dot(he[HALO:HALO + TOK], w_ref[:, OFF_Q:], preferred_element_type=F32)
        o_ref[lo:lo + TOK, 2 * CONV_CH:2 * CONV_CH + NA_CH] = (
            qkv[:, :NA_CH] * (NA_HEAD_DIM ** -0.5)).astype(BF16)
        o_ref[lo:lo + TOK, 2 * CONV_CH + NA_CH:] = qkv[:, NA_CH:].astype(BF16)
        return u

    def mix(sub, u):
        t = pl.program_id(0) * INPROJ_SUBTILES + sub
        is_lat = t < n_lat_tiles
        pos = jnp.where(is_lat, t % lat_tiles_per_seq, 0)
        ntile = jnp.where(is_lat, lat_tiles_per_seq, 1)
        t0 = pos * TOK
        seq_len = ntile * TOK

        row = lax.broadcasted_iota(jnp.int32, (ext, 1), 0)
        keep_lo = jnp.where(pos == 0, HALO, 0)
        keep_hi = jnp.where(pos == ntile - 1, HALO + TOK, ext)
        u = jnp.where((row >= keep_lo) & (row < keep_hi), u, 0.0)

        def shifted(a, s):
            return pltpu.roll(a, s % ext, 0)

        def centre(a):
            return a[HALO:HALO + TOK]

        z = u[:, 2 * CONV_CH:3 * CONV_CH] * u[:, :CONV_CH]
        cw = cw_ref[...]
        conv = (centre(shifted(z, 1)) * cw[0:1] + centre(z) * cw[1:2]
                + centre(shifted(z, -1)) * cw[2:3])
        y_conv = centre(u[:, CONV_CH:2 * CONV_CH]) * conv

        v = u[:, 3 * CONV_CH:]
        a1 = v + shifted(v, 1)
        a2 = a1 + shifted(a1, 2)
        a3 = a2 + shifted(a2, 4)
        a4 = a3 + shifted(a3, 8)
        sums = (centre(a1), centre(shifted(a2, -1)), centre(shifted(a3, -3)),
                centre(shifted(a4, -7)))
        tpos = t0 + lax.broadcasted_iota(jnp.int32, (TOK, 1), 0)
        lane = lax.broadcasted_iota(jnp.int32, (TOK, POOL_CH), 1)
        mean = None
        for gi in reversed(range(len(POOL_WINDOWS))):
            w = POOL_WINDOWS[gi]
            left = w // 2
            right = w - 1 - left
            cnt = jnp.minimum(tpos + right + 1, seq_len) - jnp.maximum(tpos - left, 0)
            mg = sums[gi] / cnt.astype(F32)
            mean = mg if mean is None else jnp.where(lane < (gi + 1) * POOL_GROUP, mg, mean)
        dpool = (mean - centre(v)).astype(BF16)
        y_pool = jnp.dot(dpool, pw_ref[...], preferred_element_type=F32) * ps_ref[...]
        lo = sub * TOK
        o_ref[lo:lo + TOK, :CONV_CH] = y_conv.astype(BF16)
        o_ref[lo:lo + TOK, CONV_CH:CONV_CH + POOL_CH] = y_pool.astype(BF16)

    us = [project(sub) for sub in range(INPROJ_SUBTILES)]
    for sub, u in enumerate(us):
        mix(sub, u)


def _inproj(xs, mod, norm_g4, w_in, conv_w, pool_bd, pool_scale, cast_jobs, *, layer,
            n_lat_tiles, lat_tiles_per_seq, n_batch):
    nt = xs.shape[0]
    tm = INPROJ_SUBTILES * TOK
    n_steps = nt // tm
    hb = tm // HALO
    steps_per_seq = lat_tiles_per_seq // INPROJ_SUBTILES
    per_layer = lambda i: (layer, 0, 0)
    kern = functools.partial(_inproj_kernel, n_lat_tiles=n_lat_tiles,
                             lat_tiles_per_seq=lat_tiles_per_seq)
    cast_in_specs, cast_out_specs, cast_out_shapes = _cast_job_specs(cast_jobs, n_steps)
    outs = pl.pallas_call(
        kern,
        grid=(n_steps,),
        in_specs=[
            pl.BlockSpec((HALO, D_MODEL), lambda i: (jnp.maximum(i * hb - 1, 0), 0)),
            pl.BlockSpec((tm, D_MODEL), lambda i: (i, 0)),
            pl.BlockSpec((HALO, D_MODEL), lambda i: (jnp.minimum((i + 1) * hb, nt // HALO - 1), 0)),
            pl.BlockSpec((None, None, N_MOD, D_MODEL),
                         lambda i: (layer, jnp.minimum(i // steps_per_seq, n_batch), 0, 0)),
            pl.BlockSpec((None, None, 1, D_MODEL), lambda i: (layer, 1, 0, 0)),
            pl.BlockSpec((D_MODEL, D_IN), lambda i: (0, 0), pipeline_mode=pl.Buffered(1)),
            pl.BlockSpec((None, 3, CONV_CH), per_layer),
            pl.BlockSpec((None, POOL_CH, POOL_CH), per_layer),
            pl.BlockSpec((None, 1, POOL_CH), per_layer),
        ] + cast_in_specs,
        out_specs=[pl.BlockSpec((tm, 2 * D_MODEL), lambda i: (i, 0))] + cast_out_specs,
        out_shape=[jax.ShapeDtypeStruct((nt, 2 * D_MODEL), BF16)] + cast_out_shapes,
        compiler_params=pltpu.CompilerParams(
            dimension_semantics=("arbitrary",), vmem_limit_bytes=VMEM_LIMIT),
        name="inproj",
    )(xs, xs, xs, mod, norm_g4, w_in, conv_w, pool_bd, pool_scale, *[src for src, _ in cast_jobs])
    return outs[0], outs[1:]


def _pair_lanes(p, offset=0):
    return slice(offset + 2 * p * NA_HEAD_DIM, offset + 2 * (p + 1) * NA_HEAD_DIM)


def _softmax_probs(s):
    mx = jnp.max(s, axis=-1, keepdims=True)
    return jnp.exp(s - mx).astype(BF16)


def _mix_and_project(x, ycp, q_of, k_of, v_of, probs, wo_ref, gate):
    lane = lax.broadcasted_iota(jnp.int32, (1, 2 * NA_HEAD_DIM), 1)
    own = [jnp.where(lane < NA_HEAD_DIM, 1.0, 0.0).astype(BF16),
           jnp.where(lane < NA_HEAD_DIM, 0.0, 1.0).astype(BF16)]

    def scores(t):
        p, half = divmod(t, 2)
        return lax.dot_general(q_of(p) * own[half], k_of(p), (((1,), (1,)), ((), ())),
                               preferred_element_type=F32)

    def normalised(outs, first_pair, n_pairs):
        normed = []
        for p in range(first_pair, first_pair + n_pairs):
            o0, o1 = outs[2 * p], outs[2 * p + 1]
            lo = lax.broadcasted_iota(jnp.int32, o0.shape, 1) < NA_HEAD_DIM
            num = jnp.where(lo, o0, o1)
            den = jnp.where(lo, pltpu.roll(o0, NA_HEAD_DIM, 1), pltpu.roll(o1, NA_HEAD_DIM, 1))
            normed.append((num / den).astype(BF16))
        return normed

    outs = []
    pending = [scores(t) for t in range(SCORE_LOOKAHEAD)]
    for t in range(NA_HEADS):
        s = pending.pop(0)
        if t + SCORE_LOOKAHEAD < NA_HEADS:
            pending.append(scores(t + SCORE_LOOKAHEAD))
        vm = v_of(t // 2) * own[t % 2] + own[1 - t % 2]
        outs.append(jnp.dot(probs(t, s), vm, preferred_element_type=F32))
    y = jnp.concatenate([ycp] + normalised(outs, 0, NA_HEADS // 2), axis=-1)
    return x + gate * jnp.dot(y, wo_ref[...], preferred_element_type=F32)


def _fill_bias(bias_buf, rp_ref, rb, nrb):
    width = 2 * GRID_W
    lane = lax.broadcasted_iota(jnp.int32, (GRID_W, width), 1)
    qc = lax.broadcasted_iota(jnp.int32, (GRID_W, width), 0)
    kc = lane & (GRID_W - 1)
    cstart = jnp.clip(qc - NA_KW // 2, 0, GRID_W - NA_KW)
    col_ok = (kc >= cstart) & (kc < cstart + NA_KW)
    lo_half = lane < GRID_W
    is_first = rb == 0
    is_last = rb == nrb - 1
    base = jnp.where(is_first, 0, jnp.where(is_last, -2 * QROWS, -QROWS))
    shift0 = (width - (NA_KW - 1)) % width
    shift1 = (shift0 + GRID_W) % width

    def per_head(h, carry):
        for i in range(QROWS):
            jlo = jnp.where(is_first, 0, jnp.where(is_last, QROWS, i))
            for jj in range(KROWS // 2):
                halves = []
                for j, sh in ((2 * jj, shift0), (2 * jj + 1, shift1)):
                    ok = (j >= jlo) & (j < jlo + NA_KH_MAX)
                    idx = jnp.clip(j - i + base + NA_KH_MAX - 1, 0, 2 * NA_KH_MAX - 2)
                    row = jnp.where(ok, rp_ref[h, pl.ds(idx, 1), :], NEG_INF)
                    halves.append(pltpu.roll(jnp.broadcast_to(row, (GRID_W, width)), sh, 1,
                                             stride=1, stride_axis=0))
                tile = jnp.where(col_ok, jnp.where(lo_half, halves[0], halves[1]), NEG_INF)
                bias_buf[h, i * GRID_W:(i + 1) * GRID_W, jj * width:(jj + 1) * width] = tile
        return carry

    lax.fori_loop(0, NA_HEADS, per_head, 0)


def _attn_kernel(x_ref, yq_ref, kv0_ref, kv1_ref, kv2_ref, kvc_ref, rp_ref, mod_ref, wo_ref, o_ref,
                 kvbuf, bias_buf, *, nrb):
    rb = pl.program_id(0)
    n_loc = NKB * TOK

    @pl.when((pl.program_id(1) == 0) & ((rb <= 1) | (rb == nrb - 1)))
    def _():
        _fill_bias(bias_buf, rp_ref, rb, nrb)

    for g in range(ATTN_BATCHES):
        for j, kv_ref in enumerate((kv0_ref, kv1_ref, kv2_ref)):
            kvbuf[g, j * TOK:(j + 1) * TOK] = kv_ref[g]
        kvbuf[g, n_loc:] = kvc_ref[g * TOK:(g + 1) * TOK]

    def mix(probs):
        for g in range(ATTN_BATCHES):
            o_ref[g] = _mix_and_project(
                x_ref[g], yq_ref[g, :, :NA_CH], lambda p, g=g: yq_ref[g, :, _pair_lanes(p, NA_CH)],
                lambda p, g=g: kvbuf[g, :, _pair_lanes(p)],
                lambda p, g=g: kvbuf[g, :, _pair_lanes(p, NA_CH)], probs, wo_ref, mod_ref[g, 5:6])

    def probs_any(h, s):
        return _softmax_probs(jnp.concatenate([s[:, :n_loc] + bias_buf[h], s[:, n_loc:]], axis=-1))

    def probs_interior(h, s):
        width = 2 * GRID_W
        out_rows = []
        for i in range(QROWS):
            rs = slice(i * GRID_W, (i + 1) * GRID_W)
            lo, hi = (i // 2) * width, ((i + NA_KH_MAX - 1) // 2 + 1) * width
            s_i = jnp.concatenate([s[rs, lo:hi] + bias_buf[h, rs, lo:hi], s[rs, n_loc:]], axis=-1)
            p_i = _softmax_probs(s_i)
            pieces = [p_i[:, :hi - lo], p_i[:, hi - lo:]]
            if lo:
                pieces.insert(0, jnp.zeros((GRID_W, lo), BF16))
            if hi < n_loc:
                pieces.insert(-1, jnp.zeros((GRID_W, n_loc - hi), BF16))
            out_rows.append(jnp.concatenate(pieces, axis=-1))
        return jnp.concatenate(out_rows, axis=0)

    interior = (rb >= 1) & (rb < nrb - 1)
    pl.when(interior)(lambda: mix(probs_interior))
    pl.when(jnp.logical_not(interior))(lambda: mix(probs_any))


def _attention(xs, mix, rp, mod, w_out, *, layer, n_batch, rows):
    nrb = rows // QROWS
    seq = rows * GRID_W
    nt = xs.shape[0]
    g = ATTN_BATCHES
    assert nt % seq == 0 and n_batch % g == 0
    xs3 = xs.reshape(nt // seq, seq, D_MODEL)
    mix3 = mix.reshape(nt // seq, seq, mix.shape[1])
    ksb = lambda rb: jnp.clip(rb - 1, 0, nrb - NKB)

    def kv_spec(j):
        return pl.BlockSpec((g, TOK, 2 * NA_CH), lambda rb, bp: (bp, ksb(rb) + j, 1))

    in_specs = [
        pl.BlockSpec((g, TOK, D_MODEL), lambda rb, bp: (bp, rb, 0)),
        pl.BlockSpec((g, TOK, 2 * NA_CH), lambda rb, bp: (bp, rb, 0)),
        kv_spec(0), kv_spec(1), kv_spec(2),
        pl.BlockSpec((None, g * TOK, 2 * NA_CH), lambda rb, bp: (n_batch, bp, 1)),
        pl.BlockSpec((None,) + rp.shape[1:], lambda rb, bp: (layer, 0, 0, 0)),
        pl.BlockSpec((None, g, N_MOD, D_MODEL), lambda rb, bp: (layer, bp, 0, 0)),
        pl.BlockSpec((D_MODEL, D_MODEL), lambda rb, bp: (0, 0)),
    ]
    out = pl.pallas_call(
        functools.partial(_attn_kernel, nrb=nrb),
        grid=(nrb, n_batch // g),
        in_specs=in_specs,
        out_specs=pl.BlockSpec((g, TOK, D_MODEL), lambda rb, bp: (bp, rb, 0)),
        out_shape=jax.ShapeDtypeStruct(xs3.shape, F32),
        scratch_shapes=[
            pltpu.VMEM((g, NKB * TOK + TOK, 2 * NA_CH), BF16),
            pltpu.VMEM((NA_HEADS, TOK, NKB * TOK), F32),
        ],
        input_output_aliases={0: 0},
        compiler_params=pltpu.CompilerParams(
            dimension_semantics=("arbitrary", "arbitrary"), vmem_limit_bytes=VMEM_LIMIT),
        name="attention",
    )(xs3, mix3, mix3, mix3, mix3, mix3, rp, mod, w_out)
    return out.reshape(nt, D_MODEL)


def _ctx_kernel(x_ref, yq_ref, kv_ref, mod_ref, wo_ref, o_ref):
    o_ref[...] = _mix_and_project(
        x_ref[...], yq_ref[:, :NA_CH], lambda p: yq_ref[:, _pair_lanes(p, NA_CH)],
        lambda p: kv_ref[:, _pair_lanes(p)], lambda p: kv_ref[:, _pair_lanes(p, NA_CH)],
        lambda h, s: _softmax_probs(s), wo_ref, mod_ref[5:6])


def _ctx_attention(xs, mix, mod, w_out, *, layer, n_batch, n_lat_tiles):
    blk = lambda b: n_lat_tiles + b
    return pl.pallas_call(
        _ctx_kernel,
        grid=(n_batch,),
        in_specs=[
            pl.BlockSpec((TOK, D_MODEL), lambda b: (blk(b), 0)),
            pl.BlockSpec((TOK, 2 * NA_CH), lambda b: (blk(b), 0)),
            pl.BlockSpec((TOK, 2 * NA_CH), lambda b: (blk(b), 1)),
            pl.BlockSpec((None, None, N_MOD, D_MODEL), lambda b: (layer, n_batch, 0, 0)),
            pl.BlockSpec((D_MODEL, D_MODEL), lambda b: (0, 0)),
        ],
        out_specs=pl.BlockSpec((TOK, D_MODEL), lambda b: (blk(b), 0)),
        out_shape=jax.ShapeDtypeStruct(xs.shape, F32),
        input_output_aliases={0: 0},
        compiler_params=pltpu.CompilerParams(
            dimension_semantics=("arbitrary",), vmem_limit_bytes=VMEM_LIMIT),
        name="ctx_attention",
    )(xs, mix, mix, mod, w_out)


def _pad_rpb(rpb):
    depth, heads, n_off, n_col = rpb.shape
    out = jnp.full((depth, heads, n_off + 1, 2 * GRID_W), NEG_INF, F32)
    return out.at[:, :, :n_off, :n_col].set(rpb)


def _pool_block_diag(pool_w):
    depth, n = pool_w.shape[:2]
    out = jnp.zeros((depth, POOL_CH, POOL_CH), pool_w.dtype)
    for gi in range(n):
        sl = slice(gi * POOL_GROUP, (gi + 1) * POOL_GROUP)
        out = out.at[:, sl, sl].set(pool_w[:, gi])
    return out


def kernel(x, c, ctx, c_ctx, w_mod, b_mod, norm_g, ffn_w_in, ffn_w_out, w_in, conv_w, pool_w,
           pool_scale, rpb, w_out, final_g):
    n_batch, seq, d = x.shape
    ctx_len = ctx.shape[1]
    depth = w_mod.shape[0]
    rows = seq // GRID_W
    assert d == D_MODEL and ctx_len == TOK and seq % FFN_TM == 0 and rows >= KROWS
    assert (n_batch * ctx_len) % FFN_TM == 0 and rows // QROWS >= NKB + 1
    assert seq % (INPROJ_SUBTILES * TOK) == 0 and (n_batch * ctx_len) % (INPROJ_SUBTILES * TOK) == 0
    assert NA_KH_MAX // 2 == QROWS and rpb.shape[2:] == (2 * NA_KH_MAX - 1, 2 * NA_KW - 1)
    assert n_batch + 1 <= MOD_ROWS and w_in.shape[-1] == D_IN and ffn_w_in.shape[-1] == 2 * D_FF
    n_lat = n_batch * seq
    n_lat_tiles = n_lat // TOK
    tiles_per_seq = seq // TOK
    lat_ffn_tiles = n_lat // FFN_TM
    ctx_ffn_tiles = n_batch * ctx_len // FFN_TM

    cc = jnp.zeros((MOD_ROWS, d), F32).at[:n_batch].set(c).at[n_batch].set(c_ctx)
    mod = _modulation(cc, w_mod, b_mod).reshape(depth, MOD_ROWS, N_MOD, d)

    pool_bd = _pool_block_diag(pool_w).astype(BF16)
    pool_sc = pool_scale.reshape(depth, 1, POOL_CH)
    norm_g4 = norm_g.reshape(depth, 3, 1, d)
    rp = _pad_rpb(rpb)

    bf = lambda a: a.astype(BF16)
    w1, w2, w_in_b, w_out_b = bf(ffn_w_in[0, 0]), bf(ffn_w_out[0, 0]), bf(w_in[0]), bf(w_out[0])
    ffn_kw = dict(lat_tiles_per_batch=seq // FFN_TM, n_batch=n_batch)
    xa, xb, xb_off = x.reshape(n_lat, d), ctx.reshape(n_batch * ctx_len, d), 0
    for l in range(depth):
        last = l == depth - 1
        xs = _ffn(xa, xb, xb_off, lat_ffn_tiles, ctx_ffn_tiles, mod, norm_g4, w1, w2, final_g,
                  layer=l, sub=0, final=False, **ffn_kw)
        jobs = [(ffn_w_in, (l, 1)), (ffn_w_out, (l, 1))]
        if not last:
            jobs += [(ffn_w_in, (l + 1, 0)), (ffn_w_out, (l + 1, 0)), (w_in, (l + 1,)),
                     (w_out, (l + 1,))]
        mix, cast = _inproj(xs, mod, norm_g4, w_in_b, conv_w, pool_bd, pool_sc, jobs, layer=l,
                            n_lat_tiles=n_lat_tiles, lat_tiles_per_seq=tiles_per_seq,
                            n_batch=n_batch)
        xs = _attention(xs, mix, rp, mod, w_out_b, layer=l, n_batch=n_batch, rows=rows)
        if not last:
            xs = _ctx_attention(xs, mix, mod, w_out_b, layer=l, n_batch=n_batch,
                                n_lat_tiles=n_lat_tiles)
        xs = _ffn(xs, xs, lat_ffn_tiles, lat_ffn_tiles, 0 if last else ctx_ffn_tiles, mod, norm_g4,
                  cast[0], cast[1], final_g, layer=l, sub=2, final=last, **ffn_kw)
        if not last:
            w1, w2, w_in_b, w_out_b = cast[2:]
        xa, xb, xb_off = xs, xs, lat_ffn_tiles
    return xs.reshape(n_batch, seq, d)
```

```python
import functools

import jax
import jax.numpy as jnp
from jax import lax
from jax.experimental import pallas as pl
from jax.experimental.pallas import tpu as pltpu

F32 = jnp.float32
BF16 = jnp.bfloat16

D_MODEL = 1024
GRID_W = 64
D_FF = 2816
N_MOD = 9
RMS_EPS = 1e-6
NEG_INF = -1e30
CONV_CH = D_MODEL // 4
POOL_CH = D_MODEL // 4
POOL_WINDOWS = (2, 4, 8, 16)
POOL_GROUP = POOL_CH // 4
NA_HEAD_DIM = 64
NA_CH = D_MODEL // 2
NA_HEADS = NA_CH // NA_HEAD_DIM
NA_KH_MAX = 8
NA_KW = 16
OFF_Q = 3 * CONV_CH + POOL_CH
D_IN = OFF_Q + 3 * NA_CH

FFN_TM = 1024
FFN_CHUNKS = ((0, 1536), (1536, 2816))
FFN_SUBTILES = 4
CAST_BLOCKS = 16
BF16_SUBLANES = 16
TOK = 256
HALO = 8
INPROJ_SUBTILES = 4
QROWS = TOK // GRID_W
KROWS = QROWS + NA_KH_MAX
NKB = KROWS * GRID_W // TOK
SCORE_LOOKAHEAD = 1
ATTN_BATCHES = 2
MOD_ROWS = 16
MOD_TN = 1152
VMEM_LIMIT = 56 * 1024 * 1024


def _rms_mod(x, g, scale, shift):
    ms = jnp.mean(x * x, axis=-1, keepdims=True)
    return (x * lax.rsqrt(ms + RMS_EPS) * g) * (1.0 + scale) + shift


def _mod_kernel(c_ref, w_ref, b_ref, *rest):
    n_cast = (len(rest) - 1) // 2
    o_ref = rest[n_cast]
    _run_cast_jobs(rest[:n_cast], rest[n_cast + 1:])
    c = c_ref[...]
    s = (c * jax.nn.sigmoid(c)).astype(BF16)
    o_ref[...] = jnp.dot(s, w_ref[...].astype(BF16), preferred_element_type=F32) + b_ref[...]


def _modulation(cc, w_mod, b_mod, cast_jobs):
    depth = w_mod.shape[0]
    n = N_MOD * D_MODEL
    per_layer = n // MOD_TN
    cast_in_specs, cast_out_specs, cast_out_shapes = _cast_job_specs(cast_jobs, depth * per_layer)
    outs = pl.pallas_call(
        _mod_kernel,
        grid=(depth * per_layer,),
        in_specs=[
            pl.BlockSpec((MOD_ROWS, D_MODEL), lambda i: (0, 0)),
            pl.BlockSpec((None, D_MODEL, MOD_TN), lambda i: (i // per_layer, 0, i % per_layer)),
            pl.BlockSpec((None, 1, MOD_TN), lambda i: (i // per_layer, 0, i % per_layer)),
        ] + cast_in_specs,
        out_specs=[pl.BlockSpec((None, MOD_ROWS, MOD_TN),
                                lambda i: (i // per_layer, 0, i % per_layer))] + cast_out_specs,
        out_shape=[jax.ShapeDtypeStruct((depth, MOD_ROWS, n), F32)] + cast_out_shapes,
        compiler_params=pltpu.CompilerParams(
            dimension_semantics=("arbitrary",), vmem_limit_bytes=VMEM_LIMIT),
        name="modulation",
    )(cc, w_mod, b_mod.reshape(depth, 1, n), *[src for src, _ in cast_jobs])
    return outs[0], outs[1:]


def _cast_job_specs(cast_jobs, n_steps):
    assert n_steps >= CAST_BLOCKS
    cast_blk = lambda i: jnp.minimum(i, CAST_BLOCKS - 1)
    in_specs, out_specs, out_shapes = [], [], []
    for src, lead in cast_jobs:
        rows, cols = src.shape[-2:]
        assert rows % (CAST_BLOCKS * BF16_SUBLANES) == 0 and len(lead) == src.ndim - 2
        blk = (rows // CAST_BLOCKS, cols)
        in_specs.append(pl.BlockSpec((None,) * len(lead) + blk,
                                     lambda i, lead=lead: lead + (cast_blk(i), 0)))
        out_specs.append(pl.BlockSpec(blk, lambda i: (cast_blk(i), 0)))
        out_shapes.append(jax.ShapeDtypeStruct((rows, cols), BF16))
    return in_specs, out_specs, out_shapes


def _run_cast_jobs(src_refs, dst_refs):
    for src_ref, dst_ref in zip(src_refs, dst_refs):
        dst_ref[...] = src_ref[...].astype(BF16)


def _ffn_kernel(xa_ref, xb_ref, mod_ref, g_ref, w1_ref, w2_ref, fg_ref, o_ref, *, sub, final,
                n_a_tiles, n_b_tiles):
    m = mod_ref[...]
    shift = m[3 * sub:3 * sub + 1]
    scale = m[3 * sub + 1:3 * sub + 2]
    gate = m[3 * sub + 2:3 * sub + 3]
    rows = FFN_TM // FFN_SUBTILES

    def load_x(r):
        sl = slice(r * rows, (r + 1) * rows)
        if n_b_tiles:
            return jnp.where(pl.program_id(0) < n_a_tiles, xa_ref[sl], xb_ref[sl])
        return xa_ref[sl]

    units = [(r, c) for r in range(FFN_SUBTILES) for c in range(len(FFN_CHUNKS))]
    xs, hbs, accs = {}, {}, {}

    def up(r, c):
        if r not in xs:
            xs[r] = load_x(r)
            hbs[r] = _rms_mod(xs[r], g_ref[...], scale, shift).astype(BF16)
        lo, hi = FFN_CHUNKS[c]
        a = jnp.dot(hbs[r], w1_ref[:, lo:hi], preferred_element_type=F32)
        b = jnp.dot(hbs[r], w1_ref[:, D_FF + lo:D_FF + hi], preferred_element_type=F32)
        return a, b

    ab = up(*units[0])
    for i, (r, c) in enumerate(units):
        a, b = ab
        if i + 1 < len(units):
            ab = up(*units[i + 1])
        lo, hi = FFN_CHUNKS[c]
        gg = (a * jax.nn.sigmoid(a) * b).astype(BF16)
        part = jnp.dot(gg, w2_ref[lo:hi, :], preferred_element_type=F32)
        accs[r] = part if r not in accs else accs[r] + part
    for r in range(FFN_SUBTILES):
        y = xs[r] + (0.5 * gate) * accs[r]
        if final:
            ms = jnp.mean(y * y, axis=-1, keepdims=True)
            y = y * lax.rsqrt(ms + RMS_EPS) * fg_ref[...]
        o_ref[r * rows:(r + 1) * rows] = y


def _ffn(xa, xb, xb_off, n_a_tiles, n_b_tiles, mod, norm_g4, w1, w2, final_g, *, layer, sub, final,
         lat_tiles_per_batch, n_batch):
    n_tiles = n_a_tiles + n_b_tiles
    const = lambda i: (0, 0)
    kern = functools.partial(_ffn_kernel, sub=sub, final=final, n_a_tiles=n_a_tiles,
                             n_b_tiles=n_b_tiles)
    return pl.pallas_call(
        kern,
        grid=(n_tiles,),
        in_specs=[
            pl.BlockSpec((FFN_TM, D_MODEL), lambda i: (jnp.minimum(i, n_a_tiles - 1), 0)),
            pl.BlockSpec((FFN_TM, D_MODEL), lambda i: (jnp.maximum(i - n_a_tiles, 0) + xb_off, 0)),
            pl.BlockSpec((None, None, N_MOD, D_MODEL),
                         lambda i: (layer, jnp.minimum(i // lat_tiles_per_batch, n_batch), 0, 0)),
            pl.BlockSpec((None, None, 1, D_MODEL), lambda i: (layer, sub, 0, 0)),
            pl.BlockSpec((D_MODEL, 2 * D_FF), const, pipeline_mode=pl.Buffered(1)),
            pl.BlockSpec((D_FF, D_MODEL), const, pipeline_mode=pl.Buffered(1)),
            pl.BlockSpec((1, D_MODEL), const),
        ],
        out_specs=pl.BlockSpec((FFN_TM, D_MODEL), lambda i: (i, 0)),
        out_shape=jax.ShapeDtypeStruct((n_tiles * FFN_TM, D_MODEL), F32),
        compiler_params=pltpu.CompilerParams(
            dimension_semantics=("arbitrary",), vmem_limit_bytes=VMEM_LIMIT),
        name="ffn",
    )(xa, xb, mod, norm_g4, w1, w2, final_g.reshape(1, D_MODEL))


def _inproj_kernel(xp_ref, x_ref, xn_ref, mod_ref, g_ref, w_ref, cw_ref, pw_ref, ps_ref, *rest,
                   n_lat_tiles, lat_tiles_per_seq):
    n_cast = (len(rest) - 1) // 2
    o_ref = rest[n_cast]
    _run_cast_jobs(rest[:n_cast], rest[n_cast + 1:])
    m = mod_ref[...]
    shift, scale = m[3:4], m[4:5]
    g = g_ref[...]
    ext = TOK + 2 * HALO

    def project(sub):
        lo = sub * TOK
        xp = xp_ref[...] if sub == 0 else x_ref[lo - HALO:lo]
        xn = xn_ref[...] if sub == INPROJ_SUBTILES - 1 else x_ref[lo + TOK:lo + TOK + HALO]
        xe = jnp.concatenate([xp, x_ref[lo:lo + TOK], xn], axis=0)
        he = _rms_mod(xe, g, scale, shift).astype(BF16)
        u = jnp.dot(he, w_ref[:, :OFF_Q], preferred_element_type=F32)
        qkv = jnp.dot(he[HALO:HALO + TOK], w_ref[:, OFF_Q:], preferred_element_type=F32)
        o_ref[lo:lo + TOK, 2 * CONV_CH:2 * CONV_CH + NA_CH] = (
            qkv[:, :NA_CH] * (NA_HEAD_DIM ** -0.5)).astype(BF16)
        o_ref[lo:lo + TOK, 2 * CONV_CH + NA_CH:] = qkv[:, NA_CH:].astype(BF16)
        return u

    def mix(sub, u):
        t = pl.program_id(0) * INPROJ_SUBTILES + sub
        is_lat = t < n_lat_tiles
        pos = jnp.where(is_lat, t % lat_tiles_per_seq, 0)
        ntile = jnp.where(is_lat, lat_tiles_per_seq, 1)
        t0 = pos * TOK
        seq_len = ntile * TOK

        row = lax.broadcasted_iota(jnp.int32, (ext, 1), 0)
        keep_lo = jnp.where(pos == 0, HALO, 0)
        keep_hi = jnp.where(pos == ntile - 1, HALO + TOK, ext)
        u = jnp.where((row >= keep_lo) & (row < keep_hi), u, 0.0)

        def shifted(a, s):
            return pltpu.roll(a, s % ext, 0)

        def centre(a):
            return a[HALO:HALO + TOK]

        z = u[:, 2 * CONV_CH:3 * CONV_CH] * u[:, :CONV_CH]
        cw = cw_ref[...]
        conv = (centre(shifted(z, 1)) * cw[0:1] + centre(z) * cw[1:2]
                + centre(shifted(z, -1)) * cw[2:3])
        y_conv = centre(u[:, CONV_CH:2 * CONV_CH]) * conv

        v = u[:, 3 * CONV_CH:]
        a1 = v + shifted(v, 1)
        a2 = a1 + shifted(a1, 2)
        a3 = a2 + shifted(a2, 4)
        a4 = a3 + shifted(a3, 8)
        sums = (centre(a1), centre(shifted(a2, -1)), centre(shifted(a3, -3)),
                centre(shifted(a4, -7)))
        tpos = t0 + lax.broadcasted_iota(jnp.int32, (TOK, 1), 0)
        lane = lax.broadcasted_iota(jnp.int32, (TOK, POOL_CH), 1)
        mean = None
        for gi in reversed(range(len(POOL_WINDOWS))):
            w = POOL_WINDOWS[gi]
            left = w // 2
            right = w - 1 - left
            cnt = jnp.minimum(tpos + right + 1, seq_len) - jnp.maximum(tpos - left, 0)
            mg = sums[gi] / cnt.astype(F32)
            mean = mg if mean is None else jnp.where(lane < (gi + 1) * POOL_GROUP, mg, mean)
        dpool = (mean - centre(v)).astype(BF16)
        y_pool = jnp.dot(dpool, pw_ref[...], preferred_element_type=F32) * ps_ref[...]
        lo = sub * TOK
        o_ref[lo:lo + TOK, :CONV_CH] = y_conv.astype(BF16)
        o_ref[lo:lo + TOK, CONV_CH:CONV_CH + POOL_CH] = y_pool.astype(BF16)

    us = [project(sub) for sub in range(INPROJ_SUBTILES)]
    for sub, u in enumerate(us):
        mix(sub, u)


def _inproj(xs, mod, norm_g4, w_in, conv_w, pool_bd, pool_scale, cast_jobs, *, layer,
            n_lat_tiles, lat_tiles_per_seq, n_batch):
    nt = xs.shape[0]
    tm = INPROJ_SUBTILES * TOK
    n_steps = nt // tm
    hb = tm // HALO
    steps_per_seq = lat_tiles_per_seq // INPROJ_SUBTILES
    per_layer = lambda i: (layer, 0, 0)
    kern = functools.partial(_inproj_kernel, n_lat_tiles=n_lat_tiles,
                             lat_tiles_per_seq=lat_tiles_per_seq)
    cast_in_specs, cast_out_specs, cast_out_shapes = _cast_job_specs(cast_jobs, n_steps)
    outs = pl.pallas_call(
        kern,
        grid=(n_steps,),
        in_specs=[
            pl.BlockSpec((HALO, D_MODEL), lambda i: (jnp.maximum(i * hb - 1, 0), 0)),
            pl.BlockSpec((tm, D_MODEL), lambda i: (i, 0)),
            pl.BlockSpec((HALO, D_MODEL), lambda i: (jnp.minimum((i + 1) * hb, nt // HALO - 1), 0)),
            pl.BlockSpec((None, None, N_MOD, D_MODEL),
                         lambda i: (layer, jnp.minimum(i // steps_per_seq, n_batch), 0, 0)),
            pl.BlockSpec((None, None, 1, D_MODEL), lambda i: (layer, 1, 0, 0)),
            pl.BlockSpec((D_MODEL, D_IN), lambda i: (0, 0), pipeline_mode=pl.Buffered(1)),
            pl.BlockSpec((None, 3, CONV_CH), per_layer),
            pl.BlockSpec((None, POOL_CH, POOL_CH), per_layer),
            pl.BlockSpec((None, 1, POOL_CH), per_layer),
        ] + cast_in_specs,
        out_specs=[pl.BlockSpec((tm, 2 * D_MODEL), lambda i: (i, 0))] + cast_out_specs,
        out_shape=[jax.ShapeDtypeStruct((nt, 2 * D_MODEL), BF16)] + cast_out_shapes,
        compiler_params=pltpu.CompilerParams(
            dimension_semantics=("arbitrary",), vmem_limit_bytes=VMEM_LIMIT),
        name="inproj",
    )(xs, xs, xs, mod, norm_g4, w_in, conv_w, pool_bd, pool_scale, *[src for src, _ in cast_jobs])
    return outs[0], outs[1:]


def _pair_lanes(p, offset=0):
    return slice(offset + 2 * p * NA_HEAD_DIM, offset + 2 * (p + 1) * NA_HEAD_DIM)


def _softmax_probs(s):
    mx = jnp.max(s, axis=-1, keepdims=True)
    return jnp.exp(s - mx).astype(BF16)


def _mix_and_project(x, ycp, q_of, k_of, v_of, probs, wo_ref, gate):
    lane = lax.broadcasted_iota(jnp.int32, (1, 2 * NA_HEAD_DIM), 1)
    own = [jnp.where(lane < NA_HEAD_DIM, 1.0, 0.0).astype(BF16),
           jnp.where(lane < NA_HEAD_DIM, 0.0, 1.0).astype(BF16)]

    def scores(t):
        p, half = divmod(t, 2)
        return lax.dot_general(q_of(p) * own[half], k_of(p), (((1,), (1,)), ((), ())),
                               preferred_element_type=F32)

    def normalised(outs, first_pair, n_pairs):
        normed = []
        for p in range(first_pair, first_pair + n_pairs):
            o0, o1 = outs[2 * p], outs[2 * p + 1]
            lo = lax.broadcasted_iota(jnp.int32, o0.shape, 1) < NA_HEAD_DIM
            num = jnp.where(lo, o0, o1)
            den = jnp.where(lo, pltpu.roll(o0, NA_HEAD_DIM, 1), pltpu.roll(o1, NA_HEAD_DIM, 1))
            normed.append((num / den).astype(BF16))
        return normed

    outs = []
    pending = [scores(t) for t in range(SCORE_LOOKAHEAD)]
    for t in range(NA_HEADS):
        s = pending.pop(0)
        if t + SCORE_LOOKAHEAD < NA_HEADS:
            pending.append(scores(t + SCORE_LOOKAHEAD))
        vm = v_of(t // 2) * own[t % 2] + own[1 - t % 2]
        outs.append(jnp.dot(probs(t, s), vm, preferred_element_type=F32))
    y = jnp.concatenate([ycp] + normalised(outs, 0, NA_HEADS // 2), axis=-1)
    return x + gate * jnp.dot(y, wo_ref[...], preferred_element_type=F32)


def _fill_bias(bias_buf, rp_ref, rb, nrb):
    width = 2 * GRID_W
    lane = lax.broadcasted_iota(jnp.int32, (GRID_W, width), 1)
    qc = lax.broadcasted_iota(jnp.int32, (GRID_W, width), 0)
    kc = lane & (GRID_W - 1)
    cstart = jnp.clip(qc - NA_KW // 2, 0, GRID_W - NA_KW)
    col_ok = (kc >= cstart) & (kc < cstart + NA_KW)
    lo_half = lane < GRID_W
    is_first = rb == 0
    is_last = rb == nrb - 1
    base = jnp.where(is_first, 0, jnp.where(is_last, -2 * QROWS, -QROWS))
    shift0 = (width - (NA_KW - 1)) % width
    shift1 = (shift0 + GRID_W) % width

    def per_head(h, carry):
        for i in range(QROWS):
            jlo = jnp.where(is_first, 0, jnp.where(is_last, QROWS, i))
            for jj in range(KROWS // 2):
                halves = []
                for j, sh in ((2 * jj, shift0), (2 * jj + 1, shift1)):
                    ok = (j >= jlo) & (j < jlo + NA_KH_MAX)
                    idx = jnp.clip(j - i + base + NA_KH_MAX - 1, 0, 2 * NA_KH_MAX - 2)
                    row = jnp.where(ok, rp_ref[h, pl.ds(idx, 1), :], NEG_INF)
                    halves.append(pltpu.roll(jnp.broadcast_to(row, (GRID_W, width)), sh, 1,
                                             stride=1, stride_axis=0))
                tile = jnp.where(col_ok, jnp.where(lo_half, halves[0], halves[1]), NEG_INF)
                bias_buf[h, i * GRID_W:(i + 1) * GRID_W, jj * width:(jj + 1) * width] = tile
        return carry

    lax.fori_loop(0, NA_HEADS, per_head, 0)


def _attn_kernel(x_ref, yq_ref, kv0_ref, kv1_ref, kv2_ref, kvc_ref, rp_ref, mod_ref, wo_ref, o_ref,
                 kvbuf, bias_buf, *, nrb):
    rb = pl.program_id(0)
    n_loc = NKB * TOK

    @pl.when((pl.program_id(1) == 0) & ((rb <= 1) | (rb == nrb - 1)))
    def _():
        _fill_bias(bias_buf, rp_ref, rb, nrb)

    for g in range(ATTN_BATCHES):
        for j, kv_ref in enumerate((kv0_ref, kv1_ref, kv2_ref)):
            kvbuf[g, j * TOK:(j + 1) * TOK] = kv_ref[g]
        kvbuf[g, n_loc:] = kvc_ref[g * TOK:(g + 1) * TOK]

    def mix(probs):
        for g in range(ATTN_BATCHES):
            o_ref[g] = _mix_and_project(
                x_ref[g], yq_ref[g, :, :NA_CH], lambda p, g=g: yq_ref[g, :, _pair_lanes(p, NA_CH)],
                lambda p, g=g: kvbuf[g, :, _pair_lanes(p)],
                lambda p, g=g: kvbuf[g, :, _pair_lanes(p, NA_CH)], probs, wo_ref, mod_ref[g, 5:6])

    def probs_any(h, s):
        return _softmax_probs(jnp.concatenate([s[:, :n_loc] + bias_buf[h], s[:, n_loc:]], axis=-1))

    def probs_interior(h, s):
        width = 2 * GRID_W
        out_rows = []
        for i in range(QROWS):
            rs = slice(i * GRID_W, (i + 1) * GRID_W)
            lo, hi = (i // 2) * width, ((i + NA_KH_MAX - 1) // 2 + 1) * width
            s_i = jnp.concatenate([s[rs, lo:hi] + bias_buf[h, rs, lo:hi], s[rs, n_loc:]], axis=-1)
            p_i = _softmax_probs(s_i)
            pieces = [p_i[:, :hi - lo], p_i[:, hi - lo:]]
            if lo:
                pieces.insert(0, jnp.zeros((GRID_W, lo), BF16))
            if hi < n_loc:
                pieces.insert(-1, jnp.zeros((GRID_W, n_loc - hi), BF16))
            out_rows.append(jnp.concatenate(pieces, axis=-1))
        return jnp.concatenate(out_rows, axis=0)

    interior = (rb >= 1) & (rb < nrb - 1)
    pl.when(interior)(lambda: mix(probs_interior))
    pl.when(jnp.logical_not(interior))(lambda: mix(probs_any))


def _attention(xs, mix, rp, mod, w_out, *, layer, n_batch, rows):
    nrb = rows // QROWS
    seq = rows * GRID_W
    nt = xs.shape[0]
    g = ATTN_BATCHES
    assert nt % seq == 0 and n_batch % g == 0
    xs3 = xs.reshape(nt // seq, seq, D_MODEL)
    mix3 = mix.reshape(nt // seq, seq, mix.shape[1])
    ksb = lambda rb: jnp.clip(rb - 1, 0, nrb - NKB)

    def kv_spec(j):
        return pl.BlockSpec((g, TOK, 2 * NA_CH), lambda rb, bp: (bp, ksb(rb) + j, 1))

    in_specs = [
        pl.BlockSpec((g, TOK, D_MODEL), lambda rb, bp: (bp, rb, 0)),
        pl.BlockSpec((g, TOK, 2 * NA_CH), lambda rb, bp: (bp, rb, 0)),
        kv_spec(0), kv_spec(1), kv_spec(2),
        pl.BlockSpec((None, g * TOK, 2 * NA_CH), lambda rb, bp: (n_batch, bp, 1)),
        pl.BlockSpec((None,) + rp.shape[1:], lambda rb, bp: (layer, 0, 0, 0)),
        pl.BlockSpec((None, g, N_MOD, D_MODEL), lambda rb, bp: (layer, bp, 0, 0)),
        pl.BlockSpec((D_MODEL, D_MODEL), lambda rb, bp: (0, 0)),
    ]
    out = pl.pallas_call(
        functools.partial(_attn_kernel, nrb=nrb),
        grid=(nrb, n_batch // g),
        in_specs=in_specs,
        out_specs=pl.BlockSpec((g, TOK, D_MODEL), lambda rb, bp: (bp, rb, 0)),
        out_shape=jax.ShapeDtypeStruct(xs3.shape, F32),
        scratch_shapes=[
            pltpu.VMEM((g, NKB * TOK + TOK, 2 * NA_CH), BF16),
            pltpu.VMEM((NA_HEADS, TOK, NKB * TOK), F32),
        ],
        input_output_aliases={0: 0},
        compiler_params=pltpu.CompilerParams(
            dimension_semantics=("arbitrary", "arbitrary"), vmem_limit_bytes=VMEM_LIMIT),
        name="attention",
    )(xs3, mix3, mix3, mix3, mix3, mix3, rp, mod, w_out)
    return out.reshape(nt, D_MODEL)


def _ctx_kernel(x_ref, yq_ref, kv_ref, mod_ref, wo_ref, o_ref):
    o_ref[...] = _mix_and_project(
        x_ref[...], yq_ref[:, :NA_CH], lambda p: yq_ref[:, _pair_lanes(p, NA_CH)],
        lambda p: kv_ref[:, _pair_lanes(p)], lambda p: kv_ref[:, _pair_lanes(p, NA_CH)],
        lambda h, s: _softmax_probs(s), wo_ref, mod_ref[5:6])


def _ctx_attention(xs, mix, mod, w_out, *, layer, n_batch, n_lat_tiles):
    blk = lambda b: n_lat_tiles + b
    return pl.pallas_call(
        _ctx_kernel,
        grid=(n_batch,),
        in_specs=[
            pl.BlockSpec((TOK, D_MODEL), lambda b: (blk(b), 0)),
            pl.BlockSpec((TOK, 2 * NA_CH), lambda b: (blk(b), 0)),
            pl.BlockSpec((TOK, 2 * NA_CH), lambda b: (blk(b), 1)),
            pl.BlockSpec((None, None, N_MOD, D_MODEL), lambda b: (layer, n_batch, 0, 0)),
            pl.BlockSpec((D_MODEL, D_MODEL), lambda b: (0, 0)),
        ],
        out_specs=pl.BlockSpec((TOK, D_MODEL), lambda b: (blk(b), 0)),
        out_shape=jax.ShapeDtypeStruct(xs.shape, F32),
        input_output_aliases={0: 0},
        compiler_params=pltpu.CompilerParams(
            dimension_semantics=("arbitrary",), vmem_limit_bytes=VMEM_LIMIT),
        name="ctx_attention",
    )(xs, mix, mix, mod, w_out)


def _pad_rpb(rpb):
    depth, heads, n_off, n_col = rpb.shape
    out = jnp.full((depth, heads, n_off + 1, 2 * GRID_W), NEG_INF, F32)
    return out.at[:, :, :n_off, :n_col].set(rpb)


def _pool_block_diag(pool_w):
    depth, n = pool_w.shape[:2]
    out = jnp.zeros((depth, POOL_CH, POOL_CH), pool_w.dtype)
    for gi in range(n):
        sl = slice(gi * POOL_GROUP, (gi + 1) * POOL_GROUP)
        out = out.at[:, sl, sl].set(pool_w[:, gi])
    return out


def kernel(x, c, ctx, c_ctx, w_mod, b_mod, norm_g, ffn_w_in, ffn_w_out, w_in, conv_w, pool_w,
           pool_scale, rpb, w_out, final_g):
    n_batch, seq, d = x.shape
    ctx_len = ctx.shape[1]
    depth = w_mod.shape[0]
    rows = seq // GRID_W
    assert d == D_MODEL and ctx_len == TOK and seq % FFN_TM == 0 and rows >= KROWS
    assert (n_batch * ctx_len) % FFN_TM == 0 and rows // QROWS >= NKB + 1
    assert seq % (INPROJ_SUBTILES * TOK) == 0 and (n_batch * ctx_len) % (INPROJ_SUBTILES * TOK) == 0
    assert NA_KH_MAX // 2 == QROWS and rpb.shape[2:] == (2 * NA_KH_MAX - 1, 2 * NA_KW - 1)
    assert n_batch + 1 <= MOD_ROWS and w_in.shape[-1] == D_IN and ffn_w_in.shape[-1] == 2 * D_FF
    n_lat = n_batch * seq
    n_lat_tiles = n_lat // TOK
    tiles_per_seq = seq // TOK
    lat_ffn_tiles = n_lat // FFN_TM
    ctx_ffn_tiles = n_batch * ctx_len // FFN_TM

    cc = jnp.zeros((MOD_ROWS, d), F32).at[:n_batch].set(c).at[n_batch].set(c_ctx)
    mod, (w1, w2, w_in_b, w_out_b) = _modulation(
        cc, w_mod, b_mod, [(ffn_w_in, (0, 0)), (ffn_w_out, (0, 0)), (w_in, (0,)), (w_out, (0,))])
    mod = mod.reshape(depth, MOD_ROWS, N_MOD, d)

    pool_bd = _pool_block_diag(pool_w).astype(BF16)
    pool_sc = pool_scale.reshape(depth, 1, POOL_CH)
    norm_g4 = norm_g.reshape(depth, 3, 1, d)
    rp = _pad_rpb(rpb)

    ffn_kw = dict(lat_tiles_per_batch=seq // FFN_TM, n_batch=n_batch)
    xa, xb, xb_off = x.reshape(n_lat, d), ctx.reshape(n_batch * ctx_len, d), 0
    for l in range(depth):
        last = l == depth - 1
        xs = _ffn(xa, xb, xb_off, lat_ffn_tiles, ctx_ffn_tiles, mod, norm_g4, w1, w2, final_g,
                  layer=l, sub=0, final=False, **ffn_kw)
        jobs = [(ffn_w_in, (l, 1)), (ffn_w_out, (l, 1))]
        if not last:
            jobs += [(ffn_w_in, (l + 1, 0)), (ffn_w_out, (l + 1, 0)), (w_in, (l + 1,)),
                     (w_out, (l + 1,))]
        mix, cast = _inproj(xs, mod, norm_g4, w_in_b, conv_w, pool_bd, pool_sc, jobs, layer=l,
                            n_lat_tiles=n_lat_tiles, lat_tiles_per_seq=tiles_per_seq,
                            n_batch=n_batch)
        xs = _attention(xs, mix, rp, mod, w_out_b, layer=l, n_batch=n_batch, rows=rows)
        if not last:
            xs = _ctx_attention(xs, mix, mod, w_out_b, layer=l, n_batch=n_batch,
                                n_lat_tiles=n_lat_tiles)
        xs = _ffn(xs, xs, lat_ffn_tiles, lat_ffn_tiles, 0 if last else ctx_ffn_tiles, mod, norm_g4,
                  cast[0], cast[1], final_g, layer=l, sub=2, final=last, **ffn_kw)
        if not last:
            w1, w2, w_in_b, w_out_b = cast[2:]
        xa, xb, xb_off = xs, xs, lat_ffn_tiles
    return xs.reshape(n_batch, seq, d)
```

```python
import functools

import jax
import jax.numpy as jnp
from jax import lax
from jax.experimental import pallas as pl
from jax.experimental.pallas import tpu as pltpu

F32 = jnp.float32
BF16 = jnp.bfloat16

D_MODEL = 1024
GRID_W = 64
D_FF = 2816
N_MOD = 9
RMS_EPS = 1e-6
NEG_INF = -1e30
CONV_CH = D_MODEL // 4
POOL_CH = D_MODEL // 4
POOL_WINDOWS = (2, 4, 8, 16)
POOL_GROUP = POOL_CH // 4
NA_HEAD_DIM = 64
NA_CH = D_MODEL // 2
NA_HEADS = NA_CH // NA_HEAD_DIM
NA_KH_MAX = 8
NA_KW = 16
OFF_Q = 3 * CONV_CH + POOL_CH
D_IN = OFF_Q + 3 * NA_CH

FFN_TM = 1024
FFN_CHUNKS = ((0, 1536), (1536, 2816))
FFN_SUBTILES = 4
CAST_BLOCKS = 16
BF16_SUBLANES = 16
TOK = 256
HALO = 8
INPROJ_SUBTILES = 4
QROWS = TOK // GRID_W
KROWS = QROWS + NA_KH_MAX
NKB = KROWS * GRID_W // TOK
SCORE_LOOKAHEAD = 1
ATTN_BATCHES = 2
MOD_ROWS = 16
MOD_TN = 1152
VMEM_LIMIT = 56 * 1024 * 1024


def _rms_mod(x, g, scale, shift):
    ms = jnp.mean(x * x, axis=-1, keepdims=True)
    return (x * lax.rsqrt(ms + RMS_EPS) * g) * (1.0 + scale) + shift


def _mod_kernel(c_ref, w_ref, b_ref, *rest):
    n_cast = (len(rest) - 1) // 2
    o_ref = rest[n_cast]
    _run_cast_jobs(rest[:n_cast], rest[n_cast + 1:])
    c = c_ref[...]
    s = (c * jax.nn.sigmoid(c)).astype(BF16)
    o_ref[...] = jnp.dot(s, w_ref[...].astype(BF16), preferred_element_type=F32) + b_ref[...]


def _modulation(cc, w_mod, b_mod, cast_jobs):
    depth = w_mod.shape[0]
    n = N_MOD * D_MODEL
    per_layer = n // MOD_TN
    cast_in_specs, cast_out_specs, cast_out_shapes = _cast_job_specs(cast_jobs, depth * per_layer)
    outs = pl.pallas_call(
        _mod_kernel,
        grid=(depth * per_layer,),
        in_specs=[
            pl.BlockSpec((MOD_ROWS, D_MODEL), lambda i: (0, 0)),
            pl.BlockSpec((None, D_MODEL, MOD_TN), lambda i: (i // per_layer, 0, i % per_layer)),
            pl.BlockSpec((None, 1, MOD_TN), lambda i: (i // per_layer, 0, i % per_layer)),
        ] + cast_in_specs,
        out_specs=[pl.BlockSpec((None, MOD_ROWS, MOD_TN),
                                lambda i: (i // per_layer, 0, i % per_layer))] + cast_out_specs,
        out_shape=[jax.ShapeDtypeStruct((depth, MOD_ROWS, n), F32)] + cast_out_shapes,
        compiler_params=pltpu.CompilerParams(
            dimension_semantics=("arbitrary",), vmem_limit_bytes=VMEM_LIMIT),
        name="modulation",
    )(cc, w_mod, b_mod.reshape(depth, 1, n), *[src for src, _ in cast_jobs])
    return outs[0], outs[1:]


def _cast_job_specs(cast_jobs, n_steps):
    assert n_steps >= CAST_BLOCKS
    cast_blk = lambda i: jnp.minimum(i, CAST_BLOCKS - 1)
    in_specs, out_specs, out_shapes = [], [], []
    for src, lead in cast_jobs:
        rows, cols = src.shape[-2:]
        assert rows % (CAST_BLOCKS * BF16_SUBLANES) == 0 and len(lead) == src.ndim - 2
        blk = (rows // CAST_BLOCKS, cols)
        in_specs.append(pl.BlockSpec((None,) * len(lead) + blk,
                                     lambda i, lead=lead: lead + (cast_blk(i), 0)))
        out_specs.append(pl.BlockSpec(blk, lambda i: (cast_blk(i), 0)))
        out_shapes.append(jax.ShapeDtypeStruct((rows, cols), BF16))
    return in_specs, out_specs, out_shapes


def _run_cast_jobs(src_refs, dst_refs):
    for src_ref, dst_ref in zip(src_refs, dst_refs):
        dst_ref[...] = src_ref[...].astype(BF16)


def _ffn_kernel(xa_ref, xb_ref, mod_ref, g_ref, w1_ref, w2_ref, fg_ref, o_ref, *, sub, final,
                n_a_tiles, n_b_tiles):
    m = mod_ref[...]
    shift = m[3 * sub:3 * sub + 1]
    scale = m[3 * sub + 1:3 * sub + 2]
    gate = m[3 * sub + 2:3 * sub + 3]
    rows = FFN_TM // FFN_SUBTILES

    def load_x(r):
        sl = slice(r * rows, (r + 1) * rows)
        if n_b_tiles:
            return jnp.where(pl.program_id(0) < n_a_tiles, xa_ref[sl], xb_ref[sl])
        return xa_ref[sl]

    units = [(r, c) for r in range(FFN_SUBTILES) for c in range(len(FFN_CHUNKS))]
    xs, hbs, accs = {}, {}, {}

    def up(r, c):
        if r not in xs:
            xs[r] = load_x(r)
            hbs[r] = _rms_mod(xs[r], g_ref[...], scale, shift).astype(BF16)
        lo, hi = FFN_CHUNKS[c]
        a = jnp.dot(hbs[r], w1_ref[:, lo:hi], preferred_element_type=F32)
        b = jnp.dot(hbs[r], w1_ref[:, D_FF + lo:D_FF + hi], preferred_element_type=F32)
        return a, b

    ab = up(*units[0])
    for i, (r, c) in enumerate(units):
        a, b = ab
        if i + 1 < len(units):
            ab = up(*units[i + 1])
        lo, hi = FFN_CHUNKS[c]
        gg = (a * jax.nn.sigmoid(a) * b).astype(BF16)
        part = jnp.dot(gg, w2_ref[lo:hi, :], preferred_element_type=F32)
        accs[r] = part if r not in accs else accs[r] + part
    for r in range(FFN_SUBTILES):
        y = xs[r] + (0.5 * gate) * accs[r]
        if final:
            ms = jnp.mean(y * y, axis=-1, keepdims=True)
            y = y * lax.rsqrt(ms + RMS_EPS) * fg_ref[...]
        o_ref[r * rows:(r + 1) * rows] = y


def _ffn(xa, xb, xb_off, n_a_tiles, n_b_tiles, mod, norm_g4, w1, w2, final_g, *, layer, sub, final,
         lat_tiles_per_batch, n_batch):
    n_tiles = n_a_tiles + n_b_tiles
    const = lambda i: (0, 0)
    kern = functools.partial(_ffn_kernel, sub=sub, final=final, n_a_tiles=n_a_tiles,
                             n_b_tiles=n_b_tiles)
    return pl.pallas_call(
        kern,
        grid=(n_tiles,),
        in_specs=[
            pl.BlockSpec((FFN_TM, D_MODEL), lambda i: (jnp.minimum(i, n_a_tiles - 1), 0)),
            pl.BlockSpec((FFN_TM, D_MODEL), lambda i: (jnp.maximum(i - n_a_tiles, 0) + xb_off, 0)),
            pl.BlockSpec((None, None, N_MOD, D_MODEL),
                         lambda i: (layer, jnp.minimum(i // lat_tiles_per_batch, n_batch), 0, 0)),
            pl.BlockSpec((None, None, 1, D_MODEL), lambda i: (layer, sub, 0, 0)),
            pl.BlockSpec((D_MODEL, 2 * D_FF), const, pipeline_mode=pl.Buffered(1)),
            pl.BlockSpec((D_FF, D_MODEL), const, pipeline_mode=pl.Buffered(1)),
            pl.BlockSpec((1, D_MODEL), const),
        ],
        out_specs=pl.BlockSpec((FFN_TM, D_MODEL), lambda i: (i, 0)),
        out_shape=jax.ShapeDtypeStruct((n_tiles * FFN_TM, D_MODEL), F32),
        compiler_params=pltpu.CompilerParams(
            dimension_semantics=("arbitrary",), vmem_limit_bytes=VMEM_LIMIT),
        name="ffn",
    )(xa, xb, mod, norm_g4, w1, w2, final_g.reshape(1, D_MODEL))


def _inproj_kernel(xp_ref, x_ref, xn_ref, mod_ref, g_ref, w_ref, cw_ref, pw_ref, ps_ref, *rest,
                   n_lat_tiles, lat_tiles_per_seq):
    n_cast = (len(rest) - 1) // 2
    o_ref = rest[n_cast]
    _run_cast_jobs(rest[:n_cast], rest[n_cast + 1:])
    m = mod_ref[...]
    shift, scale = m[3:4], m[4:5]
    g = g_ref[...]
    ext = TOK + 2 * HALO

    def project(sub):
        lo = sub * TOK
        xp = xp_ref[...] if sub == 0 else x_ref[lo - HALO:lo]
        xn = xn_ref[...] if sub == INPROJ_SUBTILES - 1 else x_ref[lo + TOK:lo + TOK + HALO]
        xe = jnp.concatenate([x_ref[lo:lo + TOK], xn, xp], axis=0)
        he = _rms_mod(xe, g, scale, shift).astype(BF16)
        u = jnp.dot(he, w_ref[:, :OFF_Q], preferred_element_type=F32)
        qkv = jnp.dot(he[:TOK], w_ref[:, OFF_Q:], preferred_element_type=F32)
        o_ref[lo:lo + TOK, 2 * CONV_CH:2 * CONV_CH + NA_CH] = (
            qkv[:, :NA_CH] * (NA_HEAD_DIM ** -0.5)).astype(BF16)
        o_ref[lo:lo + TOK, 2 * CONV_CH + NA_CH:] = qkv[:, NA_CH:].astype(BF16)
        return u

    def mix(sub, u):
        t = pl.program_id(0) * INPROJ_SUBTILES + sub
        is_lat = t < n_lat_tiles
        pos = jnp.where(is_lat, t % lat_tiles_per_seq, 0)
        ntile = jnp.where(is_lat, lat_tiles_per_seq, 1)
        t0 = pos * TOK
        seq_len = ntile * TOK

        row = lax.broadcasted_iota(jnp.int32, (ext, 1), 0)
        next_end = jnp.where(pos == ntile - 1, TOK, TOK + HALO)
        prev_start = jnp.where(pos == 0, ext, TOK + HALO)
        u = jnp.where((row < next_end) | (row >= prev_start), u, 0.0)

        def shifted(a, s):
            return pltpu.roll(a, s % ext, 0)

        def centre(a):
            return a[:TOK]

        z = u[:, 2 * CONV_CH:3 * CONV_CH] * u[:, :CONV_CH]
        cw = cw_ref[...]
        conv = (centre(shifted(z, 1)) * cw[0:1] + centre(z) * cw[1:2]
                + centre(shifted(z, -1)) * cw[2:3])
        y_conv = centre(u[:, CONV_CH:2 * CONV_CH]) * conv

        v = u[:, 3 * CONV_CH:]
        a1 = v + shifted(v, 1)
        a2 = a1 + shifted(a1, 2)
        a3 = a2 + shifted(a2, 4)
        a4 = a3 + shifted(a3, 8)
        sums = (centre(a1), centre(shifted(a2, -1)), centre(shifted(a3, -3)),
                centre(shifted(a4, -7)))
        tpos = t0 + lax.broadcasted_iota(jnp.int32, (TOK, 1), 0)
        lane = lax.broadcasted_iota(jnp.int32, (TOK, POOL_CH), 1)
        mean = None
        for gi in reversed(range(len(POOL_WINDOWS))):
            w = POOL_WINDOWS[gi]
            left = w // 2
            right = w - 1 - left
            cnt = jnp.minimum(tpos + right + 1, seq_len) - jnp.maximum(tpos - left, 0)
            mg = sums[gi] / cnt.astype(F32)
            mean = mg if mean is None else jnp.where(lane < (gi + 1) * POOL_GROUP, mg, mean)
        dpool = (mean - centre(v)).astype(BF16)
        y_pool = jnp.dot(dpool, pw_ref[...], preferred_element_type=F32) * ps_ref[...]
        lo = sub * TOK
        o_ref[lo:lo + TOK, :CONV_CH] = y_conv.astype(BF16)
        o_ref[lo:lo + TOK, CONV_CH:CONV_CH + POOL_CH] = y_pool.astype(BF16)

    us = [project(sub) for sub in range(INPROJ_SUBTILES)]
    for sub, u in enumerate(us):
        mix(sub, u)


def _inproj(xs, mod, norm_g4, w_in, conv_w, pool_bd, pool_scale, cast_jobs, *, layer,
            n_lat_tiles, lat_tiles_per_seq, n_batch):
    nt = xs.shape[0]
    tm = INPROJ_SUBTILES * TOK
    n_steps = nt // tm
    hb = tm // HALO
    steps_per_seq = lat_tiles_per_seq // INPROJ_SUBTILES
    per_layer = lambda i: (layer, 0, 0)
    kern = functools.partial(_inproj_kernel, n_lat_tiles=n_lat_tiles,
                             lat_tiles_per_seq=lat_tiles_per_seq)
    cast_in_specs, cast_out_specs, cast_out_shapes = _cast_job_specs(cast_jobs, n_steps)
    outs = pl.pallas_call(
        kern,
        grid=(n_steps,),
        in_specs=[
            pl.BlockSpec((HALO, D_MODEL), lambda i: (jnp.maximum(i * hb - 1, 0), 0)),
            pl.BlockSpec((tm, D_MODEL), lambda i: (i, 0)),
            pl.BlockSpec((HALO, D_MODEL), lambda i: (jnp.minimum((i + 1) * hb, nt // HALO - 1), 0)),
            pl.BlockSpec((None, None, N_MOD, D_MODEL),
                         lambda i: (layer, jnp.minimum(i // steps_per_seq, n_batch), 0, 0)),
            pl.BlockSpec((None, None, 1, D_MODEL), lambda i: (layer, 1, 0, 0)),
            pl.BlockSpec((D_MODEL, D_IN), lambda i: (0, 0), pipeline_mode=pl.Buffered(1)),
            pl.BlockSpec((None, 3, CONV_CH), per_layer),
            pl.BlockSpec((None, POOL_CH, POOL_CH), per_layer),
            pl.BlockSpec((None, 1, POOL_CH), per_layer),
        ] + cast_in_specs,
        out_specs=[pl.BlockSpec((tm, 2 * D_MODEL), lambda i: (i, 0))] + cast_out_specs,
        out_shape=[jax.ShapeDtypeStruct((nt, 2 * D_MODEL), BF16)] + cast_out_shapes,
        compiler_params=pltpu.CompilerParams(
            dimension_semantics=("arbitrary",), vmem_limit_bytes=VMEM_LIMIT),
        name="inproj",
    )(xs, xs, xs, mod, norm_g4, w_in, conv_w, pool_bd, pool_scale, *[src for src, _ in cast_jobs])
    return outs[0], outs[1:]


def _pair_lanes(p, offset=0):
    return slice(offset + 2 * p * NA_HEAD_DIM, offset + 2 * (p + 1) * NA_HEAD_DIM)


def _softmax_probs(s):
    mx = jnp.max(s, axis=-1, keepdims=True)
    return jnp.exp(s - mx).astype(BF16)


def _mix_and_project(x, ycp, q_of, k_of, v_of, probs, wo_ref, gate):
    lane = lax.broadcasted_iota(jnp.int32, (1, 2 * NA_HEAD_DIM), 1)
    own = [jnp.where(lane < NA_HEAD_DIM, 1.0, 0.0).astype(BF16),
           jnp.where(lane < NA_HEAD_DIM, 0.0, 1.0).astype(BF16)]

    def scores(t):
        p, half = divmod(t, 2)
        return lax.dot_general(q_of(p) * own[half], k_of(p), (((1,), (1,)), ((), ())),
                               preferred_element_type=F32)

    def normalised(outs, first_pair, n_pairs):
        normed = []
        for p in range(first_pair, first_pair + n_pairs):
            o0, o1 = outs[2 * p], outs[2 * p + 1]
            lo = lax.broadcasted_iota(jnp.int32, o0.shape, 1) < NA_HEAD_DIM
            num = jnp.where(lo, o0, o1)
            den = jnp.where(lo, pltpu.roll(o0, NA_HEAD_DIM, 1), pltpu.roll(o1, NA_HEAD_DIM, 1))
            normed.append((num / den).astype(BF16))
        return normed

    outs = []
    pending = [scores(t) for t in range(SCORE_LOOKAHEAD)]
    for t in range(NA_HEADS):
        s = pending.pop(0)
        if t + SCORE_LOOKAHEAD < NA_HEADS:
            pending.append(scores(t + SCORE_LOOKAHEAD))
        vm = v_of(t // 2) * own[t % 2] + own[1 - t % 2]
        outs.append(jnp.dot(probs(t, s), vm, preferred_element_type=F32))
    y = jnp.concatenate([ycp] + normalised(outs, 0, NA_HEADS // 2), axis=-1)
    return x + gate * jnp.dot(y, wo_ref[...], preferred_element_type=F32)


def _build_bias_tables(tab, rp_ref):
    width = 2 * GRID_W
    n_off = 2 * NA_KH_MAX - 1
    lane = lax.broadcasted_iota(jnp.int32, (GRID_W, width), 1)
    qc = lax.broadcasted_iota(jnp.int32, (GRID_W, width), 0)
    kc = lane & (GRID_W - 1)
    cstart = jnp.clip(qc - NA_KW // 2, 0, GRID_W - NA_KW)
    col_ok = (kc >= cstart) & (kc < cstart + NA_KW)
    shift0 = (width - (NA_KW - 1)) % width

    def per_head(h, carry):
        for d in range(n_off):
            row = jnp.broadcast_to(rp_ref[h, d:d + 1, :], (GRID_W, width))
            for half in range(2):
                rolled = pltpu.roll(row, (shift0 + half * GRID_W) % width, 1, stride=1,
                                    stride_axis=0)
                tab[half, h, d] = jnp.where(col_ok, rolled, NEG_INF)
        for half in range(2):
            tab[half, h, n_off] = jnp.full((GRID_W, width), NEG_INF, F32)
        return carry

    lax.fori_loop(0, NA_HEADS, per_head, 0)


def _fill_bias(bias_buf, tab, rb, nrb):
    width = 2 * GRID_W
    n_off = 2 * NA_KH_MAX - 1
    lo_half = lax.broadcasted_iota(jnp.int32, (GRID_W, width), 1) < GRID_W
    is_first = rb == 0
    is_last = rb == nrb - 1
    base = jnp.where(is_first, 0, jnp.where(is_last, -2 * QROWS, -QROWS))

    def per_head(h, carry):
        for i in range(QROWS):
            jlo = jnp.where(is_first, 0, jnp.where(is_last, QROWS, i))
            for jj in range(KROWS // 2):
                halves = []
                for half, j in enumerate((2 * jj, 2 * jj + 1)):
                    ok = (j >= jlo) & (j < jlo + NA_KH_MAX)
                    idx = jnp.where(ok, jnp.clip(j - i + base + NA_KH_MAX - 1, 0, n_off - 1), n_off)
                    halves.append(tab[half, h, idx])
                bias_buf[h, i * GRID_W:(i + 1) * GRID_W, jj * width:(jj + 1) * width] = jnp.where(
                    lo_half, halves[0], halves[1])
        return carry

    lax.fori_loop(0, NA_HEADS, per_head, 0)


def _kv_window_copies(mix_hbm, kvbuf, sems, slot, rb, bp, nrb, n_batch):
    n_loc = NKB * TOK
    row0 = jnp.clip(rb - 1, 0, nrb - NKB) * TOK
    cols = pl.ds(2 * NA_CH, 2 * NA_CH)
    copies = []
    for g in range(ATTN_BATCHES):
        b = bp * ATTN_BATCHES + g
        copies.append(pltpu.make_async_copy(
            mix_hbm.at[b, pl.ds(row0, n_loc), cols], kvbuf.at[slot, g, pl.ds(0, n_loc)],
            sems.at[slot, g, 0]))
        copies.append(pltpu.make_async_copy(
            mix_hbm.at[n_batch, pl.ds(b * TOK, TOK), cols], kvbuf.at[slot, g, pl.ds(n_loc, TOK)],
            sems.at[slot, g, 1]))
    return copies


def _attn_kernel(x_ref, yq_ref, mix_hbm, rp_ref, mod_ref, wo_ref, o_ref,
                 kvbuf, kv_sems, bias_buf, bias_tab, *, nrb, n_bp, n_batch):
    rb = pl.program_id(0)
    bp = pl.program_id(1)
    n_loc = NKB * TOK
    step = rb * n_bp + bp
    slot = lax.rem(step, 2)

    @pl.when(step == 0)
    def _():
        for cp in _kv_window_copies(mix_hbm, kvbuf, kv_sems, slot, rb, bp, nrb, n_batch):
            cp.start()

    @pl.when(step + 1 < nrb * n_bp)
    def _():
        nxt = step + 1
        for cp in _kv_window_copies(mix_hbm, kvbuf, kv_sems, 1 - slot, nxt // n_bp,
                                    lax.rem(nxt, n_bp), nrb, n_batch):
            cp.start()

    @pl.when((pl.program_id(1) == 0) & (rb == 0))
    def _():
        _build_bias_tables(bias_tab, rp_ref)

    @pl.when((pl.program_id(1) == 0) & ((rb <= 1) | (rb == nrb - 1)))
    def _():
        _fill_bias(bias_buf, bias_tab, rb, nrb)

    for cp in _kv_window_copies(mix_hbm, kvbuf, kv_sems, slot, rb, bp, nrb, n_batch):
        cp.wait()

    def mix(probs):
        for g in range(ATTN_BATCHES):
            o_ref[g] = _mix_and_project(
                x_ref[g], yq_ref[g, :, :NA_CH], lambda p, g=g: yq_ref[g, :, _pair_lanes(p, NA_CH)],
                lambda p, g=g: kvbuf[slot, g, :, _pair_lanes(p)],
                lambda p, g=g: kvbuf[slot, g, :, _pair_lanes(p, NA_CH)], probs, wo_ref,
                mod_ref[g, 5:6])

    def probs_any(h, s):
        return _softmax_probs(jnp.concatenate([s[:, :n_loc] + bias_buf[h], s[:, n_loc:]], axis=-1))

    def probs_interior(h, s):
        width = 2 * GRID_W
        out_rows = []
        for i in range(QROWS):
            rs = slice(i * GRID_W, (i + 1) * GRID_W)
            lo, hi = (i // 2) * width, ((i + NA_KH_MAX - 1) // 2 + 1) * width
            s_i = jnp.concatenate([s[rs, lo:hi] + bias_buf[h, rs, lo:hi], s[rs, n_loc:]], axis=-1)
            p_i = _softmax_probs(s_i)
            pieces = [p_i[:, :hi - lo], p_i[:, hi - lo:]]
            if lo:
                pieces.insert(0, jnp.zeros((GRID_W, lo), BF16))
            if hi < n_loc:
                pieces.insert(-1, jnp.zeros((GRID_W, n_loc - hi), BF16))
            out_rows.append(jnp.concatenate(pieces, axis=-1))
        return jnp.concatenate(out_rows, axis=0)

    interior = (rb >= 1) & (rb < nrb - 1)
    pl.when(interior)(lambda: mix(probs_interior))
    pl.when(jnp.logical_not(interior))(lambda: mix(probs_any))


def _attention(xs, mix, rp, mod, w_out, *, layer, n_batch, rows):
    nrb = rows // QROWS
    seq = rows * GRID_W
    nt = xs.shape[0]
    g = ATTN_BATCHES
    assert nt % seq == 0 and n_batch % g == 0
    xs3 = xs.reshape(nt // seq, seq, D_MODEL)
    mix3 = mix.reshape(nt // seq, seq, mix.shape[1])
    n_bp = n_batch // g
    in_specs = [
        pl.BlockSpec((g, TOK, D_MODEL), lambda rb, bp: (bp, rb, 0)),
        pl.BlockSpec((g, TOK, 2 * NA_CH), lambda rb, bp: (bp, rb, 0)),
        pl.BlockSpec(memory_space=pl.ANY),
        pl.BlockSpec((None,) + rp.shape[1:], lambda rb, bp: (layer, 0, 0, 0)),
        pl.BlockSpec((None, g, N_MOD, D_MODEL), lambda rb, bp: (layer, bp, 0, 0)),
        pl.BlockSpec((D_MODEL, D_MODEL), lambda rb, bp: (0, 0)),
    ]
    out = pl.pallas_call(
        functools.partial(_attn_kernel, nrb=nrb, n_bp=n_bp, n_batch=n_batch),
        grid=(nrb, n_bp),
        in_specs=in_specs,
        out_specs=pl.BlockSpec((g, TOK, D_MODEL), lambda rb, bp: (bp, rb, 0)),
        out_shape=jax.ShapeDtypeStruct(xs3.shape, F32),
        scratch_shapes=[
            pltpu.VMEM((2, g, NKB * TOK + TOK, 2 * NA_CH), BF16),
            pltpu.SemaphoreType.DMA((2, g, 2)),
            pltpu.VMEM((NA_HEADS, TOK, NKB * TOK), F32),
            pltpu.VMEM((2, NA_HEADS, 2 * NA_KH_MAX, GRID_W, 2 * GRID_W), F32),
        ],
        input_output_aliases={0: 0},
        compiler_params=pltpu.CompilerParams(
            dimension_semantics=("arbitrary", "arbitrary"), vmem_limit_bytes=VMEM_LIMIT),
        name="attention",
    )(xs3, mix3, mix3, rp, mod, w_out)
    return out.reshape(nt, D_MODEL)


def _ctx_kernel(x_ref, yq_ref, kv_ref, mod_ref, wo_ref, o_ref):
    o_ref[...] = _mix_and_project(
        x_ref[...], yq_ref[:, :NA_CH], lambda p: yq_ref[:, _pair_lanes(p, NA_CH)],
        lambda p: kv_ref[:, _pair_lanes(p)], lambda p: kv_ref[:, _pair_lanes(p, NA_CH)],
        lambda h, s: _softmax_probs(s), wo_ref, mod_ref[5:6])


def _ctx_attention(xs, mix, mod, w_out, *, layer, n_batch, n_lat_tiles):
    blk = lambda b: n_lat_tiles + b
    return pl.pallas_call(
        _ctx_kernel,
        grid=(n_batch,),
        in_specs=[
            pl.BlockSpec((TOK, D_MODEL), lambda b: (blk(b), 0)),
            pl.BlockSpec((TOK, 2 * NA_CH), lambda b: (blk(b), 0)),
            pl.BlockSpec((TOK, 2 * NA_CH), lambda b: (blk(b), 1)),
            pl.BlockSpec((None, None, N_MOD, D_MODEL), lambda b: (layer, n_batch, 0, 0)),
            pl.BlockSpec((D_MODEL, D_MODEL), lambda b: (0, 0)),
        ],
        out_specs=pl.BlockSpec((TOK, D_MODEL), lambda b: (blk(b), 0)),
        out_shape=jax.ShapeDtypeStruct(xs.shape, F32),
        input_output_aliases={0: 0},
        compiler_params=pltpu.CompilerParams(
            dimension_semantics=("arbitrary",), vmem_limit_bytes=VMEM_LIMIT),
        name="ctx_attention",
    )(xs, mix, mix, mod, w_out)


def _pad_rpb(rpb):
    depth, heads, n_off, n_col = rpb.shape
    out = jnp.full((depth, heads, n_off + 1, 2 * GRID_W), NEG_INF, F32)
    return out.at[:, :, :n_off, :n_col].set(rpb)


def _pool_block_diag(pool_w):
    depth, n = pool_w.shape[:2]
    out = jnp.zeros((depth, POOL_CH, POOL_CH), pool_w.dtype)
    for gi in range(n):
        sl = slice(gi * POOL_GROUP, (gi + 1) * POOL_GROUP)
        out = out.at[:, sl, sl].set(pool_w[:, gi])
    return out


def kernel(x, c, ctx, c_ctx, w_mod, b_mod, norm_g, ffn_w_in, ffn_w_out, w_in, conv_w, pool_w,
           pool_scale, rpb, w_out, final_g):
    n_batch, seq, d = x.shape
    ctx_len = ctx.shape[1]
    depth = w_mod.shape[0]
    rows = seq // GRID_W
    assert d == D_MODEL and ctx_len == TOK and seq % FFN_TM == 0 and rows >= KROWS
    assert (n_batch * ctx_len) % FFN_TM == 0 and rows // QROWS >= NKB + 1
    assert seq % (INPROJ_SUBTILES * TOK) == 0 and (n_batch * ctx_len) % (INPROJ_SUBTILES * TOK) == 0
    assert NA_KH_MAX // 2 == QROWS and rpb.shape[2:] == (2 * NA_KH_MAX - 1, 2 * NA_KW - 1)
    assert n_batch + 1 <= MOD_ROWS and w_in.shape[-1] == D_IN and ffn_w_in.shape[-1] == 2 * D_FF
    n_lat = n_batch * seq
    n_lat_tiles = n_lat // TOK
    tiles_per_seq = seq // TOK
    lat_ffn_tiles = n_lat // FFN_TM
    ctx_ffn_tiles = n_batch * ctx_len // FFN_TM

    cc = jnp.zeros((MOD_ROWS, d), F32).at[:n_batch].set(c).at[n_batch].set(c_ctx)
    mod, (w1, w2, w_in_b, w_out_b) = _modulation(
        cc, w_mod, b_mod, [(ffn_w_in, (0, 0)), (ffn_w_out, (0, 0)), (w_in, (0,)), (w_out, (0,))])
    mod = mod.reshape(depth, MOD_ROWS, N_MOD, d)

    pool_bd = _pool_block_diag(pool_w).astype(BF16)
    pool_sc = pool_scale.reshape(depth, 1, POOL_CH)
    norm_g4 = norm_g.reshape(depth, 3, 1, d)
    rp = _pad_rpb(rpb)

    ffn_kw = dict(lat_tiles_per_batch=seq // FFN_TM, n_batch=n_batch)
    xa, xb, xb_off = x.reshape(n_lat, d), ctx.reshape(n_batch * ctx_len, d), 0
    for l in range(depth):
        last = l == depth - 1
        xs = _ffn(xa, xb, xb_off, lat_ffn_tiles, ctx_ffn_tiles, mod, norm_g4, w1, w2, final_g,
                  layer=l, sub=0, final=False, **ffn_kw)
        jobs = [(ffn_w_in, (l, 1)), (ffn_w_out, (l, 1))]
        if not last:
            jobs += [(ffn_w_in, (l + 1, 0)), (ffn_w_out, (l + 1, 0)), (w_in, (l + 1,)),
                     (w_out, (l + 1,))]
        mix, cast = _inproj(xs, mod, norm_g4, w_in_b, conv_w, pool_bd, pool_sc, jobs, layer=l,
                            n_lat_tiles=n_lat_tiles, lat_tiles_per_seq=tiles_per_seq,
                            n_batch=n_batch)
        xs = _attention(xs, mix, rp, mod, w_out_b, layer=l, n_batch=n_batch, rows=rows)
        if not last:
            xs = _ctx_attention(xs, mix, mod, w_out_b, layer=l, n_batch=n_batch,
                                n_lat_tiles=n_lat_tiles)
        xs = _ffn(xs, xs, lat_ffn_tiles, lat_ffn_tiles, 0 if last else ctx_ffn_tiles, mod, norm_g4,
                  cast[0], cast[1], final_g, layer=l, sub=2, final=last, **ffn_kw)
        if not last:
            w1, w2, w_in_b, w_out_b = cast[2:]
        xa, xb, xb_off = xs, xs, lat_ffn_tiles
    return xs.reshape(n_batch, seq, d)
```

```python
import functools

import jax
import jax.numpy as jnp
from jax import lax
from jax.experimental import pallas as pl
from jax.experimental.pallas import tpu as pltpu

F32 = jnp.float32
BF16 = jnp.bfloat16

D_MODEL = 1024
GRID_W = 64
D_FF = 2816
N_MOD = 9
RMS_EPS = 1e-6
NEG_INF = -1e30
CONV_CH = D_MODEL // 4
POOL_CH = D_MODEL // 4
POOL_WINDOWS = (2, 4, 8, 16)
POOL_GROUP = POOL_CH // 4
NA_HEAD_DIM = 64
NA_CH = D_MODEL // 2
NA_HEADS = NA_CH // NA_HEAD_DIM
NA_KH_MAX = 8
NA_KW = 16
OFF_Q = 3 * CONV_CH + POOL_CH
D_IN = OFF_Q + 3 * NA_CH

FFN_TM = 1024
FFN_CHUNKS = ((0, 1536), (1536, 2816))
FFN_SUBTILES = 4
CAST_BLOCKS = 16
BF16_SUBLANES = 16
TOK = 256
HALO = 8
INPROJ_SUBTILES = 4
QROWS = TOK // GRID_W
KROWS = QROWS + NA_KH_MAX
NKB = KROWS * GRID_W // TOK
SCORE_LOOKAHEAD = 1
ATTN_BATCHES = 2
MOD_ROWS = 16
MOD_TN = 1152
VMEM_LIMIT = 56 * 1024 * 1024


def _rms_mod(x, g, scale, shift):
    ms = jnp.mean(x * x, axis=-1, keepdims=True)
    return (x * lax.rsqrt(ms + RMS_EPS) * g) * (1.0 + scale) + shift


def _mod_kernel(c_ref, w_ref, b_ref, *rest):
    n_cast = (len(rest) - 1) // 2
    o_ref = rest[n_cast]
    _run_cast_jobs(rest[:n_cast], rest[n_cast + 1:])
    c = c_ref[...]
    s = (c * jax.nn.sigmoid(c)).astype(BF16)
    o_ref[...] = jnp.dot(s, w_ref[...].astype(BF16), preferred_element_type=F32) + b_ref[...]


def _modulation(cc, w_mod, b_mod, cast_jobs):
    depth = w_mod.shape[0]
    n = N_MOD * D_MODEL
    per_layer = n // MOD_TN
    cast_in_specs, cast_out_specs, cast_out_shapes = _cast_job_specs(cast_jobs, depth * per_layer)
    outs = pl.pallas_call(
        _mod_kernel,
        grid=(depth * per_layer,),
        in_specs=[
            pl.BlockSpec((MOD_ROWS, D_MODEL), lambda i: (0, 0)),
            pl.BlockSpec((None, D_MODEL, MOD_TN), lambda i: (i // per_layer, 0, i % per_layer)),
            pl.BlockSpec((None, 1, MOD_TN), lambda i: (i // per_layer, 0, i % per_layer)),
        ] + cast_in_specs,
        out_specs=[pl.BlockSpec((None, MOD_ROWS, MOD_TN),
                                lambda i: (i // per_layer, 0, i % per_layer))] + cast_out_specs,
        out_shape=[jax.ShapeDtypeStruct((depth, MOD_ROWS, n), F32)] + cast_out_shapes,
        compiler_params=pltpu.CompilerParams(
            dimension_semantics=("arbitrary",), vmem_limit_bytes=VMEM_LIMIT),
        name="modulation",
    )(cc, w_mod, b_mod.reshape(depth, 1, n), *[src for src, _ in cast_jobs])
    return outs[0], outs[1:]


def _cast_job_specs(cast_jobs, n_steps):
    assert n_steps >= CAST_BLOCKS
    cast_blk = lambda i: jnp.minimum(i, CAST_BLOCKS - 1)
    in_specs, out_specs, out_shapes = [], [], []
    for src, lead in cast_jobs:
        rows, cols = src.shape[-2:]
        assert rows % (CAST_BLOCKS * BF16_SUBLANES) == 0 and len(lead) == src.ndim - 2
        blk = (rows // CAST_BLOCKS, cols)
        in_specs.append(pl.BlockSpec((None,) * len(lead) + blk,
                                     lambda i, lead=lead: lead + (cast_blk(i), 0)))
        out_specs.append(pl.BlockSpec(blk, lambda i: (cast_blk(i), 0)))
        out_shapes.append(jax.ShapeDtypeStruct((rows, cols), BF16))
    return in_specs, out_specs, out_shapes


def _run_cast_jobs(src_refs, dst_refs):
    for src_ref, dst_ref in zip(src_refs, dst_refs):
        dst_ref[...] = src_ref[...].astype(BF16)


def _ffn_kernel(xa_ref, xb_ref, mod_ref, g_ref, w1_ref, w2_ref, fg_ref, o_ref, *, sub, final,
                n_a_tiles, n_b_tiles):
    m = mod_ref[...]
    shift = m[3 * sub:3 * sub + 1]
    scale = m[3 * sub + 1:3 * sub + 2]
    gate = m[3 * sub + 2:3 * sub + 3]
    rows = FFN_TM // FFN_SUBTILES

    def load_x(r):
        sl = slice(r * rows, (r + 1) * rows)
        if n_b_tiles:
            return jnp.where(pl.program_id(0) < n_a_tiles, xa_ref[sl], xb_ref[sl])
        return xa_ref[sl]

    units = [(r, c) for r in range(FFN_SUBTILES) for c in range(len(FFN_CHUNKS))]
    xs, hbs, accs = {}, {}, {}

    def up(r, c):
        if r not in xs:
            xs[r] = load_x(r)
            hbs[r] = _rms_mod(xs[r], g_ref[...], scale, shift).astype(BF16)
        lo, hi = FFN_CHUNKS[c]
        a = jnp.dot(hbs[r], w1_ref[:, lo:hi], preferred_element_type=F32)
        b = jnp.dot(hbs[r], w1_ref[:, D_FF + lo:D_FF + hi], preferred_element_type=F32)
        return a, b

    ab = up(*units[0])
    for i, (r, c) in enumerate(units):
        a, b = ab
        if i + 1 < len(units):
            ab = up(*units[i + 1])
        lo, hi = FFN_CHUNKS[c]
        gg = (a * jax.nn.sigmoid(a) * b).astype(BF16)
        part = jnp.dot(gg, w2_ref[lo:hi, :], preferred_element_type=F32)
        accs[r] = part if r not in accs else accs[r] + part
    for r in range(FFN_SUBTILES):
        y = xs[r] + (0.5 * gate) * accs[r]
        if final:
            ms = jnp.mean(y * y, axis=-1, keepdims=True)
            y = y * lax.rsqrt(ms + RMS_EPS) * fg_ref[...]
        o_ref[r * rows:(r + 1) * rows] = y


def _ffn(xa, xb, xb_off, n_a_tiles, n_b_tiles, mod, norm_g4, w1, w2, final_g, *, layer, sub, final,
         lat_tiles_per_batch, n_batch):
    n_tiles = n_a_tiles + n_b_tiles
    const = lambda i: (0, 0)
    kern = functools.partial(_ffn_kernel, sub=sub, final=final, n_a_tiles=n_a_tiles,
                             n_b_tiles=n_b_tiles)
    return pl.pallas_call(
        kern,
        grid=(n_tiles,),
        in_specs=[
            pl.BlockSpec((FFN_TM, D_MODEL), lambda i: (jnp.minimum(i, n_a_tiles - 1), 0)),
            pl.BlockSpec((FFN_TM, D_MODEL), lambda i: (jnp.maximum(i - n_a_tiles, 0) + xb_off, 0)),
            pl.BlockSpec((None, None, N_MOD, D_MODEL),
                         lambda i: (layer, jnp.minimum(i // lat_tiles_per_batch, n_batch), 0, 0)),
            pl.BlockSpec((None, None, 1, D_MODEL), lambda i: (layer, sub, 0, 0)),
            pl.BlockSpec((D_MODEL, 2 * D_FF), const, pipeline_mode=pl.Buffered(1)),
            pl.BlockSpec((D_FF, D_MODEL), const, pipeline_mode=pl.Buffered(1)),
            pl.BlockSpec((1, D_MODEL), const),
        ],
        out_specs=pl.BlockSpec((FFN_TM, D_MODEL), lambda i: (i, 0)),
        out_shape=jax.ShapeDtypeStruct((n_tiles * FFN_TM, D_MODEL), F32),
        compiler_params=pltpu.CompilerParams(
            dimension_semantics=("arbitrary",), vmem_limit_bytes=VMEM_LIMIT),
        name="ffn",
    )(xa, xb, mod, norm_g4, w1, w2, final_g.reshape(1, D_MODEL))


def _inproj_kernel(xp_ref, x_ref, xn_ref, mod_ref, g_ref, w_ref, cw_ref, pw_ref, ps_ref, *rest,
                   n_lat_tiles, lat_tiles_per_seq):
    n_cast = (len(rest) - 1) // 2
    o_ref = rest[n_cast]
    _run_cast_jobs(rest[:n_cast], rest[n_cast + 1:])
    m = mod_ref[...]
    shift, scale = m[3:4], m[4:5]
    g = g_ref[...]
    ext = TOK + 2 * HALO

    def project(sub):
        lo = sub * TOK
        xp = xp_ref[...] if sub == 0 else x_ref[lo - HALO:lo]
        xn = xn_ref[...] if sub == INPROJ_SUBTILES - 1 else x_ref[lo + TOK:lo + TOK + HALO]
        xe = jnp.concatenate([x_ref[lo:lo + TOK], xn, xp], axis=0)
        he = _rms_mod(xe, g, scale, shift).astype(BF16)
        u = jnp.dot(he, w_ref[:, :OFF_Q], preferred_element_type=F32)
        qkv = jnp.dot(he[:TOK], w_ref[:, OFF_Q:], preferred_element_type=F32)
        o_ref[lo:lo + TOK, 2 * CONV_CH:2 * CONV_CH + NA_CH] = (
            qkv[:, :NA_CH] * (NA_HEAD_DIM ** -0.5)).astype(BF16)
        o_ref[lo:lo + TOK, 2 * CONV_CH + NA_CH:] = qkv[:, NA_CH:].astype(BF16)
        return u

    def mix(sub, u):
        t = pl.program_id(0) * INPROJ_SUBTILES + sub
        is_lat = t < n_lat_tiles
        pos = jnp.where(is_lat, t % lat_tiles_per_seq, 0)
        ntile = jnp.where(is_lat, lat_tiles_per_seq, 1)
        t0 = pos * TOK
        seq_len = ntile * TOK

        row = lax.broadcasted_iota(jnp.int32, (ext, 1), 0)
        next_end = jnp.where(pos == ntile - 1, TOK, TOK + HALO)
        prev_start = jnp.where(pos == 0, ext, TOK + HALO)
        u = jnp.where((row < next_end) | (row >= prev_start), u, 0.0)

        def shifted(a, s):
            return pltpu.roll(a, s % ext, 0)

        def centre(a):
            return a[:TOK]

        z = u[:, 2 * CONV_CH:3 * CONV_CH] * u[:, :CONV_CH]
        cw = cw_ref[...]
        conv = (centre(shifted(z, 1)) * cw[0:1] + centre(z) * cw[1:2]
                + centre(shifted(z, -1)) * cw[2:3])
        y_conv = centre(u[:, CONV_CH:2 * CONV_CH]) * conv

        v = u[:, 3 * CONV_CH:]
        a1 = v + shifted(v, 1)
        a2 = a1 + shifted(a1, 2)
        a3 = a2 + shifted(a2, 4)
        a4 = a3 + shifted(a3, 8)
        sums = (centre(a1), centre(shifted(a2, -1)), centre(shifted(a3, -3)),
                centre(shifted(a4, -7)))
        tpos = t0 + lax.broadcasted_iota(jnp.int32, (TOK, 1), 0)
        lane = lax.broadcasted_iota(jnp.int32, (TOK, POOL_CH), 1)
        mean = None
        for gi in reversed(range(len(POOL_WINDOWS))):
            w = POOL_WINDOWS[gi]
            left = w // 2
            right = w - 1 - left
            cnt = jnp.minimum(tpos + right + 1, seq_len) - jnp.maximum(tpos - left, 0)
            mg = sums[gi] / cnt.astype(F32)
            mean = mg if mean is None else jnp.where(lane < (gi + 1) * POOL_GROUP, mg, mean)
        dpool = (mean - centre(v)).astype(BF16)
        y_pool = jnp.dot(dpool, pw_ref[...], preferred_element_type=F32) * ps_ref[...]
        lo = sub * TOK
        o_ref[lo:lo + TOK, :CONV_CH] = y_conv.astype(BF16)
        o_ref[lo:lo + TOK, CONV_CH:CONV_CH + POOL_CH] = y_pool.astype(BF16)

    us = [project(sub) for sub in range(INPROJ_SUBTILES)]
    for sub, u in enumerate(us):
        mix(sub, u)


def _inproj(xs, mod, norm_g4, w_in, conv_w, pool_bd, pool_scale, cast_jobs, *, layer,
            n_lat_tiles, lat_tiles_per_seq, n_batch):
    nt = xs.shape[0]
    tm = INPROJ_SUBTILES * TOK
    n_steps = nt // tm
    hb = tm // HALO
    steps_per_seq = lat_tiles_per_seq // INPROJ_SUBTILES
    per_layer = lambda i: (layer, 0, 0)
    kern = functools.partial(_inproj_kernel, n_lat_tiles=n_lat_tiles,
                             lat_tiles_per_seq=lat_tiles_per_seq)
    cast_in_specs, cast_out_specs, cast_out_shapes = _cast_job_specs(cast_jobs, n_steps)
    outs = pl.pallas_call(
        kern,
        grid=(n_steps,),
        in_specs=[
            pl.BlockSpec((HALO, D_MODEL), lambda i: (jnp.maximum(i * hb - 1, 0), 0)),
            pl.BlockSpec((tm, D_MODEL), lambda i: (i, 0)),
            pl.BlockSpec((HALO, D_MODEL), lambda i: (jnp.minimum((i + 1) * hb, nt // HALO - 1), 0)),
            pl.BlockSpec((None, None, N_MOD, D_MODEL),
                         lambda i: (layer, jnp.minimum(i // steps_per_seq, n_batch), 0, 0)),
            pl.BlockSpec((None, None, 1, D_MODEL), lambda i: (layer, 1, 0, 0)),
            pl.BlockSpec((D_MODEL, D_IN), lambda i: (0, 0), pipeline_mode=pl.Buffered(1)),
            pl.BlockSpec((None, 3, CONV_CH), per_layer),
            pl.BlockSpec((None, POOL_CH, POOL_CH), per_layer),
            pl.BlockSpec((None, 1, POOL_CH), per_layer),
        ] + cast_in_specs,
        out_specs=[pl.BlockSpec((tm, 2 * D_MODEL), lambda i: (i, 0))] + cast_out_specs,
        out_shape=[jax.ShapeDtypeStruct((nt, 2 * D_MODEL), BF16)] + cast_out_shapes,
        compiler_params=pltpu.CompilerParams(
            dimension_semantics=("arbitrary",), vmem_limit_bytes=VMEM_LIMIT),
        name="inproj",
    )(xs, xs, xs, mod, norm_g4, w_in, conv_w, pool_bd, pool_scale, *[src for src, _ in cast_jobs])
    return outs[0], outs[1:]


def _pair_lanes(p, offset=0):
    return slice(offset + 2 * p * NA_HEAD_DIM, offset + 2 * (p + 1) * NA_HEAD_DIM)


def _softmax_probs(s):
    mx = jnp.max(s, axis=-1, keepdims=True)
    return jnp.exp(s - mx).astype(BF16)


def _mix_and_project(x, ycp, q_of, k_of, v_of, probs, wo_ref, gate):
    lane = lax.broadcasted_iota(jnp.int32, (1, 2 * NA_HEAD_DIM), 1)
    own = [jnp.where(lane < NA_HEAD_DIM, 1.0, 0.0).astype(BF16),
           jnp.where(lane < NA_HEAD_DIM, 0.0, 1.0).astype(BF16)]

    def scores(t):
        p, half = divmod(t, 2)
        return lax.dot_general(q_of(p) * own[half], k_of(p), (((1,), (1,)), ((), ())),
                               preferred_element_type=F32)

    def normalised(outs, first_pair, n_pairs):
        normed = []
        for p in range(first_pair, first_pair + n_pairs):
            o0, o1 = outs[2 * p], outs[2 * p + 1]
            lo = lax.broadcasted_iota(jnp.int32, o0.shape, 1) < NA_HEAD_DIM
            num = jnp.where(lo, o0, o1)
            den = jnp.where(lo, pltpu.roll(o0, NA_HEAD_DIM, 1), pltpu.roll(o1, NA_HEAD_DIM, 1))
            normed.append((num / den).astype(BF16))
        return normed

    outs = []
    pending = [scores(t) for t in range(SCORE_LOOKAHEAD)]
    for t in range(NA_HEADS):
        s = pending.pop(0)
        if t + SCORE_LOOKAHEAD < NA_HEADS:
            pending.append(scores(t + SCORE_LOOKAHEAD))
        vm = v_of(t // 2) * own[t % 2] + own[1 - t % 2]
        outs.append(jnp.dot(probs(t, s), vm, preferred_element_type=F32))
    y = jnp.concatenate([ycp] + normalised(outs, 0, NA_HEADS // 2), axis=-1)
    return x + gate * jnp.dot(y, wo_ref[...], preferred_element_type=F32)


def _build_bias_tables(tab, rp_ref):
    width = 2 * GRID_W
    n_off = 2 * NA_KH_MAX - 1
    lane = lax.broadcasted_iota(jnp.int32, (GRID_W, width), 1)
    qc = lax.broadcasted_iota(jnp.int32, (GRID_W, width), 0)
    kc = lane & (GRID_W - 1)
    cstart = jnp.clip(qc - NA_KW // 2, 0, GRID_W - NA_KW)
    col_ok = (kc >= cstart) & (kc < cstart + NA_KW)
    shift0 = (width - (NA_KW - 1)) % width

    def per_head(h, carry):
        for d in range(n_off):
            row = jnp.broadcast_to(rp_ref[h, d:d + 1, :], (GRID_W, width))
            for half in range(2):
                rolled = pltpu.roll(row, (shift0 + half * GRID_W) % width, 1, stride=1,
                                    stride_axis=0)
                tab[half, h, d] = jnp.where(col_ok, rolled, NEG_INF)
        for half in range(2):
            tab[half, h, n_off] = jnp.full((GRID_W, width), NEG_INF, F32)
        return carry

    lax.fori_loop(0, NA_HEADS, per_head, 0)


def _fill_bias(bias_buf, tab, rb, nrb):
    width = 2 * GRID_W
    n_off = 2 * NA_KH_MAX - 1
    lo_half = lax.broadcasted_iota(jnp.int32, (GRID_W, width), 1) < GRID_W
    is_first = rb == 0
    is_last = rb == nrb - 1
    base = jnp.where(is_first, 0, jnp.where(is_last, -2 * QROWS, -QROWS))

    def per_head(h, carry):
        for i in range(QROWS):
            jlo = jnp.where(is_first, 0, jnp.where(is_last, QROWS, i))
            for jj in range(KROWS // 2):
                halves = []
                for half, j in enumerate((2 * jj, 2 * jj + 1)):
                    ok = (j >= jlo) & (j < jlo + NA_KH_MAX)
                    idx = jnp.where(ok, jnp.clip(j - i + base + NA_KH_MAX - 1, 0, n_off - 1), n_off)
                    halves.append(tab[half, h, idx])
                bias_buf[h, i * GRID_W:(i + 1) * GRID_W, jj * width:(jj + 1) * width] = jnp.where(
                    lo_half, halves[0], halves[1])
        return carry

    lax.fori_loop(0, NA_HEADS, per_head, 0)


def _kv_window_copies(mix_hbm, kvbuf, sems, slot, rb, bp, nrb, n_batch):
    n_loc = NKB * TOK
    row0 = jnp.clip(rb - 1, 0, nrb - NKB) * TOK
    cols = pl.ds(2 * NA_CH, 2 * NA_CH)
    copies = []
    for g in range(ATTN_BATCHES):
        b = bp * ATTN_BATCHES + g
        copies.append(pltpu.make_async_copy(
            mix_hbm.at[b, pl.ds(row0, n_loc), cols], kvbuf.at[slot, g, pl.ds(0, n_loc)],
            sems.at[slot, g, 0]))
        copies.append(pltpu.make_async_copy(
            mix_hbm.at[n_batch, pl.ds(b * TOK, TOK), cols], kvbuf.at[slot, g, pl.ds(n_loc, TOK)],
            sems.at[slot, g, 1]))
    return copies


def _attn_kernel(x_ref, yq_ref, mix_hbm, rp_ref, mod_ref, wo_ref, o_ref,
                 kvbuf, kv_sems, bias_buf, bias_tab, *, nrb, n_bp, n_batch):
    rb = pl.program_id(0)
    bp = pl.program_id(1)
    n_loc = NKB * TOK
    step = rb * n_bp + bp
    slot = lax.rem(step, 2)

    @pl.when(step == 0)
    def _():
        for cp in _kv_window_copies(mix_hbm, kvbuf, kv_sems, slot, rb, bp, nrb, n_batch):
            cp.start()

    @pl.when(step + 1 < nrb * n_bp)
    def _():
        nxt = step + 1
        for cp in _kv_window_copies(mix_hbm, kvbuf, kv_sems, 1 - slot, nxt // n_bp,
                                    lax.rem(nxt, n_bp), nrb, n_batch):
            cp.start()

    @pl.when((pl.program_id(1) == 0) & (rb == 0))
    def _():
        _build_bias_tables(bias_tab, rp_ref)

    @pl.when((pl.program_id(1) == 0) & ((rb <= 1) | (rb == nrb - 1)))
    def _():
        _fill_bias(bias_buf, bias_tab, rb, nrb)

    for cp in _kv_window_copies(mix_hbm, kvbuf, kv_sems, slot, rb, bp, nrb, n_batch):
        cp.wait()

    def mix(probs):
        for g in range(ATTN_BATCHES):
            o_ref[g] = _mix_and_project(
                x_ref[g], yq_ref[g, :, :NA_CH], lambda p, g=g: yq_ref[g, :, _pair_lanes(p, NA_CH)],
                lambda p, g=g: kvbuf[slot, g, :, _pair_lanes(p)],
                lambda p, g=g: kvbuf[slot, g, :, _pair_lanes(p, NA_CH)], probs, wo_ref,
                mod_ref[g, 5:6])

    def probs_for(groups):
        def probs(h, s):
            out_rows = []
            for q0, nq, k0, nk in groups:
                rs = slice(q0 * GRID_W, (q0 + nq) * GRID_W)
                lo, hi = k0 * GRID_W, (k0 + nk) * GRID_W
                s_g = jnp.concatenate([s[rs, lo:hi] + bias_buf[h, rs, lo:hi], s[rs, n_loc:]],
                                      axis=-1)
                p_g = _softmax_probs(s_g)
                pieces = [p_g[:, :hi - lo], p_g[:, hi - lo:]]
                if lo:
                    pieces.insert(0, jnp.zeros((nq * GRID_W, lo), BF16))
                if hi < n_loc:
                    pieces.insert(-1, jnp.zeros((nq * GRID_W, n_loc - hi), BF16))
                out_rows.append(jnp.concatenate(pieces, axis=-1))
            return jnp.concatenate(out_rows, axis=0)
        return probs

    first_groups = [(0, QROWS, 0, NA_KH_MAX)]
    last_groups = [(0, QROWS, KROWS - NA_KH_MAX, NA_KH_MAX)]
    interior_groups = [(i, 1, 2 * (i // 2), 2 * ((i + NA_KH_MAX - 1) // 2 + 1) - 2 * (i // 2))
                       for i in range(QROWS)]
    pl.when(rb == 0)(lambda: mix(probs_for(first_groups)))
    pl.when(rb == nrb - 1)(lambda: mix(probs_for(last_groups)))
    pl.when((rb >= 1) & (rb < nrb - 1))(lambda: mix(probs_for(interior_groups)))


def _attention(xs, mix, rp, mod, w_out, *, layer, n_batch, rows):
    nrb = rows // QROWS
    seq = rows * GRID_W
    nt = xs.shape[0]
    g = ATTN_BATCHES
    assert nt % seq == 0 and n_batch % g == 0
    xs3 = xs.reshape(nt // seq, seq, D_MODEL)
    mix3 = mix.reshape(nt // seq, seq, mix.shape[1])
    n_bp = n_batch // g
    in_specs = [
        pl.BlockSpec((g, TOK, D_MODEL), lambda rb, bp: (bp, rb, 0)),
        pl.BlockSpec((g, TOK, 2 * NA_CH), lambda rb, bp: (bp, rb, 0)),
        pl.BlockSpec(memory_space=pl.ANY),
        pl.BlockSpec((None,) + rp.shape[1:], lambda rb, bp: (layer, 0, 0, 0)),
        pl.BlockSpec((None, g, N_MOD, D_MODEL), lambda rb, bp: (layer, bp, 0, 0)),
        pl.BlockSpec((D_MODEL, D_MODEL), lambda rb, bp: (0, 0)),
    ]
    out = pl.pallas_call(
        functools.partial(_attn_kernel, nrb=nrb, n_bp=n_bp, n_batch=n_batch),
        grid=(nrb, n_bp),
        in_specs=in_specs,
        out_specs=pl.BlockSpec((g, TOK, D_MODEL), lambda rb, bp: (bp, rb, 0)),
        out_shape=jax.ShapeDtypeStruct(xs3.shape, F32),
        scratch_shapes=[
            pltpu.VMEM((2, g, NKB * TOK + TOK, 2 * NA_CH), BF16),
            pltpu.SemaphoreType.DMA((2, g, 2)),
            pltpu.VMEM((NA_HEADS, TOK, NKB * TOK), F32),
            pltpu.VMEM((2, NA_HEADS, 2 * NA_KH_MAX, GRID_W, 2 * GRID_W), F32),
        ],
        input_output_aliases={0: 0},
        compiler_params=pltpu.CompilerParams(
            dimension_semantics=("arbitrary", "arbitrary"), vmem_limit_bytes=VMEM_LIMIT),
        name="attention",
    )(xs3, mix3, mix3, rp, mod, w_out)
    return out.reshape(nt, D_MODEL)


def _ctx_kernel(x_ref, yq_ref, kv_ref, mod_ref, wo_ref, o_ref):
    o_ref[...] = _mix_and_project(
        x_ref[...], yq_ref[:, :NA_CH], lambda p: yq_ref[:, _pair_lanes(p, NA_CH)],
        lambda p: kv_ref[:, _pair_lanes(p)], lambda p: kv_ref[:, _pair_lanes(p, NA_CH)],
        lambda h, s: _softmax_probs(s), wo_ref, mod_ref[5:6])


def _ctx_attention(xs, mix, mod, w_out, *, layer, n_batch, n_lat_tiles):
    blk = lambda b: n_lat_tiles + b
    return pl.pallas_call(
        _ctx_kernel,
        grid=(n_batch,),
        in_specs=[
            pl.BlockSpec((TOK, D_MODEL), lambda b: (blk(b), 0)),
            pl.BlockSpec((TOK, 2 * NA_CH), lambda b: (blk(b), 0)),
            pl.BlockSpec((TOK, 2 * NA_CH), lambda b: (blk(b), 1)),
            pl.BlockSpec((None, None, N_MOD, D_MODEL), lambda b: (layer, n_batch, 0, 0)),
            pl.BlockSpec((D_MODEL, D_MODEL), lambda b: (0, 0)),
        ],
        out_specs=pl.BlockSpec((TOK, D_MODEL), lambda b: (blk(b), 0)),
        out_shape=jax.ShapeDtypeStruct(xs.shape, F32),
        input_output_aliases={0: 0},
        compiler_params=pltpu.CompilerParams(
            dimension_semantics=("arbitrary",), vmem_limit_bytes=VMEM_LIMIT),
        name="ctx_attention",
    )(xs, mix, mix, mod, w_out)


def _pad_rpb(rpb):
    depth, heads, n_off, n_col = rpb.shape
    out = jnp.full((depth, heads, n_off + 1, 2 * GRID_W), NEG_INF, F32)
    return out.at[:, :, :n_off, :n_col].set(rpb)


def _pool_block_diag(pool_w):
    depth, n = pool_w.shape[:2]
    out = jnp.zeros((depth, POOL_CH, POOL_CH), pool_w.dtype)
    for gi in range(n):
        sl = slice(gi * POOL_GROUP, (gi + 1) * POOL_GROUP)
        out = out.at[:, sl, sl].set(pool_w[:, gi])
    return out


def kernel(x, c, ctx, c_ctx, w_mod, b_mod, norm_g, ffn_w_in, ffn_w_out, w_in, conv_w, pool_w,
           pool_scale, rpb, w_out, final_g):
    n_batch, seq, d = x.shape
    ctx_len = ctx.shape[1]
    depth = w_mod.shape[0]
    rows = seq // GRID_W
    assert d == D_MODEL and ctx_len == TOK and seq % FFN_TM == 0 and rows >= KROWS
    assert (n_batch * ctx_len) % FFN_TM == 0 and rows // QROWS >= NKB + 1
    assert seq % (INPROJ_SUBTILES * TOK) == 0 and (n_batch * ctx_len) % (INPROJ_SUBTILES * TOK) == 0
    assert NA_KH_MAX // 2 == QROWS and rpb.shape[2:] == (2 * NA_KH_MAX - 1, 2 * NA_KW - 1)
    assert n_batch + 1 <= MOD_ROWS and w_in.shape[-1] == D_IN and ffn_w_in.shape[-1] == 2 * D_FF
    n_lat = n_batch * seq
    n_lat_tiles = n_lat // TOK
    tiles_per_seq = seq // TOK
    lat_ffn_tiles = n_lat // FFN_TM
    ctx_ffn_tiles = n_batch * ctx_len // FFN_TM

    cc = jnp.zeros((MOD_ROWS, d), F32).at[:n_batch].set(c).at[n_batch].set(c_ctx)
    mod, (w1, w2, w_in_b, w_out_b) = _modulation(
        cc, w_mod, b_mod, [(ffn_w_in, (0, 0)), (ffn_w_out, (0, 0)), (w_in, (0,)), (w_out, (0,))])
    mod = mod.reshape(depth, MOD_ROWS, N_MOD, d)

    pool_bd = _pool_block_diag(pool_w).astype(BF16)
    pool_sc = pool_scale.reshape(depth, 1, POOL_CH)
    norm_g4 = norm_g.reshape(depth, 3, 1, d)
    rp = _pad_rpb(rpb)

    ffn_kw = dict(lat_tiles_per_batch=seq // FFN_TM, n_batch=n_batch)
    xa, xb, xb_off = x.reshape(n_lat, d), ctx.reshape(n_batch * ctx_len, d), 0
    for l in range(depth):
        last = l == depth - 1
        xs = _ffn(xa, xb, xb_off, lat_ffn_tiles, ctx_ffn_tiles, mod, norm_g4, w1, w2, final_g,
                  layer=l, sub=0, final=False, **ffn_kw)
        jobs = [(ffn_w_in, (l, 1)), (ffn_w_out, (l, 1))]
        if not last:
            jobs += [(ffn_w_in, (l + 1, 0)), (ffn_w_out, (l + 1, 0)), (w_in, (l + 1,)),
                     (w_out, (l + 1,))]
        mix, cast = _inproj(xs, mod, norm_g4, w_in_b, conv_w, pool_bd, pool_sc, jobs, layer=l,
                            n_lat_tiles=n_lat_tiles, lat_tiles_per_seq=tiles_per_seq,
                            n_batch=n_batch)
        xs = _attention(xs, mix, rp, mod, w_out_b, layer=l, n_batch=n_batch, rows=rows)
        if not last:
            xs = _ctx_attention(xs, mix, mod, w_out_b, layer=l, n_batch=n_batch,
                                n_lat_tiles=n_lat_tiles)
        xs = _ffn(xs, xs, lat_ffn_tiles, lat_ffn_tiles, 0 if last else ctx_ffn_tiles, mod, norm_g4,
                  cast[0], cast[1], final_g, layer=l, sub=2, final=last, **ffn_kw)
        if not last:
            w1, w2, w_in_b, w_out_b = cast[2:]
        xa, xb, xb_off = xs, xs, lat_ffn_tiles
    return xs.reshape(n_batch, seq, d)
```

```python
import functools

import jax
import jax.numpy as jnp
from jax import lax
from jax.experimental import pallas as pl
from jax.experimental.pallas import tpu as pltpu

F32 = jnp.float32
BF16 = jnp.bfloat16

D_MODEL = 1024
GRID_W = 64
D_FF = 2816
N_MOD = 9
RMS_EPS = 1e-6
NEG_INF = -1e30
CONV_CH = D_MODEL // 4
POOL_CH = D_MODEL // 4
POOL_WINDOWS = (2, 4, 8, 16)
POOL_GROUP = POOL_CH // 4
NA_HEAD_DIM = 64
NA_CH = D_MODEL // 2
NA_HEADS = NA_CH // NA_HEAD_DIM
NA_KH_MAX = 8
NA_KW = 16
OFF_Q = 3 * CONV_CH + POOL_CH
D_IN = OFF_Q + 3 * NA_CH

FFN_TM = 1024
FFN_CHUNKS = ((0, 1536), (1536, 2816))
FFN_SUBTILES = 4
CAST_BLOCKS = 16
BF16_SUBLANES = 16
TOK = 256
HALO = 8
INPROJ_SUBTILES = 4
QROWS = TOK // GRID_W
KROWS = QROWS + NA_KH_MAX
NKB = KROWS * GRID_W // TOK
SCORE_LOOKAHEAD = 1
ATTN_BATCHES = 2
CTX_BATCHES = 4
MOD_ROWS = 16
MOD_TN = 1152
VMEM_LIMIT = 56 * 1024 * 1024


def _rms_mod(x, g, scale, shift):
    ms = jnp.mean(x * x, axis=-1, keepdims=True)
    return (x * lax.rsqrt(ms + RMS_EPS) * g) * (1.0 + scale) + shift


def _mod_kernel(c_ref, w_ref, b_ref, *rest):
    n_cast = (len(rest) - 1) // 2
    o_ref = rest[n_cast]
    _run_cast_jobs(rest[:n_cast], rest[n_cast + 1:])
    c = c_ref[...]
    s = (c * jax.nn.sigmoid(c)).astype(BF16)
    o_ref[...] = jnp.dot(s, w_ref[...].astype(BF16), preferred_element_type=F32) + b_ref[...]


def _modulation(cc, w_mod, b_mod, cast_jobs):
    depth = w_mod.shape[0]
    n = N_MOD * D_MODEL
    per_layer = n // MOD_TN
    cast_in_specs, cast_out_specs, cast_out_shapes = _cast_job_specs(cast_jobs, depth * per_layer)
    outs = pl.pallas_call(
        _mod_kernel,
        grid=(depth * per_layer,),
        in_specs=[
            pl.BlockSpec((MOD_ROWS, D_MODEL), lambda i: (0, 0)),
            pl.BlockSpec((None, D_MODEL, MOD_TN), lambda i: (i // per_layer, 0, i % per_layer)),
            pl.BlockSpec((None, 1, MOD_TN), lambda i: (i // per_layer, 0, i % per_layer)),
        ] + cast_in_specs,
        out_specs=[pl.BlockSpec((None, MOD_ROWS, MOD_TN),
                                lambda i: (i // per_layer, 0, i % per_layer))] + cast_out_specs,
        out_shape=[jax.ShapeDtypeStruct((depth, MOD_ROWS, n), F32)] + cast_out_shapes,
        compiler_params=pltpu.CompilerParams(
            dimension_semantics=("arbitrary",), vmem_limit_bytes=VMEM_LIMIT),
        name="modulation",
    )(cc, w_mod, b_mod.reshape(depth, 1, n), *[src for src, _ in cast_jobs])
    return outs[0], outs[1:]


def _cast_job_specs(cast_jobs, n_steps):
    assert n_steps >= CAST_BLOCKS
    cast_blk = lambda i: jnp.minimum(i, CAST_BLOCKS - 1)
    in_specs, out_specs, out_shapes = [], [], []
    for src, lead in cast_jobs:
        rows, cols = src.shape[-2:]
        assert rows % (CAST_BLOCKS * BF16_SUBLANES) == 0 and len(lead) == src.ndim - 2
        blk = (rows // CAST_BLOCKS, cols)
        in_specs.append(pl.BlockSpec((None,) * len(lead) + blk,
                                     lambda i, lead=lead: lead + (cast_blk(i), 0)))
        out_specs.append(pl.BlockSpec(blk, lambda i: (cast_blk(i), 0)))
        out_shapes.append(jax.ShapeDtypeStruct((rows, cols), BF16))
    return in_specs, out_specs, out_shapes


def _run_cast_jobs(src_refs, dst_refs):
    for src_ref, dst_ref in zip(src_refs, dst_refs):
        dst_ref[...] = src_ref[...].astype(BF16)


def _ffn_kernel(xa_ref, xb_ref, mod_ref, g_ref, w1_ref, w2_ref, fg_ref, o_ref, *, sub, final,
                n_a_tiles, n_b_tiles):
    m = mod_ref[...]
    shift = m[3 * sub:3 * sub + 1]
    scale = m[3 * sub + 1:3 * sub + 2]
    gate = m[3 * sub + 2:3 * sub + 3]
    rows = FFN_TM // FFN_SUBTILES

    def load_x(r):
        sl = slice(r * rows, (r + 1) * rows)
        if n_b_tiles:
            return jnp.where(pl.program_id(0) < n_a_tiles, xa_ref[sl], xb_ref[sl])
        return xa_ref[sl]

    units = [(r, c) for r in range(FFN_SUBTILES) for c in range(len(FFN_CHUNKS))]
    xs, hbs, accs = {}, {}, {}

    def up(r, c):
        if r not in xs:
            xs[r] = load_x(r)
            hbs[r] = _rms_mod(xs[r], g_ref[...], scale, shift).astype(BF16)
        lo, hi = FFN_CHUNKS[c]
        a = jnp.dot(hbs[r], w1_ref[:, lo:hi], preferred_element_type=F32)
        b = jnp.dot(hbs[r], w1_ref[:, D_FF + lo:D_FF + hi], preferred_element_type=F32)
        return a, b

    ab = up(*units[0])
    for i, (r, c) in enumerate(units):
        a, b = ab
        if i + 1 < len(units):
            ab = up(*units[i + 1])
        lo, hi = FFN_CHUNKS[c]
        gg = (a * jax.nn.sigmoid(a) * b).astype(BF16)
        part = jnp.dot(gg, w2_ref[lo:hi, :], preferred_element_type=F32)
        accs[r] = part if r not in accs else accs[r] + part
    for r in range(FFN_SUBTILES):
        y = xs[r] + (0.5 * gate) * accs[r]
        if final:
            ms = jnp.mean(y * y, axis=-1, keepdims=True)
            y = y * lax.rsqrt(ms + RMS_EPS) * fg_ref[...]
        o_ref[r * rows:(r + 1) * rows] = y


def _ffn(xa, xb, xb_off, n_a_tiles, n_b_tiles, mod, norm_g4, w1, w2, final_g, *, layer, sub, final,
         lat_tiles_per_batch, n_batch):
    n_tiles = n_a_tiles + n_b_tiles
    const = lambda i: (0, 0)
    kern = functools.partial(_ffn_kernel, sub=sub, final=final, n_a_tiles=n_a_tiles,
                             n_b_tiles=n_b_tiles)
    return pl.pallas_call(
        kern,
        grid=(n_tiles,),
        in_specs=[
            pl.BlockSpec((FFN_TM, D_MODEL), lambda i: (jnp.minimum(i, n_a_tiles - 1), 0)),
            pl.BlockSpec((FFN_TM, D_MODEL), lambda i: (jnp.maximum(i - n_a_tiles, 0) + xb_off, 0)),
            pl.BlockSpec((None, None, N_MOD, D_MODEL),
                         lambda i: (layer, jnp.minimum(i // lat_tiles_per_batch, n_batch), 0, 0)),
            pl.BlockSpec((None, None, 1, D_MODEL), lambda i: (layer, sub, 0, 0)),
            pl.BlockSpec((D_MODEL, 2 * D_FF), const, pipeline_mode=pl.Buffered(1)),
            pl.BlockSpec((D_FF, D_MODEL), const, pipeline_mode=pl.Buffered(1)),
            pl.BlockSpec((1, D_MODEL), const),
        ],
        out_specs=pl.BlockSpec((FFN_TM, D_MODEL), lambda i: (i, 0)),
        out_shape=jax.ShapeDtypeStruct((n_tiles * FFN_TM, D_MODEL), F32),
        compiler_params=pltpu.CompilerParams(
            dimension_semantics=("arbitrary",), vmem_limit_bytes=VMEM_LIMIT),
        name="ffn",
    )(xa, xb, mod, norm_g4, w1, w2, final_g.reshape(1, D_MODEL))


def _inproj_kernel(xp_ref, x_ref, xn_ref, mod_ref, g_ref, w_ref, cw_ref, pw_ref, ps_ref, *rest,
                   n_lat_tiles, lat_tiles_per_seq):
    n_cast = (len(rest) - 1) // 2
    o_ref = rest[n_cast]
    _run_cast_jobs(rest[:n_cast], rest[n_cast + 1:])
    m = mod_ref[...]
    shift, scale = m[3:4], m[4:5]
    g = g_ref[...]
    ext = TOK + 2 * HALO

    def project(sub):
        lo = sub * TOK
        xp = xp_ref[...] if sub == 0 else x_ref[lo - HALO:lo]
        xn = xn_ref[...] if sub == INPROJ_SUBTILES - 1 else x_ref[lo + TOK:lo + TOK + HALO]
        xe = jnp.concatenate([x_ref[lo:lo + TOK], xn, xp], axis=0)
        he = _rms_mod(xe, g, scale, shift).astype(BF16)
        u = jnp.dot(he, w_ref[:, :OFF_Q], preferred_element_type=F32)
        qkv = jnp.dot(he[:TOK], w_ref[:, OFF_Q:], preferred_element_type=F32)
        o_ref[lo:lo + TOK, 2 * CONV_CH:2 * CONV_CH + NA_CH] = (
            qkv[:, :NA_CH] * (NA_HEAD_DIM ** -0.5)).astype(BF16)
        o_ref[lo:lo + TOK, 2 * CONV_CH + NA_CH:] = qkv[:, NA_CH:].astype(BF16)
        return u

    def mix(sub, u):
        t = pl.program_id(0) * INPROJ_SUBTILES + sub
        is_lat = t < n_lat_tiles
        pos = jnp.where(is_lat, t % lat_tiles_per_seq, 0)
        ntile = jnp.where(is_lat, lat_tiles_per_seq, 1)
        t0 = pos * TOK
        seq_len = ntile * TOK

        row = lax.broadcasted_iota(jnp.int32, (ext, 1), 0)
        next_end = jnp.where(pos == ntile - 1, TOK, TOK + HALO)
        prev_start = jnp.where(pos == 0, ext, TOK + HALO)
        u = jnp.where((row < next_end) | (row >= prev_start), u, 0.0)

        def shifted(a, s):
            return pltpu.roll(a, s % ext, 0)

        def centre(a):
            return a[:TOK]

        z = u[:, 2 * CONV_CH:3 * CONV_CH] * u[:, :CONV_CH]
        cw = cw_ref[...]
        conv = (centre(shifted(z, 1)) * cw[0:1] + centre(z) * cw[1:2]
                + centre(shifted(z, -1)) * cw[2:3])
        y_conv = centre(u[:, CONV_CH:2 * CONV_CH]) * conv

        v = u[:, 3 * CONV_CH:]
        a1 = v + shifted(v, 1)
        a2 = a1 + shifted(a1, 2)
        a3 = a2 + shifted(a2, 4)
        a4 = a3 + shifted(a3, 8)
        sums = (centre(a1), centre(shifted(a2, -1)), centre(shifted(a3, -3)),
                centre(shifted(a4, -7)))
        tpos = t0 + lax.broadcasted_iota(jnp.int32, (TOK, 1), 0)
        lane = lax.broadcasted_iota(jnp.int32, (TOK, POOL_CH), 1)
        mean = None
        for gi in reversed(range(len(POOL_WINDOWS))):
            w = POOL_WINDOWS[gi]
            left = w // 2
            right = w - 1 - left
            cnt = jnp.minimum(tpos + right + 1, seq_len) - jnp.maximum(tpos - left, 0)
            mg = sums[gi] / cnt.astype(F32)
            mean = mg if mean is None else jnp.where(lane < (gi + 1) * POOL_GROUP, mg, mean)
        dpool = (mean - centre(v)).astype(BF16)
        y_pool = jnp.dot(dpool, pw_ref[...], preferred_element_type=F32) * ps_ref[...]
        lo = sub * TOK
        o_ref[lo:lo + TOK, :CONV_CH] = y_conv.astype(BF16)
        o_ref[lo:lo + TOK, CONV_CH:CONV_CH + POOL_CH] = y_pool.astype(BF16)

    us = [project(sub) for sub in range(INPROJ_SUBTILES)]
    for sub, u in enumerate(us):
        mix(sub, u)


def _inproj(xs, mod, norm_g4, w_in, conv_w, pool_bd, pool_scale, cast_jobs, *, layer,
            n_lat_tiles, lat_tiles_per_seq, n_batch):
    nt = xs.shape[0]
    tm = INPROJ_SUBTILES * TOK
    n_steps = nt // tm
    hb = tm // HALO
    steps_per_seq = lat_tiles_per_seq // INPROJ_SUBTILES
    per_layer = lambda i: (layer, 0, 0)
    kern = functools.partial(_inproj_kernel, n_lat_tiles=n_lat_tiles,
                             lat_tiles_per_seq=lat_tiles_per_seq)
    cast_in_specs, cast_out_specs, cast_out_shapes = _cast_job_specs(cast_jobs, n_steps)
    outs = pl.pallas_call(
        kern,
        grid=(n_steps,),
        in_specs=[
            pl.BlockSpec((HALO, D_MODEL), lambda i: (jnp.maximum(i * hb - 1, 0), 0)),
            pl.BlockSpec((tm, D_MODEL), lambda i: (i, 0)),
            pl.BlockSpec((HALO, D_MODEL), lambda i: (jnp.minimum((i + 1) * hb, nt // HALO - 1), 0)),
            pl.BlockSpec((None, None, N_MOD, D_MODEL),
                         lambda i: (layer, jnp.minimum(i // steps_per_seq, n_batch), 0, 0)),
            pl.BlockSpec((None, None, 1, D_MODEL), lambda i: (layer, 1, 0, 0)),
            pl.BlockSpec((D_MODEL, D_IN), lambda i: (0, 0), pipeline_mode=pl.Buffered(1)),
            pl.BlockSpec((None, 3, CONV_CH), per_layer),
            pl.BlockSpec((None, POOL_CH, POOL_CH), per_layer),
            pl.BlockSpec((None, 1, POOL_CH), per_layer),
        ] + cast_in_specs,
        out_specs=[pl.BlockSpec((tm, 2 * D_MODEL), lambda i: (i, 0))] + cast_out_specs,
        out_shape=[jax.ShapeDtypeStruct((nt, 2 * D_MODEL), BF16)] + cast_out_shapes,
        compiler_params=pltpu.CompilerParams(
            dimension_semantics=("arbitrary",), vmem_limit_bytes=VMEM_LIMIT),
        name="inproj",
    )(xs, xs, xs, mod, norm_g4, w_in, conv_w, pool_bd, pool_scale, *[src for src, _ in cast_jobs])
    return outs[0], outs[1:]


def _pair_lanes(p, offset=0):
    return slice(offset + 2 * p * NA_HEAD_DIM, offset + 2 * (p + 1) * NA_HEAD_DIM)


def _softmax_probs(s):
    mx = jnp.max(s, axis=-1, keepdims=True)
    return jnp.exp(s - mx).astype(BF16)


def _mix_and_project(x, ycp, q_of, k_of, v_of, probs, wo_ref, gate):
    lane = lax.broadcasted_iota(jnp.int32, (1, 2 * NA_HEAD_DIM), 1)
    own = [jnp.where(lane < NA_HEAD_DIM, 1.0, 0.0).astype(BF16),
           jnp.where(lane < NA_HEAD_DIM, 0.0, 1.0).astype(BF16)]

    def scores(t):
        p, half = divmod(t, 2)
        return lax.dot_general(q_of(p) * own[half], k_of(p), (((1,), (1,)), ((), ())),
                               preferred_element_type=F32)

    def normalised(outs, first_pair, n_pairs):
        normed = []
        for p in range(first_pair, first_pair + n_pairs):
            o0, o1 = outs[2 * p], outs[2 * p + 1]
            lo = lax.broadcasted_iota(jnp.int32, o0.shape, 1) < NA_HEAD_DIM
            num = jnp.where(lo, o0, o1)
            den = jnp.where(lo, pltpu.roll(o0, NA_HEAD_DIM, 1), pltpu.roll(o1, NA_HEAD_DIM, 1))
            normed.append((num / den).astype(BF16))
        return normed

    outs = []
    pending = [scores(t) for t in range(SCORE_LOOKAHEAD)]
    for t in range(NA_HEADS):
        s = pending.pop(0)
        if t + SCORE_LOOKAHEAD < NA_HEADS:
            pending.append(scores(t + SCORE_LOOKAHEAD))
        vm = v_of(t // 2) * own[t % 2] + own[1 - t % 2]
        outs.append(jnp.dot(probs(t, s), vm, preferred_element_type=F32))
    y = jnp.concatenate([ycp] + normalised(outs, 0, NA_HEADS // 2), axis=-1)
    return x + gate * jnp.dot(y, wo_ref[...], preferred_element_type=F32)


def _build_bias_tables(tab, rp_ref):
    width = 2 * GRID_W
    n_off = 2 * NA_KH_MAX - 1
    lane = lax.broadcasted_iota(jnp.int32, (GRID_W, width), 1)
    qc = lax.broadcasted_iota(jnp.int32, (GRID_W, width), 0)
    kc = lane & (GRID_W - 1)
    cstart = jnp.clip(qc - NA_KW // 2, 0, GRID_W - NA_KW)
    col_ok = (kc >= cstart) & (kc < cstart + NA_KW)
    shift0 = (width - (NA_KW - 1)) % width

    def per_head(h, carry):
        for d in range(n_off):
            row = jnp.broadcast_to(rp_ref[h, d:d + 1, :], (GRID_W, width))
            for half in range(2):
                rolled = pltpu.roll(row, (shift0 + half * GRID_W) % width, 1, stride=1,
                                    stride_axis=0)
                tab[half, h, d] = jnp.where(col_ok, rolled, NEG_INF)
        for half in range(2):
            tab[half, h, n_off] = jnp.full((GRID_W, width), NEG_INF, F32)
        return carry

    lax.fori_loop(0, NA_HEADS, per_head, 0)


def _fill_bias(bias_buf, tab, rb, nrb):
    width = 2 * GRID_W
    n_off = 2 * NA_KH_MAX - 1
    lo_half = lax.broadcasted_iota(jnp.int32, (GRID_W, width), 1) < GRID_W
    is_first = rb == 0
    is_last = rb == nrb - 1
    base = jnp.where(is_first, 0, jnp.where(is_last, -2 * QROWS, -QROWS))

    def per_head(h, carry):
        for i in range(QROWS):
            jlo = jnp.where(is_first, 0, jnp.where(is_last, QROWS, i))
            for jj in range(KROWS // 2):
                halves = []
                for half, j in enumerate((2 * jj, 2 * jj + 1)):
                    ok = (j >= jlo) & (j < jlo + NA_KH_MAX)
                    idx = jnp.where(ok, jnp.clip(j - i + base + NA_KH_MAX - 1, 0, n_off - 1), n_off)
                    halves.append(tab[half, h, idx])
                bias_buf[h, i * GRID_W:(i + 1) * GRID_W, jj * width:(jj + 1) * width] = jnp.where(
                    lo_half, halves[0], halves[1])
        return carry

    lax.fori_loop(0, NA_HEADS, per_head, 0)


def _kv_window_copies(mix_hbm, kvbuf, sems, slot, rb, bp, nrb, n_batch):
    n_loc = NKB * TOK
    row0 = jnp.clip(rb - 1, 0, nrb - NKB) * TOK
    cols = pl.ds(2 * NA_CH, 2 * NA_CH)
    copies = []
    for g in range(ATTN_BATCHES):
        b = bp * ATTN_BATCHES + g
        copies.append(pltpu.make_async_copy(
            mix_hbm.at[b, pl.ds(row0, n_loc), cols], kvbuf.at[slot, g, pl.ds(0, n_loc)],
            sems.at[slot, g, 0]))
        copies.append(pltpu.make_async_copy(
            mix_hbm.at[n_batch, pl.ds(b * TOK, TOK), cols], kvbuf.at[slot, g, pl.ds(n_loc, TOK)],
            sems.at[slot, g, 1]))
    return copies


def _attn_kernel(x_ref, yq_ref, mix_hbm, rp_ref, mod_ref, wo_ref, o_ref,
                 kvbuf, kv_sems, bias_buf, bias_tab, *, nrb, n_bp, n_batch):
    rb = pl.program_id(0)
    bp = pl.program_id(1)
    n_loc = NKB * TOK
    step = rb * n_bp + bp
    slot = lax.rem(step, 2)

    @pl.when(step == 0)
    def _():
        for cp in _kv_window_copies(mix_hbm, kvbuf, kv_sems, slot, rb, bp, nrb, n_batch):
            cp.start()

    @pl.when(step + 1 < nrb * n_bp)
    def _():
        nxt = step + 1
        for cp in _kv_window_copies(mix_hbm, kvbuf, kv_sems, 1 - slot, nxt // n_bp,
                                    lax.rem(nxt, n_bp), nrb, n_batch):
            cp.start()

    @pl.when((pl.program_id(1) == 0) & (rb == 0))
    def _():
        _build_bias_tables(bias_tab, rp_ref)

    @pl.when((pl.program_id(1) == 0) & ((rb <= 1) | (rb == nrb - 1)))
    def _():
        _fill_bias(bias_buf, bias_tab, rb, nrb)

    for cp in _kv_window_copies(mix_hbm, kvbuf, kv_sems, slot, rb, bp, nrb, n_batch):
        cp.wait()

    def mix(probs):
        for g in range(ATTN_BATCHES):
            o_ref[g] = _mix_and_project(
                x_ref[g], yq_ref[g, :, :NA_CH], lambda p, g=g: yq_ref[g, :, _pair_lanes(p, NA_CH)],
                lambda p, g=g: kvbuf[slot, g, :, _pair_lanes(p)],
                lambda p, g=g: kvbuf[slot, g, :, _pair_lanes(p, NA_CH)], probs, wo_ref,
                mod_ref[g, 5:6])

    def probs_for(groups):
        def probs(h, s):
            out_rows = []
            for q0, nq, k0, nk in groups:
                rs = slice(q0 * GRID_W, (q0 + nq) * GRID_W)
                lo, hi = k0 * GRID_W, (k0 + nk) * GRID_W
                s_g = jnp.concatenate([s[rs, lo:hi] + bias_buf[h, rs, lo:hi], s[rs, n_loc:]],
                                      axis=-1)
                p_g = _softmax_probs(s_g)
                pieces = [p_g[:, :hi - lo], p_g[:, hi - lo:]]
                if lo:
                    pieces.insert(0, jnp.zeros((nq * GRID_W, lo), BF16))
                if hi < n_loc:
                    pieces.insert(-1, jnp.zeros((nq * GRID_W, n_loc - hi), BF16))
                out_rows.append(jnp.concatenate(pieces, axis=-1))
            return jnp.concatenate(out_rows, axis=0)
        return probs

    first_groups = [(0, QROWS, 0, NA_KH_MAX)]
    last_groups = [(0, QROWS, KROWS - NA_KH_MAX, NA_KH_MAX)]
    interior_groups = [(i, 1, 2 * (i // 2), 2 * ((i + NA_KH_MAX - 1) // 2 + 1) - 2 * (i // 2))
                       for i in range(QROWS)]
    pl.when(rb == 0)(lambda: mix(probs_for(first_groups)))
    pl.when(rb == nrb - 1)(lambda: mix(probs_for(last_groups)))
    pl.when((rb >= 1) & (rb < nrb - 1))(lambda: mix(probs_for(interior_groups)))


def _attention(xs, mix, rp, mod, w_out, *, layer, n_batch, rows):
    nrb = rows // QROWS
    seq = rows * GRID_W
    nt = xs.shape[0]
    g = ATTN_BATCHES
    assert nt % seq == 0 and n_batch % g == 0
    xs3 = xs.reshape(nt // seq, seq, D_MODEL)
    mix3 = mix.reshape(nt // seq, seq, mix.shape[1])
    n_bp = n_batch // g
    in_specs = [
        pl.BlockSpec((g, TOK, D_MODEL), lambda rb, bp: (bp, rb, 0)),
        pl.BlockSpec((g, TOK, 2 * NA_CH), lambda rb, bp: (bp, rb, 0)),
        pl.BlockSpec(memory_space=pl.ANY),
        pl.BlockSpec((None,) + rp.shape[1:], lambda rb, bp: (layer, 0, 0, 0)),
        pl.BlockSpec((None, g, N_MOD, D_MODEL), lambda rb, bp: (layer, bp, 0, 0)),
        pl.BlockSpec((D_MODEL, D_MODEL), lambda rb, bp: (0, 0)),
    ]
    out = pl.pallas_call(
        functools.partial(_attn_kernel, nrb=nrb, n_bp=n_bp, n_batch=n_batch),
        grid=(nrb, n_bp),
        in_specs=in_specs,
        out_specs=pl.BlockSpec((g, TOK, D_MODEL), lambda rb, bp: (bp, rb, 0)),
        out_shape=jax.ShapeDtypeStruct(xs3.shape, F32),
        scratch_shapes=[
            pltpu.VMEM((2, g, NKB * TOK + TOK, 2 * NA_CH), BF16),
            pltpu.SemaphoreType.DMA((2, g, 2)),
            pltpu.VMEM((NA_HEADS, TOK, NKB * TOK), F32),
            pltpu.VMEM((2, NA_HEADS, 2 * NA_KH_MAX, GRID_W, 2 * GRID_W), F32),
        ],
        input_output_aliases={0: 0},
        compiler_params=pltpu.CompilerParams(
            dimension_semantics=("arbitrary", "arbitrary"), vmem_limit_bytes=VMEM_LIMIT),
        name="attention",
    )(xs3, mix3, mix3, rp, mod, w_out)
    return out.reshape(nt, D_MODEL)


def _ctx_kernel(x_ref, yq_ref, kv_ref, mod_ref, wo_ref, o_ref):
    for g in range(CTX_BATCHES):
        rs = slice(g * TOK, (g + 1) * TOK)
        o_ref[rs] = _mix_and_project(
            x_ref[rs], yq_ref[rs, :NA_CH], lambda p, rs=rs: yq_ref[rs, _pair_lanes(p, NA_CH)],
            lambda p, rs=rs: kv_ref[rs, _pair_lanes(p)],
            lambda p, rs=rs: kv_ref[rs, _pair_lanes(p, NA_CH)],
            lambda h, s: _softmax_probs(s), wo_ref, mod_ref[5:6])


def _ctx_attention(xs, mix, mod, w_out, *, layer, n_batch, n_lat_tiles):
    rows = CTX_BATCHES * TOK
    assert n_batch % CTX_BATCHES == 0 and n_lat_tiles % CTX_BATCHES == 0
    blk = lambda b: n_lat_tiles // CTX_BATCHES + b
    return pl.pallas_call(
        _ctx_kernel,
        grid=(n_batch // CTX_BATCHES,),
        in_specs=[
            pl.BlockSpec((rows, D_MODEL), lambda b: (blk(b), 0)),
            pl.BlockSpec((rows, 2 * NA_CH), lambda b: (blk(b), 0)),
            pl.BlockSpec((rows, 2 * NA_CH), lambda b: (blk(b), 1)),
            pl.BlockSpec((None, None, N_MOD, D_MODEL), lambda b: (layer, n_batch, 0, 0)),
            pl.BlockSpec((D_MODEL, D_MODEL), lambda b: (0, 0)),
        ],
        out_specs=pl.BlockSpec((rows, D_MODEL), lambda b: (blk(b), 0)),
        out_shape=jax.ShapeDtypeStruct(xs.shape, F32),
        input_output_aliases={0: 0},
        compiler_params=pltpu.CompilerParams(
            dimension_semantics=("arbitrary",), vmem_limit_bytes=VMEM_LIMIT),
        name="ctx_attention",
    )(xs, mix, mix, mod, w_out)


def _pad_rpb(rpb):
    depth, heads, n_off, n_col = rpb.shape
    out = jnp.full((depth, heads, n_off + 1, 2 * GRID_W), NEG_INF, F32)
    return out.at[:, :, :n_off, :n_col].set(rpb)


def _pool_block_diag(pool_w):
    depth, n = pool_w.shape[:2]
    out = jnp.zeros((depth, POOL_CH, POOL_CH), pool_w.dtype)
    for gi in range(n):
        sl = slice(gi * POOL_GROUP, (gi + 1) * POOL_GROUP)
        out = out.at[:, sl, sl].set(pool_w[:, gi])
    return out


def kernel(x, c, ctx, c_ctx, w_mod, b_mod, norm_g, ffn_w_in, ffn_w_out, w_in, conv_w, pool_w,
           pool_scale, rpb, w_out, final_g):
    n_batch, seq, d = x.shape
    ctx_len = ctx.shape[1]
    depth = w_mod.shape[0]
    rows = seq // GRID_W
    assert d == D_MODEL and ctx_len == TOK and seq % FFN_TM == 0 and rows >= KROWS
    assert (n_batch * ctx_len) % FFN_TM == 0 and rows // QROWS >= NKB + 1
    assert seq % (INPROJ_SUBTILES * TOK) == 0 and (n_batch * ctx_len) % (INPROJ_SUBTILES * TOK) == 0
    assert NA_KH_MAX // 2 == QROWS and rpb.shape[2:] == (2 * NA_KH_MAX - 1, 2 * NA_KW - 1)
    assert n_batch + 1 <= MOD_ROWS and w_in.shape[-1] == D_IN and ffn_w_in.shape[-1] == 2 * D_FF
    n_lat = n_batch * seq
    n_lat_tiles = n_lat // TOK
    tiles_per_seq = seq // TOK
    lat_ffn_tiles = n_lat // FFN_TM
    ctx_ffn_tiles = n_batch * ctx_len // FFN_TM

    cc = jnp.zeros((MOD_ROWS, d), F32).at[:n_batch].set(c).at[n_batch].set(c_ctx)
    mod, (w1, w2, w_in_b, w_out_b) = _modulation(
        cc, w_mod, b_mod, [(ffn_w_in, (0, 0)), (ffn_w_out, (0, 0)), (w_in, (0,)), (w_out, (0,))])
    mod = mod.reshape(depth, MOD_ROWS, N_MOD, d)

    pool_bd = _pool_block_diag(pool_w).astype(BF16)
    pool_sc = pool_scale.reshape(depth, 1, POOL_CH)
    norm_g4 = norm_g.reshape(depth, 3, 1, d)
    rp = _pad_rpb(rpb)

    ffn_kw = dict(lat_tiles_per_batch=seq // FFN_TM, n_batch=n_batch)
    xa, xb, xb_off = x.reshape(n_lat, d), ctx.reshape(n_batch * ctx_len, d), 0
    for l in range(depth):
        last = l == depth - 1
        xs = _ffn(xa, xb, xb_off, lat_ffn_tiles, ctx_ffn_tiles, mod, norm_g4, w1, w2, final_g,
                  layer=l, sub=0, final=False, **ffn_kw)
        jobs = [(ffn_w_in, (l, 1)), (ffn_w_out, (l, 1))]
        if not last:
            jobs += [(ffn_w_in, (l + 1, 0)), (ffn_w_out, (l + 1, 0)), (w_in, (l + 1,)),
                     (w_out, (l + 1,))]
        mix, cast = _inproj(xs, mod, norm_g4, w_in_b, conv_w, pool_bd, pool_sc, jobs, layer=l,
                            n_lat_tiles=n_lat_tiles, lat_tiles_per_seq=tiles_per_seq,
                            n_batch=n_batch)
        xs = _attention(xs, mix, rp, mod, w_out_b, layer=l, n_batch=n_batch, rows=rows)
        if not last:
            xs = _ctx_attention(xs, mix, mod, w_out_b, layer=l, n_batch=n_batch,
                                n_lat_tiles=n_lat_tiles)
        xs = _ffn(xs, xs, lat_ffn_tiles, lat_ffn_tiles, 0 if last else ctx_ffn_tiles, mod, norm_g4,
                  cast[0], cast[1], final_g, layer=l, sub=2, final=last, **ffn_kw)
        if not last:
            w1, w2, w_in_b, w_out_b = cast[2:]
        xa, xb, xb_off = xs, xs, lat_ffn_tiles
    return xs.reshape(n_batch, seq, d)
```
